```python
import jax, jax.numpy as jnp
from jax import lax
import numpy as np

D_MODEL = 1024
BATCH = 2
SEQ = 8192
DEPTH = 1

CHUNK = 64
MEM_LEN = 256
MIX_DIM = D_MODEL
GLA_HEADS = 4
GLA_DV = MIX_DIM // 2 // GLA_HEADS
GLA_DK = GLA_DV // 2
GLA_GATE_RANK = 16
GLA_TAU = 16.0
SB_HEADS = 8
SB_HD = MIX_DIM // 2 // SB_HEADS
SB_BLOCK = 128
MEM_HEADS = 4
MEM_HD = D_MODEL // MEM_HEADS
N_EXPERTS = 32
TOP_K = 4
D_EXPERT = D_MODEL
SWIGLU_LIMIT = 7.0
SWIGLU_ALPHA = 1.702
EXPERT_BLOCK = 128
EPS = 1e-5

Q_G = GLA_HEADS * GLA_DK
V_G = GLA_HEADS * GLA_DV
SB_W = SB_HEADS * SB_HD
IN_SPLITS = (Q_G, Q_G, V_G, GLA_GATE_RANK, V_G, SB_W, SB_W, SB_W)
IN_DIM = sum(IN_SPLITS)
IN_OFFSETS = tuple(int(v) for v in np.cumsum(IN_SPLITS)[:-1])

kernel_name = "hybrid_gla_stickbreak_moe_block"


def rmsnorm(x, g):
    xf = x.astype(jnp.float32)
    y = xf * lax.rsqrt(jnp.mean(xf * xf, axis=-1, keepdims=True) + EPS)
    return (y * g.astype(jnp.float32)).astype(x.dtype)


def gla_group(q, k, v, log_a):
    B, S, H, dk = q.shape
    dv = v.shape[-1]
    nc = S // CHUNK
    qc = q.reshape(B, nc, CHUNK, H, dk)
    kc = k.astype(jnp.float32).reshape(B, nc, CHUNK, H, dk)
    vc = v.astype(jnp.float32).reshape(B, nc, CHUNK, H, dv)
    G = jnp.cumsum(log_a.astype(jnp.float32).reshape(B, nc, CHUNK, H, dk), axis=2)
    G_end = G[:, :, -1]
    k_dec = kc * jnp.exp(G_end[:, :, None] - G)
    U = jnp.einsum('bnshd,bnshv->bnhdv', k_dec, vc)
    decay = jnp.exp(G_end)

    def step(state, inp):
        d, u = inp
        state = d[..., None] * state + u
        return state, state

    init = jnp.zeros((B, H, dk, dv), jnp.float32)
    _, states = lax.scan(step, init, (jnp.moveaxis(decay, 1, 0), jnp.moveaxis(U, 1, 0)))
    states = jnp.moveaxis(states, 0, 1)
    o = jnp.einsum('bnthd,bnhdv->bnthv', qc.astype(jnp.float32), states)
    return o.reshape(B, S, H, dv).astype(q.dtype)


def stick_breaking_group(q, k, v):
    B, S, H, d = q.shape
    scale = d ** -0.5
    outs = []
    for i in range(S // SB_BLOCK):
        t0 = i * SB_BLOCK
        L = t0 + SB_BLOCK
        qb = q[:, t0:L]
        kb = k[:, :L]
        vb = v[:, :L].astype(jnp.float32)
        z = jnp.einsum('bthd,bshd->bhts', qb, kb).astype(jnp.float32) * scale
        t_pos = t0 + jnp.arange(SB_BLOCK)
        s_pos = jnp.arange(L)
        mask = s_pos[None, :] < t_pos[:, None]
        log_1mb = jnp.where(mask, jax.nn.log_sigmoid(-z), 0.0)
        suffix = lax.cumsum(log_1mb, axis=3, reverse=True) - log_1mb
        A = jnp.where(mask, jnp.exp(jax.nn.log_sigmoid(z) + suffix), 0.0)
        outs.append(jnp.einsum('bhts,bshd->bthd', A, vb))
    return jnp.concatenate(outs, axis=1).astype(q.dtype)


def memory_cross_attention(hq, m, w_mq, w_mk, w_mv, w_mo):
    B, S, _ = hq.shape
    M = m.shape[1]
    q = (hq @ w_mq).reshape(B, S, MEM_HEADS, MEM_HD)
    k = (m @ w_mk).reshape(B, M, MEM_HEADS, MEM_HD)
    v = (m @ w_mv).reshape(B, M, MEM_HEADS, MEM_HD)
    s = jnp.einsum('bthd,bmhd->bhtm', q, k).astype(jnp.float32) * (MEM_HD ** -0.5)
    p = jax.nn.softmax(s, axis=-1).astype(v.dtype)
    o = jnp.einsum('bhtm,bmhd->bthd', p, v).reshape(B, S, MEM_HEADS * MEM_HD)
    return o @ w_mo


def moe_ffn(h, w_router, b_router, w_gu, b_gu, w_down, b_down):
    B, S, D = h.shape
    N = B * S
    hf = h.reshape(N, D)
    logits = (hf @ w_router + b_router).astype(jnp.float32)
    top_vals, top_idx = lax.top_k(logits, TOP_K)
    gates = jax.nn.softmax(top_vals, axis=-1).astype(h.dtype)

    NK = N * TOP_K
    flat_e = top_idx.reshape(NK).astype(jnp.int32)
    flat_w = gates.reshape(NK)
    flat_tok = jnp.repeat(jnp.arange(N, dtype=jnp.int32), TOP_K)
    order = jnp.argsort(flat_e)
    sorted_e = flat_e[order]
    counts = jnp.bincount(flat_e, length=N_EXPERTS).astype(jnp.int32)
    padded = ((counts + EXPERT_BLOCK - 1) // EXPERT_BLOCK) * EXPERT_BLOCK
    pad_end = jnp.cumsum(padded)
    pad_start = pad_end - padded
    start = jnp.cumsum(counts) - counts
    rank = jnp.arange(NK, dtype=jnp.int32) - start[sorted_e]
    dest = pad_start[sorted_e] + rank

    n_blocks = -(-(NK + N_EXPERTS * (EXPERT_BLOCK - 1)) // EXPERT_BLOCK)
    M = n_blocks * EXPERT_BLOCK
    slot_tok = jnp.full((M,), N, jnp.int32).at[dest].set(flat_tok[order])
    slot_w = jnp.zeros((M,), h.dtype).at[dest].set(flat_w[order])
    block_start = jnp.arange(n_blocks, dtype=jnp.int32) * EXPERT_BLOCK
    block_expert = jnp.minimum(jnp.sum(block_start[:, None] >= pad_end[None, :], axis=1), N_EXPERTS - 1).astype(jnp.int32)

    h_pad = jnp.concatenate([hf, jnp.zeros((1, D), hf.dtype)], axis=0)
    xs = h_pad[slot_tok].reshape(n_blocks, EXPERT_BLOCK, D)

    def expert_block(args):
        xb, e = args
        gu = xb @ w_gu[e] + b_gu[e]
        gate, lin = gu[:, :D_EXPERT], gu[:, D_EXPERT:]
        gate = jnp.minimum(gate, SWIGLU_LIMIT)
        lin = jnp.clip(lin, -SWIGLU_LIMIT, SWIGLU_LIMIT)
        act = (lin + 1.0) * (gate * jax.nn.sigmoid(SWIGLU_ALPHA * gate))
        return act @ w_down[e] + b_down[e]

    ys = lax.map(expert_block, (xs, block_expert)).reshape(M, D) * slot_w[:, None]
    out = jax.ops.segment_sum(ys, slot_tok, num_segments=N + 1)[:N]
    return out.reshape(B, S, D)


def setup_inputs(seed: int = 0) -> dict:
    key = jax.random.key(seed)
    ks = jax.random.split(key, 24)

    def nrm(k, shape, scale):
        return jax.random.normal(k, shape, jnp.float32) * scale

    def gain(k, shape):
        return 1.0 + 0.05 * jax.random.normal(k, shape, jnp.float32)

    L = DEPTH
    return {
        'x': nrm(ks[0], (BATCH, SEQ, D_MODEL), 1.0),
        'mem': nrm(ks[1], (BATCH, MEM_LEN, D_MODEL), 1.0),
        'g_mix': gain(ks[2], (L, D_MODEL)),
        'w_in': nrm(ks[3], (L, D_MODEL, IN_DIM), D_MODEL ** -0.5),
        'w_gla_gate': nrm(ks[4], (L, GLA_GATE_RANK, Q_G), GLA_GATE_RANK ** -0.5),
        'b_gla_gate': nrm(ks[5], (L, Q_G), 0.5),
        'g_gla_head': gain(ks[6], (L, GLA_DV)),
        'w_out': nrm(ks[7], (L, MIX_DIM, D_MODEL), MIX_DIM ** -0.5),
        'g_mem_q': gain(ks[8], (L, D_MODEL)),
        'g_mem_kv': gain(ks[9], (L, D_MODEL)),
        'w_mq': nrm(ks[10], (L, D_MODEL, MEM_HEADS * MEM_HD), D_MODEL ** -0.5),
        'w_mk': nrm(ks[11], (L, D_MODEL, MEM_HEADS * MEM_HD), D_MODEL ** -0.5),
        'w_mv': nrm(ks[12], (L, D_MODEL, MEM_HEADS * MEM_HD), D_MODEL ** -0.5),
        'w_mo': nrm(ks[13], (L, MEM_HEADS * MEM_HD, D_MODEL), (MEM_HEADS * MEM_HD) ** -0.5),
        'g_ffn': gain(ks[14], (L, D_MODEL)),
        'w_router': nrm(ks[15], (L, D_MODEL, N_EXPERTS), D_MODEL ** -0.5),
        'b_router': nrm(ks[16], (L, N_EXPERTS), 0.01),
        'w_gu': nrm(ks[17], (L, N_EXPERTS, D_MODEL, 2 * D_EXPERT), D_MODEL ** -0.5),
        'b_gu': nrm(ks[18], (L, N_EXPERTS, 2 * D_EXPERT), 0.02),
        'w_down': nrm(ks[19], (L, N_EXPERTS, D_EXPERT, D_MODEL), D_EXPERT ** -0.5),
        'b_down': nrm(ks[20], (L, N_EXPERTS, D_MODEL), 0.02),
        'g_final': gain(ks[21], (D_MODEL,)),
    }


def reference(x, mem, g_mix, w_in, w_gla_gate, b_gla_gate, g_gla_head, w_out,
              g_mem_q, g_mem_kv, w_mq, w_mk, w_mv, w_mo,
              g_ffn, w_router, b_router, w_gu, b_gu, w_down, b_down, g_final):
    B, S, _ = x.shape
    for l in range(DEPTH):
        h = rmsnorm(x, g_mix[l])
        proj = h @ w_in[l]
        qg, kg, vg, glr, rg, qs, ks_, vs = jnp.split(proj, IN_OFFSETS, axis=-1)

        log_a = jax.nn.log_sigmoid((glr @ w_gla_gate[l] + b_gla_gate[l]).astype(jnp.float32)) / GLA_TAU
        o_gla = gla_group(qg.reshape(B, S, GLA_HEADS, GLA_DK) * (GLA_DK ** -0.5),
                          kg.reshape(B, S, GLA_HEADS, GLA_DK),
                          vg.reshape(B, S, GLA_HEADS, GLA_DV),
                          log_a.reshape(B, S, GLA_HEADS, GLA_DK))
        o_gla = rmsnorm(o_gla, g_gla_head[l]).reshape(B, S, V_G) * jax.nn.silu(rg)

        o_sb = stick_breaking_group(qs.reshape(B, S, SB_HEADS, SB_HD),
                                    ks_.reshape(B, S, SB_HEADS, SB_HD),
                                    vs.reshape(B, S, SB_HEADS, SB_HD)).reshape(B, S, SB_W)

        x = x + jnp.concatenate([o_gla, o_sb], axis=-1) @ w_out[l]

        x = x + memory_cross_attention(rmsnorm(x, g_mem_q[l]), rmsnorm(mem, g_mem_kv[l]),
                                       w_mq[l], w_mk[l], w_mv[l], w_mo[l])

        x = x + moe_ffn(rmsnorm(x, g_ffn[l]), w_router[l], b_router[l],
                        w_gu[l], b_gu[l], w_down[l], b_down[l])
    return rmsnorm(x, g_final)
```

```python
import functools

import jax
import jax.numpy as jnp
import numpy as np
from jax import lax
from jax.experimental import pallas as pl
from jax.experimental.pallas import tpu as pltpu

F32 = jnp.float32
BF16 = jnp.bfloat16

EPS = 1e-5
CHUNK = 64
GLA_HEADS = 4
GLA_DK = 64
GLA_DV = 128
GLA_GATE_RANK = 16
GLA_TAU = 16.0
SB_HEADS = 8
SB_HD = 64
MEM_HEADS = 4
N_EXPERTS = 32
TOP_K = 4
SWIGLU_LIMIT = 7.0
SWIGLU_ALPHA = 1.702

LANES = 128
Q_G = GLA_HEADS * GLA_DK
V_G = GLA_HEADS * GLA_DV
SB_W = SB_HEADS * SB_HD
GLA_SLAB = Q_G + Q_G + V_G + V_G + LANES

VMEM_LIMIT = 56 * 1024 * 1024

NT_DIMS = (((1,), (1,)), ((), ()))
TN_DIMS = (((0,), (0,)), ((), ()))


def _dot(a, b):
    return jnp.dot(a, b, preferred_element_type=F32)


def _split(x):
    hi = x.astype(BF16)
    lo = (x - hi.astype(F32)).astype(BF16)
    return hi, lo


def _dot_exact_lhs(a_bf16, b_f32):
    hi, lo = _split(b_f32)
    return _dot(a_bf16, hi) + _dot(a_bf16, lo)


def _dot3(a, b):
    ah, al = _split(a)
    bh, bl = _split(b)
    return _dot(ah, bh) + (_dot(ah, bl) + _dot(al, bh))


def _rmsnorm(x, g):
    return x * lax.rsqrt(jnp.mean(x * x, axis=-1, keepdims=True) + EPS) * g


def _softplus(z):
    return jnp.maximum(z, 0.0) + jnp.log1p(jnp.exp(-jnp.abs(z)))


def _params(sem, vmem=VMEM_LIMIT):
    return pltpu.CompilerParams(dimension_semantics=sem, vmem_limit_bytes=vmem)


def _inproj_kernel(x_ref, g_ref, wg_ref, wq_ref, wk_ref, wv_ref, slab_ref, q_ref, k_ref, v_ref):
    h = _rmsnorm(x_ref[...], g_ref[...]).astype(BF16)
    slab_ref[...] = _dot(h, wg_ref[...])
    q_ref[...] = (_dot(h, wq_ref[...]) * (SB_HD ** -0.5)).astype(BF16)
    k_ref[...] = _dot(h, wk_ref[...]).astype(BF16)
    v_ref[...] = _dot(h, wv_ref[...]).astype(BF16)


def _in_proj(x2d, g, wg, wq, wk, wv, tm):
    n, d = x2d.shape
    full = lambda a: pl.BlockSpec(a.shape, lambda i: (0, 0))
    return pl.pallas_call(
        _inproj_kernel,
        grid=(n // tm,),
        in_specs=[pl.BlockSpec((tm, d), lambda i: (i, 0)), full(g), full(wg), full(wq), full(wk), full(wv)],
        out_specs=[pl.BlockSpec((tm, GLA_SLAB), lambda i: (i, 0))] + [pl.BlockSpec((tm, SB_W), lambda i: (i, 0))] * 3,
        out_shape=[jax.ShapeDtypeStruct((n, GLA_SLAB), F32)] + [jax.ShapeDtypeStruct((n, SB_W), BF16)] * 3,
        compiler_params=_params(("parallel",)),
        name="in_proj",
    )(x2d, g, wg, wq, wk, wv)


def _gla_kernel(slab_ref, wgate_ref, bgate_ref, ghead_ref, umat_ref, o_ref, state_ref, *, n_chunks):
    @pl.when(pl.program_id(1) == 0)
    def _():
        state_ref[...] = jnp.zeros_like(state_ref)

    qg = slab_ref[:, 0:Q_G] * (GLA_DK ** -0.5)
    kg = slab_ref[:, Q_G:2 * Q_G]
    glr = slab_ref[:, 2 * Q_G + 2 * V_G:GLA_SLAB]
    log_a = -_softplus(-(_dot3(glr, wgate_ref[...]) + bgate_ref[...])) * (1.0 / GLA_TAU)
    to_end = _dot_exact_lhs(umat_ref[...], log_a)
    kdec = kg * jnp.exp(to_end)
    g_chunk = to_end + log_a

    lane = lax.broadcasted_iota(jnp.int32, (CHUNK, LANES), 1)
    ghead = ghead_ref[...]
    for c in range(n_chunks):
        rows = slice(c * CHUNK, (c + 1) * CHUNK)
        for h in range(GLA_HEADS):
            pair = slice((h // 2) * LANES, (h // 2 + 1) * LANES)
            mine = (lane >= GLA_DK) if h % 2 else (lane < GLA_DK)
            kd = jnp.where(mine, kdec[rows, pair], 0.0).astype(BF16)
            qm = jnp.where(mine, qg[rows, pair], 0.0).astype(BF16)
            vh = slab_ref[rows, 2 * Q_G + h * GLA_DV:2 * Q_G + (h + 1) * GLA_DV].astype(BF16)
            decay = jnp.exp(g_chunk[c * CHUNK:c * CHUNK + 1, pair])
            st = decay * state_ref[h] + lax.dot_general(vh, kd, TN_DIMS, preferred_element_type=F32)
            state_ref[h] = st
            o = lax.dot_general(qm, st.astype(BF16), NT_DIMS, preferred_element_type=F32)
            rg = slab_ref[rows, 2 * Q_G + V_G + h * GLA_DV:2 * Q_G + V_G + (h + 1) * GLA_DV]
            o = _rmsnorm(o, ghead) * (rg * jax.nn.sigmoid(rg))
            o_ref[rows, h * GLA_DV:(h + 1) * GLA_DV] = o.astype(BF16)


def _gla(slab, wgate, bgate, ghead, batch, seq, ts):
    n = slab.shape[0]
    n_chunks = ts // CHUNK
    r = np.arange(ts)
    umat = jnp.asarray((r[None, :] > r[:, None]) & (r[None, :] // CHUNK == r[:, None] // CHUNK), BF16)
    full = lambda a: pl.BlockSpec(a.shape, lambda b, i: (0,) * a.ndim)
    steps = seq // ts
    return pl.pallas_call(
        functools.partial(_gla_kernel, n_chunks=n_chunks),
        grid=(batch, steps),
        in_specs=[pl.BlockSpec((ts, GLA_SLAB), lambda b, i: (b * steps + i, 0)),
                  full(wgate), full(bgate), full(ghead), full(umat)],
        out_specs=pl.BlockSpec((ts, V_G), lambda b, i: (b * steps + i, 0)),
        out_shape=jax.ShapeDtypeStruct((n, V_G), BF16),
        scratch_shapes=[pltpu.VMEM((GLA_HEADS, GLA_DV, LANES), F32)],
        compiler_params=_params(("parallel", "arbitrary")),
        name="gla",
    )(slab, wgate, bgate, ghead, umat)


def _sb_kernel(q_ref, k_ref, v_ref, tmat_ref, o_ref, *, blk):
    i = pl.program_id(2)
    lane = lax.broadcasted_iota(jnp.int32, (blk, LANES), 1)
    row = lax.broadcasted_iota(jnp.int32, (blk, blk), 0)
    col = lax.broadcasted_iota(jnp.int32, (blk, blk), 1)
    causal = col < row
    q = q_ref[...]
    tmat = tmat_ref[...]

    def tile(qm, j, carry, acc, diag):
        start = pl.multiple_of(j * blk, blk)
        kj = k_ref[pl.ds(start, blk), :]
        vj = v_ref[pl.ds(start, blk), :]
        z = lax.dot_general(qm, kj, NT_DIMS, preferred_element_type=F32)
        sp = _softplus(z)
        log1mb = -sp
        if diag:
            log1mb = jnp.where(causal, log1mb, 0.0)
        hi, lo = _split(log1mb)
        sums = _dot(jnp.concatenate([hi, lo], axis=1), tmat)
        log_a = (z - sp) + sums[:, :blk] + carry
        a = jnp.exp(log_a)
        if diag:
            a = jnp.where(causal, a, 0.0)
        acc = acc + _dot(a.astype(BF16), vj)
        return carry + sums[:, blk:], acc

    def head(mine):
        qm = jnp.where(mine, q, jnp.zeros_like(q))
        zeros = jnp.zeros((blk, LANES), F32)
        carry, acc = tile(qm, i, zeros, zeros, True)

        def body(s, ca):
            return tile(qm, i - 1 - s, ca[0], ca[1], False)

        _, acc = lax.fori_loop(0, i, body, (carry, acc))
        return acc

    lo_head = lane < SB_HD
    o_ref[...] = jnp.where(lo_head, head(lo_head), head(jnp.logical_not(lo_head))).astype(BF16)


def _sb(q, k, v, batch, seq, blk):
    assert blk == LANES
    r = np.arange(blk)
    tri = (r[:, None] > r[None, :]).astype(np.float32)
    half = np.concatenate([tri, np.ones((blk, blk), np.float32)], axis=1)
    tmat = jnp.asarray(np.concatenate([half, half], axis=0), BF16)
    nq = seq // blk
    pairs = SB_W // LANES
    q3, k3, v3 = (a.reshape(batch, seq, SB_W) for a in (q, k, v))
    out = pl.pallas_call(
        functools.partial(_sb_kernel, blk=blk),
        grid=(batch, pairs, nq),
        in_specs=[pl.BlockSpec((None, blk, LANES), lambda b, p, i: (b, i, p)),
                  pl.BlockSpec((None, seq, LANES), lambda b, p, i: (b, 0, p)),
                  pl.BlockSpec((None, seq, LANES), lambda b, p, i: (b, 0, p)),
                  pl.BlockSpec(tmat.shape, lambda b, p, i: (0, 0))],
        out_specs=pl.BlockSpec((None, blk, LANES), lambda b, p, i: (b, i, p)),
        out_shape=jax.ShapeDtypeStruct((batch, seq, SB_W), BF16),
        compiler_params=_params(("parallel", "parallel", "arbitrary")),
        name="sb",
    )(q3, k3, v3, tmat)
    return out.reshape(batch * seq, SB_W)


def _memkv_kernel(m_ref, g_ref, wk_ref, wv_ref, k_ref, v_ref):
    hm = _rmsnorm(m_ref[...], g_ref[...]).astype(BF16)
    k_ref[...] = _dot(hm, wk_ref[...]).astype(BF16)
    v_ref[...] = _dot(hm, wv_ref[...]).astype(BF16)


def _mem_kv(mem, g, wk, wv):
    b, m, d = mem.shape
    full = lambda a: pl.BlockSpec(a.shape, lambda i: (0, 0))
    blk = pl.BlockSpec((None, m, d), lambda i: (i, 0, 0))
    return pl.pallas_call(
        _memkv_kernel,
        grid=(b,),
        in_specs=[blk, full(g), full(wk), full(wv)],
        out_specs=[blk, blk],
        out_shape=[jax.ShapeDtypeStruct((b, m, d), BF16)] * 2,
        compiler_params=_params(("parallel",)),
        name="mem_kv",
    )(mem, g, wk, wv)


def _mid_kernel(x_ref, og_ref, os_ref, wog_ref, wos_ref, gq_ref, wmq_ref, km_ref, vm_ref, wmo_ref, gf_ref,
                x2_ref, hf_ref):
    x1 = x_ref[...] + _dot(og_ref[...], wog_ref[...]) + _dot(os_ref[...], wos_ref[...])
    hq = _rmsnorm(x1, gq_ref[...]).astype(BF16)
    d = x1.shape[-1]
    hd = d // MEM_HEADS
    q = (_dot(hq, wmq_ref[...]) * (hd ** -0.5)).astype(BF16)
    outs = []
    for h in range(MEM_HEADS):
        cols = slice(h * hd, (h + 1) * hd)
        s = lax.dot_general(q[:, cols], km_ref[:, cols], NT_DIMS, preferred_element_type=F32)
        e = jnp.exp(s - jnp.max(s, axis=-1, keepdims=True))
        p = (e / jnp.sum(e, axis=-1, keepdims=True)).astype(BF16)
        outs.append(_dot(p, vm_ref[:, cols]).astype(BF16))
    x2 = x1 + _dot(jnp.concatenate(outs, axis=1), wmo_ref[...])
    x2_ref[...] = x2
    hf_ref[...] = _rmsnorm(x2, gf_ref[...])


def _mid(x2d, og, osb, wog, wos, gq, wmq, km, vm, wmo, gf, seq, tm):
    n, d = x2d.shape
    m = km.shape[1]
    per_batch = seq // tm
    full = lambda a: pl.BlockSpec(a.shape, lambda i: (0, 0))
    rows = lambda w: pl.BlockSpec((tm, w), lambda i: (i, 0))
    mem = pl.BlockSpec((None, m, d), lambda i: (i // per_batch, 0, 0))
    return pl.pallas_call(
        _mid_kernel,
        grid=(n // tm,),
        in_specs=[rows(d), rows(V_G), rows(SB_W), full(wog), full(wos), full(gq), full(wmq), mem, mem,
                  full(wmo), full(gf)],
        out_specs=[rows(d), rows(d)],
        out_shape=[jax.ShapeDtypeStruct((n, d), F32)] * 2,
        compiler_params=_params(("parallel",)),
        name="mid",
    )(x2d, og, osb, wog, wos, gq, wmq, km, vm, wmo, gf)


def _router_kernel(hf_ref, wrt_ref, br_ref, cmat_ref, idx_ref, gate_ref, rank_ref, cnt_ref, carry_ref, *, tm):
    @pl.when(pl.program_id(0) == 0)
    def _():
        carry_ref[...] = jnp.zeros_like(carry_ref)

    hh, hl = _split(hf_ref[...])
    wh, wl = _split(wrt_ref[...])
    nt = lambda a, b: lax.dot_general(a, b, NT_DIMS, preferred_element_type=F32)
    vals = nt(wh, hh) + (nt(wh, hl) + nt(wl, hh)) + br_ref[...]
    eidx = lax.broadcasted_iota(jnp.int32, (N_EXPERTS, tm), 0)
    tops, sels, hots = [], [], []
    for _ in range(TOP_K):
        m = jnp.max(vals, axis=0, keepdims=True)
        sel = jnp.min(jnp.where(vals == m, eidx, N_EXPERTS), axis=0, keepdims=True)
        hot = eidx == sel
        vals = jnp.where(hot, -jnp.inf, vals)
        tops.append(m)
        sels.append(sel)
        hots.append(hot)
    exps = [jnp.exp(t - tops[0]) for t in tops]
    denom = exps[0] + exps[1] + exps[2] + exps[3]
    chosen = jnp.zeros((N_EXPERTS, tm), F32)
    for hot in hots:
        chosen = chosen + hot.astype(F32)
    sums = _dot(chosen.astype(BF16), cmat_ref[...])
    before = sums[:, :tm] + carry_ref[...]
    for k in range(TOP_K):
        idx_ref[k:k + 1, :] = sels[k]
        gate_ref[k:k + 1, :] = exps[k] / denom
        rank_ref[k:k + 1, :] = jnp.sum(jnp.where(hots[k], before, 0.0), axis=0, keepdims=True).astype(jnp.int32)
    carry_ref[...] = carry_ref[...] + sums[:, tm:]
    cnt_ref[...] = carry_ref[...]


def _router(hf, wrt, br, tm):
    n, d = hf.shape
    r = np.arange(tm)
    cmat = jnp.asarray(np.concatenate([(r[:, None] < r[None, :]).astype(np.float32),
                                       np.ones((tm, tm), np.float32)], axis=1), BF16)
    full = lambda a: pl.BlockSpec(a.shape, lambda i: (0, 0))
    tok = pl.BlockSpec((TOP_K, tm), lambda i: (0, i))
    return pl.pallas_call(
        functools.partial(_router_kernel, tm=tm),
        grid=(n // tm,),
        in_specs=[pl.BlockSpec((tm, d), lambda i: (i, 0)), full(wrt), full(br), full(cmat)],
        out_specs=[tok, tok, tok, pl.BlockSpec((N_EXPERTS, tm), lambda i: (0, 0))],
        out_shape=[jax.ShapeDtypeStruct((TOP_K, n), jnp.int32), jax.ShapeDtypeStruct((TOP_K, n), F32),
                   jax.ShapeDtypeStruct((TOP_K, n), jnp.int32), jax.ShapeDtypeStruct((N_EXPERTS, tm), F32)],
        scratch_shapes=[pltpu.VMEM((N_EXPERTS, tm), F32)],
        compiler_params=_params(("arbitrary",)),
        name="router",
    )(hf, wrt, br, cmat)


def _dispatch_kernel(pos_ref, fill_start_ref, fill_n_ref, nvalid_ref, hf_ref, xs_ref, zero_ref, sem, *, tm, blk):
    i = pl.program_id(0)
    n_blocks = xs_ref.shape[0] // blk

    def block_copy(b):
        return pltpu.make_async_copy(zero_ref, xs_ref.at[pl.ds(pl.multiple_of(b * blk, blk), blk), :], sem)

    def row_copy(src, t, p):
        return pltpu.make_async_copy(src.at[pl.ds(t, 1), :], xs_ref.at[pl.ds(p, 1), :], sem)

    @pl.when(i == 0)
    def _():
        zero_ref[...] = jnp.zeros_like(zero_ref)
        for e in range(N_EXPERTS):
            def fill(r, _):
                row_copy(zero_ref, 0, fill_start_ref[e] + r).start()
                return 0
            lax.fori_loop(0, fill_n_ref[e], fill, 0)
        for e in range(N_EXPERTS):
            def drain(r, _):
                row_copy(zero_ref, 0, fill_start_ref[e] + r).wait()
                return 0
            lax.fori_loop(0, fill_n_ref[e], drain, 0)
        lax.fori_loop(nvalid_ref[0], n_blocks, lambda b, _: (block_copy(b).start(), 0)[1], 0)
        lax.fori_loop(nvalid_ref[0], n_blocks, lambda b, _: (block_copy(b).wait(), 0)[1], 0)

    base = i * tm

    def issue(t, _):
        for k in range(TOP_K):
            row_copy(hf_ref, t, pos_ref[(base + t) * TOP_K + k]).start()
        return 0

    def drain(t, _):
        for k in range(TOP_K):
            row_copy(hf_ref, t, pos_ref[(base + t) * TOP_K + k]).wait()
        return 0

    lax.fori_loop(0, tm, issue, 0)
    lax.fori_loop(0, tm, drain, 0)


def _dispatch(pos_flat, fill_start, fill_n, n_valid, hf, m_pad, tm, blk):
    n, d = hf.shape
    return pl.pallas_call(
        functools.partial(_dispatch_kernel, tm=tm, blk=blk),
        grid_spec=pltpu.PrefetchScalarGridSpec(
            num_scalar_prefetch=4,
            grid=(n // tm,),
            in_specs=[pl.BlockSpec((tm, d), lambda i, *_: (i, 0))],
            out_specs=pl.BlockSpec(memory_space=pl.ANY),
            scratch_shapes=[pltpu.VMEM((blk, d), F32), pltpu.SemaphoreType.DMA],
        ),
        out_shape=jax.ShapeDtypeStruct((m_pad, d), F32),
        compiler_params=_params(("arbitrary",)),
        name="dispatch",
    )(pos_flat, fill_start, fill_n, n_valid, hf)


def _experts_kernel(blk_ref, exp_ref, nvalid_ref, xs_ref, wgu_ref, bgu_ref, wd_ref, bd_ref, ys_ref,
                    wgu_bf, wd_bf):
    i = pl.program_id(0)
    valid = i < nvalid_ref[0]
    changed = jnp.logical_or(i == 0, exp_ref[i] != exp_ref[jnp.maximum(i - 1, 0)])

    @pl.when(jnp.logical_and(valid, changed))
    def _():
        wgu_bf[...] = wgu_ref[...].astype(BF16)
        wd_bf[...] = wd_ref[...].astype(BF16)

    @pl.when(valid)
    def _():
        f = wd_bf.shape[0]
        gu = _dot(xs_ref[...].astype(BF16), wgu_bf[...]) + bgu_ref[...]
        gate = jnp.minimum(gu[:, :f], SWIGLU_LIMIT)
        lin = jnp.clip(gu[:, f:], -SWIGLU_LIMIT, SWIGLU_LIMIT)
        act = (lin + 1.0) * (gate * jax.nn.sigmoid(SWIGLU_ALPHA * gate))
        ys_ref[...] = _dot(act.astype(BF16), wd_bf[...]) + bd_ref[...]

    @pl.when(jnp.logical_not(valid))
    def _():
        ys_ref[...] = jnp.zeros_like(ys_ref)


def _experts(blk_idx, blk_exp, n_valid, xs, w_gu, b_gu, w_down, b_down, blk):
    m_pad, d = xs.shape
    e, _, f2 = w_gu.shape
    f = f2 // 2
    n_steps = m_pad // blk
    return pl.pallas_call(
        _experts_kernel,
        grid_spec=pltpu.PrefetchScalarGridSpec(
            num_scalar_prefetch=3,
            grid=(n_steps,),
            in_specs=[pl.BlockSpec((blk, d), lambda i, b, x, nv: (b[i], 0)),
                      pl.BlockSpec((None, d, f2), lambda i, b, x, nv: (x[i], 0, 0)),
                      pl.BlockSpec((None, 1, f2), lambda i, b, x, nv: (x[i], 0, 0)),
                      pl.BlockSpec((None, f, d), lambda i, b, x, nv: (x[i], 0, 0)),
                      pl.BlockSpec((None, 1, d), lambda i, b, x, nv: (x[i], 0, 0))],
            out_specs=pl.BlockSpec((blk, d), lambda i, b, x, nv: (b[i], 0)),
            scratch_shapes=[pltpu.VMEM((d, f2), BF16), pltpu.VMEM((f, d), BF16)],
        ),
        out_shape=jax.ShapeDtypeStruct((m_pad, d), F32),
        compiler_params=_params(("arbitrary",)),
        name="experts",
    )(blk_idx, blk_exp, n_valid, xs, w_gu, b_gu.reshape(e, 1, f2), w_down, b_down.reshape(e, 1, d))


def _combine_kernel(pos_ref, ys_ref, x2_ref, gate_ref, gfin_ref, y_ref, buf_ref, sems, *, tm):
    i = pl.program_id(0)
    n_steps = pl.num_programs(0)

    def row_copy(step, slot, t, k):
        p = pos_ref[(step * tm + t) * TOP_K + k]
        return pltpu.make_async_copy(ys_ref.at[pl.ds(p, 1), :], buf_ref.at[slot, k, pl.ds(t, 1), :], sems.at[slot])

    def issue(step, slot):
        def body(t, _):
            for k in range(TOP_K):
                row_copy(step, slot, t, k).start()
            return 0
        lax.fori_loop(0, tm, body, 0)

    @pl.when(i == 0)
    def _():
        issue(0, 0)

    @pl.when(i + 1 < n_steps)
    def _():
        issue(i + 1, (i + 1) % 2)

    slot = i % 2

    def drain(t, _):
        for k in range(TOP_K):
            row_copy(i, slot, t, k).wait()
        return 0

    lax.fori_loop(0, tm, drain, 0)
    acc = x2_ref[...]
    for k in range(TOP_K):
        acc = acc + gate_ref[:, k:k + 1] * buf_ref[slot, k]
    y_ref[...] = _rmsnorm(acc, gfin_ref[...])


def _combine(pos_flat, ys, x2, gates, gfin, tm):
    n, d = x2.shape
    return pl.pallas_call(
        functools.partial(_combine_kernel, tm=tm),
        grid_spec=pltpu.PrefetchScalarGridSpec(
            num_scalar_prefetch=1,
            grid=(n // tm,),
            in_specs=[pl.BlockSpec(memory_space=pl.ANY),
                      pl.BlockSpec((tm, d), lambda i, p: (i, 0)),
                      pl.BlockSpec((tm, TOP_K), lambda i, p: (i, 0)),
                      pl.BlockSpec(gfin.shape, lambda i, p: (0, 0))],
            out_specs=pl.BlockSpec((tm, d), lambda i, p: (i, 0)),
            scratch_shapes=[pltpu.VMEM((2, TOP_K, tm, d), F32), pltpu.SemaphoreType.DMA((2,))],
        ),
        out_shape=jax.ShapeDtypeStruct((n, d), F32),
        compiler_params=_params(("arbitrary",)),
        name="combine",
    )(pos_flat, ys, x2, gates, gfin)


def _pick(n, pref):
    t = min(pref, n)
    while n % t:
        t //= 2
    return t


def _layer(x2d, mem, batch, seq, g_mix, w_in, w_gla_gate, b_gla_gate, g_gla_head, w_out, g_mem_q, g_mem_kv,
           w_mq, w_mk, w_mv, w_mo, g_ffn, w_router, b_router, w_gu, b_gu, w_down, b_down, g_out):
    n, d = x2d.shape
    row = lambda v: v.reshape(1, -1).astype(F32)

    o_qg, o_kg, o_vg, o_lr, o_rg, o_qs, o_ks, o_vs = np.cumsum((0, Q_G, Q_G, V_G, GLA_GATE_RANK, V_G, SB_W, SB_W))
    lr_pad = jnp.zeros((d, LANES - GLA_GATE_RANK), w_in.dtype)
    wg = jnp.concatenate([w_in[:, o_qg:o_vg + V_G], w_in[:, o_rg:o_rg + V_G],
                          w_in[:, o_lr:o_lr + GLA_GATE_RANK], lr_pad], axis=1).astype(BF16)
    wq, wk, wv = (w_in[:, o:o + SB_W].astype(BF16) for o in (o_qs, o_ks, o_vs))
    wgate = jnp.concatenate([w_gla_gate, jnp.zeros((LANES - GLA_GATE_RANK, Q_G), F32)], axis=0)

    slab, qs, ks, vs = _in_proj(x2d, row(g_mix), wg, wq, wk, wv, _pick(n, 512))
    o_gla = _gla(slab, wgate, row(b_gla_gate), row(g_gla_head), batch, seq, _pick(seq, 512))
    o_sb = _sb(qs, ks, vs, batch, seq, LANES)
    km, vm = _mem_kv(mem, row(g_mem_kv), w_mk.astype(BF16), w_mv.astype(BF16))
    w_out_bf = w_out.astype(BF16)
    x2, hf = _mid(x2d, o_gla, o_sb, w_out_bf[:V_G], w_out_bf[V_G:], row(g_mem_q), w_mq.astype(BF16), km, vm,
                  w_mo.astype(BF16), row(g_ffn), seq, _pick(seq, 512))

    tm_r = _pick(n, 256)
    idx, gates, rank, cnt = _router(hf, w_router.T, b_router.reshape(-1, 1), tm_r)

    blk = 256
    counts = cnt[:, 0].astype(jnp.int32)
    padded = ((counts + blk - 1) // blk) * blk
    pad_end = jnp.cumsum(padded)
    pad_start = pad_end - padded
    pos = (pad_start[idx] + rank).T.reshape(-1)
    n_steps = -(-(n * TOP_K + N_EXPERTS * (blk - 1)) // blk)
    m_pad = n_steps * blk
    n_valid = pad_end[-1] // blk
    blk_idx = jnp.arange(n_steps, dtype=jnp.int32)
    used = jnp.minimum(blk_idx, n_valid - 1)
    blk_exp = jnp.minimum(jnp.sum(used[:, None] * blk >= pad_end[None, :], axis=1), N_EXPERTS - 1).astype(jnp.int32)
    n_valid = n_valid.reshape(1)

    xs = _dispatch(pos, pad_start + counts, padded - counts, n_valid, hf, m_pad, _pick(n, 256), blk)
    ys = _experts(blk_idx, blk_exp, n_valid, xs, w_gu, b_gu, w_down, b_down, blk)
    return _combine(pos, ys, x2, gates.T, row(g_out), _pick(n, 128))


def kernel(x, mem, g_mix, w_in, w_gla_gate, b_gla_gate, g_gla_head, w_out, g_mem_q, g_mem_kv, w_mq, w_mk, w_mv,
           w_mo, g_ffn, w_router, b_router, w_gu, b_gu, w_down, b_down, g_final):
    batch, seq, d = x.shape
    depth = g_mix.shape[0]
    assert depth == 1, "the final rmsnorm is fused into the single layer's combine step"
    y = _layer(x.reshape(batch * seq, d), mem, batch, seq, g_mix[0], w_in[0], w_gla_gate[0], b_gla_gate[0],
               g_gla_head[0], w_out[0], g_mem_q[0], g_mem_kv[0], w_mq[0], w_mk[0], w_mv[0], w_mo[0], g_ffn[0],
               w_router[0], b_router[0], w_gu[0], b_gu[0], w_down[0], b_down[0], g_final)
    return y.reshape(batch, seq, d)
```

```python
import functools

import jax
import jax.numpy as jnp
import numpy as np
from jax import lax
from jax.experimental import pallas as pl
from jax.experimental.pallas import tpu as pltpu

F32 = jnp.float32
BF16 = jnp.bfloat16

EPS = 1e-5
CHUNK = 64
GLA_HEADS = 4
GLA_DK = 64
GLA_DV = 128
GLA_GATE_RANK = 16
GLA_TAU = 16.0
SB_HEADS = 8
SB_HD = 64
MEM_HEADS = 4
N_EXPERTS = 32
TOP_K = 4
SWIGLU_LIMIT = 7.0
SWIGLU_ALPHA = 1.702
SB_LOG_UNDERFLOW = -104.0

LANES = 128
Q_G = GLA_HEADS * GLA_DK
V_G = GLA_HEADS * GLA_DV
SB_W = SB_HEADS * SB_HD
GLA_SLAB = Q_G + Q_G + V_G + V_G + LANES

VMEM_LIMIT = 56 * 1024 * 1024

NT_DIMS = (((1,), (1,)), ((), ()))
TN_DIMS = (((0,), (0,)), ((), ()))


def _dot(a, b):
    return jnp.dot(a, b, preferred_element_type=F32)


def _split(x):
    hi = x.astype(BF16)
    lo = (x - hi.astype(F32)).astype(BF16)
    return hi, lo


def _dot_exact_lhs(a_bf16, b_f32):
    hi, lo = _split(b_f32)
    return _dot(a_bf16, hi) + _dot(a_bf16, lo)


def _dot3(a, b):
    ah, al = _split(a)
    bh, bl = _split(b)
    return _dot(ah, bh) + (_dot(ah, bl) + _dot(al, bh))


def _rmsnorm(x, g):
    return x * lax.rsqrt(jnp.mean(x * x, axis=-1, keepdims=True) + EPS) * g


def _softplus(z):
    return jnp.maximum(z, 0.0) + jnp.log1p(jnp.exp(-jnp.abs(z)))


def _params(sem, vmem=VMEM_LIMIT):
    return pltpu.CompilerParams(dimension_semantics=sem, vmem_limit_bytes=vmem)


def _inproj_kernel(x_ref, g_ref, wg_ref, wq_ref, wk_ref, wv_ref, slab_ref, q_ref, k_ref, v_ref):
    h = _rmsnorm(x_ref[...], g_ref[...]).astype(BF16)
    slab_ref[...] = _dot(h, wg_ref[...])
    q_ref[...] = (_dot(h, wq_ref[...]) * (SB_HD ** -0.5)).astype(BF16)
    k_ref[...] = _dot(h, wk_ref[...]).astype(BF16)
    v_ref[...] = _dot(h, wv_ref[...]).astype(BF16)


def _in_proj(x2d, g, wg, wq, wk, wv, tm):
    n, d = x2d.shape
    full = lambda a: pl.BlockSpec(a.shape, lambda i: (0, 0))
    return pl.pallas_call(
        _inproj_kernel,
        grid=(n // tm,),
        in_specs=[pl.BlockSpec((tm, d), lambda i: (i, 0)), full(g), full(wg), full(wq), full(wk), full(wv)],
        out_specs=[pl.BlockSpec((tm, GLA_SLAB), lambda i: (i, 0))] + [pl.BlockSpec((tm, SB_W), lambda i: (i, 0))] * 3,
        out_shape=[jax.ShapeDtypeStruct((n, GLA_SLAB), F32)] + [jax.ShapeDtypeStruct((n, SB_W), BF16)] * 3,
        compiler_params=_params(("parallel",)),
        name="in_proj",
    )(x2d, g, wg, wq, wk, wv)


def _gla_kernel(slab_ref, wgate_ref, bgate_ref, ghead_ref, umat_ref, o_ref, state_ref, *, n_chunks):
    @pl.when(pl.program_id(1) == 0)
    def _():
        state_ref[...] = jnp.zeros_like(state_ref)

    qg = slab_ref[:, 0:Q_G] * (GLA_DK ** -0.5)
    kg = slab_ref[:, Q_G:2 * Q_G]
    glr = slab_ref[:, 2 * Q_G + 2 * V_G:GLA_SLAB]
    log_a = -_softplus(-(_dot3(glr, wgate_ref[...]) + bgate_ref[...])) * (1.0 / GLA_TAU)
    to_end = _dot_exact_lhs(umat_ref[...], log_a)
    kdec = kg * jnp.exp(to_end)
    g_chunk = to_end + log_a

    lane = lax.broadcasted_iota(jnp.int32, (CHUNK, LANES), 1)
    ghead = ghead_ref[...]
    for c in range(n_chunks):
        rows = slice(c * CHUNK, (c + 1) * CHUNK)
        for h in range(GLA_HEADS):
            pair = slice((h // 2) * LANES, (h // 2 + 1) * LANES)
            mine = (lane >= GLA_DK) if h % 2 else (lane < GLA_DK)
            kd = jnp.where(mine, kdec[rows, pair], 0.0).astype(BF16)
            qm = jnp.where(mine, qg[rows, pair], 0.0).astype(BF16)
            vh = slab_ref[rows, 2 * Q_G + h * GLA_DV:2 * Q_G + (h + 1) * GLA_DV].astype(BF16)
            decay = jnp.exp(g_chunk[c * CHUNK:c * CHUNK + 1, pair])
            st = decay * state_ref[h] + lax.dot_general(vh, kd, TN_DIMS, preferred_element_type=F32)
            state_ref[h] = st
            o = lax.dot_general(qm, st.astype(BF16), NT_DIMS, preferred_element_type=F32)
            rg = slab_ref[rows, 2 * Q_G + V_G + h * GLA_DV:2 * Q_G + V_G + (h + 1) * GLA_DV]
            o = _rmsnorm(o, ghead) * (rg * jax.nn.sigmoid(rg))
            o_ref[rows, h * GLA_DV:(h + 1) * GLA_DV] = o.astype(BF16)


def _gla(slab, wgate, bgate, ghead, batch, seq, ts):
    n = slab.shape[0]
    n_chunks = ts // CHUNK
    r = np.arange(ts)
    umat = jnp.asarray((r[None, :] > r[:, None]) & (r[None, :] // CHUNK == r[:, None] // CHUNK), BF16)
    full = lambda a: pl.BlockSpec(a.shape, lambda b, i: (0,) * a.ndim)
    steps = seq // ts
    return pl.pallas_call(
        functools.partial(_gla_kernel, n_chunks=n_chunks),
        grid=(batch, steps),
        in_specs=[pl.BlockSpec((ts, GLA_SLAB), lambda b, i: (b * steps + i, 0)),
                  full(wgate), full(bgate), full(ghead), full(umat)],
        out_specs=pl.BlockSpec((ts, V_G), lambda b, i: (b * steps + i, 0)),
        out_shape=jax.ShapeDtypeStruct((n, V_G), BF16),
        scratch_shapes=[pltpu.VMEM((GLA_HEADS, GLA_DV, LANES), F32)],
        compiler_params=_params(("parallel", "arbitrary")),
        name="gla",
    )(slab, wgate, bgate, ghead, umat)


def _sb_kernel(q_ref, k_ref, v_ref, tmat_ref, o_ref, acc_ref, carry_ref, *, blk):
    i = pl.program_id(2)
    sub = blk // 2
    lane = lax.broadcasted_iota(jnp.int32, (blk, LANES), 1)
    causal = lax.broadcasted_iota(jnp.int32, (blk, blk), 1) < lax.broadcasted_iota(jnp.int32, (blk, blk), 0)
    q = q_ref[...]
    lo_head = lane < SB_HD
    q_heads = (jnp.where(lo_head, q, jnp.zeros_like(q)), jnp.where(lo_head, jnp.zeros_like(q), q))
    tmat = tmat_ref[...]

    def chunk(j, diag):
        start = pl.multiple_of(j * blk, blk)
        kj = k_ref[pl.ds(start, blk), :]
        vj = v_ref[pl.ds(start, blk), :]
        worst = None
        for h in range(2):
            z = lax.dot_general(q_heads[h], kj, NT_DIMS, preferred_element_type=F32)
            sp = _softplus(z)
            log1mb = -sp
            if diag:
                log1mb = jnp.where(causal, log1mb, 0.0)
            hi, lo = _split(log1mb)
            right = _dot(jnp.concatenate([hi[:, sub:], lo[:, sub:]], axis=1), tmat)
            left = _dot(jnp.concatenate([hi[:, :sub], lo[:, :sub]], axis=1), tmat)
            log_sig = z - sp
            if diag:
                after_right = right[:, sub:]
                log_a_right = log_sig[:, sub:] + right[:, :sub]
            else:
                carry = carry_ref[h]
                after_right = carry + right[:, sub:]
                log_a_right = log_sig[:, sub:] + right[:, :sub] + carry
            log_a_left = log_sig[:, :sub] + left[:, :sub] + after_right
            a = jnp.exp(jnp.concatenate([log_a_left, log_a_right], axis=1))
            if diag:
                a = jnp.where(causal, a, 0.0)
            pv = _dot(a.astype(BF16), vj)
            acc_ref[h] = pv if diag else acc_ref[h] + pv
            remaining = after_right + left[:, sub:]
            carry_ref[h] = remaining
            top = jnp.max(remaining)
            worst = top if worst is None else jnp.maximum(worst, top)
        return worst

    def more(state):
        return jnp.logical_and(state[0] >= 0, state[1] > SB_LOG_UNDERFLOW)

    lax.while_loop(more, lambda state: (state[0] - 1, chunk(state[0], False)), (i - 1, chunk(i, True)))
    o_ref[...] = jnp.where(lo_head, acc_ref[0], acc_ref[1]).astype(BF16)


def _sb(q, k, v, batch, seq, blk):
    sub = blk // 2
    assert sub == LANES
    r = np.arange(sub)
    tri = (r[:, None] > r[None, :]).astype(np.float32)
    half = np.concatenate([tri, np.ones((sub, sub), np.float32)], axis=1)
    tmat = jnp.asarray(np.concatenate([half, half], axis=0), BF16)
    nq = seq // blk
    pairs = SB_W // LANES
    q3, k3, v3 = (a.reshape(batch, seq, SB_W) for a in (q, k, v))
    out = pl.pallas_call(
        functools.partial(_sb_kernel, blk=blk),
        grid=(batch, pairs, nq),
        in_specs=[pl.BlockSpec((None, blk, LANES), lambda b, p, i: (b, i, p)),
                  pl.BlockSpec((None, seq, LANES), lambda b, p, i: (b, 0, p)),
                  pl.BlockSpec((None, seq, LANES), lambda b, p, i: (b, 0, p)),
                  pl.BlockSpec(tmat.shape, lambda b, p, i: (0, 0))],
        out_specs=pl.BlockSpec((None, blk, LANES), lambda b, p, i: (b, i, p)),
        out_shape=jax.ShapeDtypeStruct((batch, seq, SB_W), BF16),
        scratch_shapes=[pltpu.VMEM((2, blk, LANES), F32), pltpu.VMEM((2, blk, LANES), F32)],
        compiler_params=_params(("parallel", "parallel", "arbitrary")),
        name="sb",
    )(q3, k3, v3, tmat)
    return out.reshape(batch * seq, SB_W)


def _memkv_kernel(m_ref, g_ref, wk_ref, wv_ref, k_ref, v_ref):
    hm = _rmsnorm(m_ref[...], g_ref[...]).astype(BF16)
    k_ref[...] = _dot(hm, wk_ref[...]).astype(BF16)
    v_ref[...] = _dot(hm, wv_ref[...]).astype(BF16)


def _mem_kv(mem, g, wk, wv):
    b, m, d = mem.shape
    full = lambda a: pl.BlockSpec(a.shape, lambda i: (0, 0))
    blk = pl.BlockSpec((None, m, d), lambda i: (i, 0, 0))
    return pl.pallas_call(
        _memkv_kernel,
        grid=(b,),
        in_specs=[blk, full(g), full(wk), full(wv)],
        out_specs=[blk, blk],
        out_shape=[jax.ShapeDtypeStruct((b, m, d), BF16)] * 2,
        compiler_params=_params(("parallel",)),
        name="mem_kv",
    )(mem, g, wk, wv)


def _mid_kernel(x_ref, og_ref, os_ref, wog_ref, wos_ref, gq_ref, wmq_ref, km_ref, vm_ref, wmo_ref, gf_ref,
                x2_ref, hf_ref):
    x1 = x_ref[...] + _dot(og_ref[...], wog_ref[...]) + _dot(os_ref[...], wos_ref[...])
    hq = _rmsnorm(x1, gq_ref[...]).astype(BF16)
    d = x1.shape[-1]
    hd = d // MEM_HEADS
    q = (_dot(hq, wmq_ref[...]) * (hd ** -0.5)).astype(BF16)
    outs = []
    for h in range(MEM_HEADS):
        cols = slice(h * hd, (h + 1) * hd)
        s = lax.dot_general(q[:, cols], km_ref[:, cols], NT_DIMS, preferred_element_type=F32)
        e = jnp.exp(s - jnp.max(s, axis=-1, keepdims=True))
        p = (e / jnp.sum(e, axis=-1, keepdims=True)).astype(BF16)
        outs.append(_dot(p, vm_ref[:, cols]).astype(BF16))
    x2 = x1 + _dot(jnp.concatenate(outs, axis=1), wmo_ref[...])
    x2_ref[...] = x2
    hf_ref[...] = _rmsnorm(x2, gf_ref[...])


def _mid(x2d, og, osb, wog, wos, gq, wmq, km, vm, wmo, gf, seq, tm):
    n, d = x2d.shape
    m = km.shape[1]
    per_batch = seq // tm
    full = lambda a: pl.BlockSpec(a.shape, lambda i: (0, 0))
    rows = lambda w: pl.BlockSpec((tm, w), lambda i: (i, 0))
    mem = pl.BlockSpec((None, m, d), lambda i: (i // per_batch, 0, 0))
    return pl.pallas_call(
        _mid_kernel,
        grid=(n // tm,),
        in_specs=[rows(d), rows(V_G), rows(SB_W), full(wog), full(wos), full(gq), full(wmq), mem, mem,
                  full(wmo), full(gf)],
        out_specs=[rows(d), rows(d)],
        out_shape=[jax.ShapeDtypeStruct((n, d), F32)] * 2,
        compiler_params=_params(("parallel",)),
        name="mid",
    )(x2d, og, osb, wog, wos, gq, wmq, km, vm, wmo, gf)


def _router_kernel(hf_ref, wrt_ref, br_ref, cmat_ref, idx_ref, gate_ref, rank_ref, cnt_ref, carry_ref, *, tm):
    @pl.when(pl.program_id(0) == 0)
    def _():
        carry_ref[...] = jnp.zeros_like(carry_ref)

    hh, hl = _split(hf_ref[...])
    wh, wl = _split(wrt_ref[...])
    nt = lambda a, b: lax.dot_general(a, b, NT_DIMS, preferred_element_type=F32)
    vals = nt(wh, hh) + (nt(wh, hl) + nt(wl, hh)) + br_ref[...]
    eidx = lax.broadcasted_iota(jnp.int32, (N_EXPERTS, tm), 0)
    tops, sels, hots = [], [], []
    for _ in range(TOP_K):
        m = jnp.max(vals, axis=0, keepdims=True)
        sel = jnp.min(jnp.where(vals == m, eidx, N_EXPERTS), axis=0, keepdims=True)
        hot = eidx == sel
        vals = jnp.where(hot, -jnp.inf, vals)
        tops.append(m)
        sels.append(sel)
        hots.append(hot)
    exps = [jnp.exp(t - tops[0]) for t in tops]
    denom = exps[0] + exps[1] + exps[2] + exps[3]
    chosen = jnp.zeros((N_EXPERTS, tm), F32)
    for hot in hots:
        chosen = chosen + hot.astype(F32)
    sums = _dot(chosen.astype(BF16), cmat_ref[...])
    before = sums[:, :tm] + carry_ref[...]
    for k in range(TOP_K):
        idx_ref[k:k + 1, :] = sels[k]
        gate_ref[k:k + 1, :] = exps[k] / denom
        rank_ref[k:k + 1, :] = jnp.sum(jnp.where(hots[k], before, 0.0), axis=0, keepdims=True).astype(jnp.int32)
    carry_ref[...] = carry_ref[...] + sums[:, tm:]
    cnt_ref[...] = carry_ref[...]


def _router(hf, wrt, br, tm):
    n, d = hf.shape
    r = np.arange(tm)
    cmat = jnp.asarray(np.concatenate([(r[:, None] < r[None, :]).astype(np.float32),
                                       np.ones((tm, tm), np.float32)], axis=1), BF16)
    full = lambda a: pl.BlockSpec(a.shape, lambda i: (0, 0))
    tok = pl.BlockSpec((TOP_K, tm), lambda i: (0, i))
    return pl.pallas_call(
        functools.partial(_router_kernel, tm=tm),
        grid=(n // tm,),
        in_specs=[pl.BlockSpec((tm, d), lambda i: (i, 0)), full(wrt), full(br), full(cmat)],
        out_specs=[tok, tok, tok, pl.BlockSpec((N_EXPERTS, tm), lambda i: (0, 0))],
        out_shape=[jax.ShapeDtypeStruct((TOP_K, n), jnp.int32), jax.ShapeDtypeStruct((TOP_K, n), F32),
                   jax.ShapeDtypeStruct((TOP_K, n), jnp.int32), jax.ShapeDtypeStruct((N_EXPERTS, tm), F32)],
        scratch_shapes=[pltpu.VMEM((N_EXPERTS, tm), F32)],
        compiler_params=_params(("arbitrary",)),
        name="router",
    )(hf, wrt, br, cmat)


def _dispatch_kernel(pos_ref, fill_start_ref, fill_n_ref, nvalid_ref, hf_ref, xs_ref, zero_ref, sem, *, tm, blk):
    i = pl.program_id(0)
    n_blocks = xs_ref.shape[0] // blk

    def block_copy(b):
        return pltpu.make_async_copy(zero_ref, xs_ref.at[pl.ds(pl.multiple_of(b * blk, blk), blk), :], sem)

    def row_copy(src, t, p):
        return pltpu.make_async_copy(src.at[pl.ds(t, 1), :], xs_ref.at[pl.ds(p, 1), :], sem)

    @pl.when(i == 0)
    def _():
        zero_ref[...] = jnp.zeros_like(zero_ref)
        for e in range(N_EXPERTS):
            def fill(r, _):
                row_copy(zero_ref, 0, fill_start_ref[e] + r).start()
                return 0
            lax.fori_loop(0, fill_n_ref[e], fill, 0)
        for e in range(N_EXPERTS):
            def drain(r, _):
                row_copy(zero_ref, 0, fill_start_ref[e] + r).wait()
                return 0
            lax.fori_loop(0, fill_n_ref[e], drain, 0)
        lax.fori_loop(nvalid_ref[0], n_blocks, lambda b, _: (block_copy(b).start(), 0)[1], 0)
        lax.fori_loop(nvalid_ref[0], n_blocks, lambda b, _: (block_copy(b).wait(), 0)[1], 0)

    base = i * tm

    def issue(t, _):
        for k in range(TOP_K):
            row_copy(hf_ref, t, pos_ref[(base + t) * TOP_K + k]).start()
        return 0

    def drain(t, _):
        for k in range(TOP_K):
            row_copy(hf_ref, t, pos_ref[(base + t) * TOP_K + k]).wait()
        return 0

    lax.fori_loop(0, tm, issue, 0)
    lax.fori_loop(0, tm, drain, 0)


def _dispatch(pos_flat, fill_start, fill_n, n_valid, hf, m_pad, tm, blk):
    n, d = hf.shape
    return pl.pallas_call(
        functools.partial(_dispatch_kernel, tm=tm, blk=blk),
        grid_spec=pltpu.PrefetchScalarGridSpec(
            num_scalar_prefetch=4,
            grid=(n // tm,),
            in_specs=[pl.BlockSpec((tm, d), lambda i, *_: (i, 0))],
            out_specs=pl.BlockSpec(memory_space=pl.ANY),
            scratch_shapes=[pltpu.VMEM((blk, d), F32), pltpu.SemaphoreType.DMA],
        ),
        out_shape=jax.ShapeDtypeStruct((m_pad, d), F32),
        compiler_params=_params(("arbitrary",)),
        name="dispatch",
    )(pos_flat, fill_start, fill_n, n_valid, hf)


def _experts_kernel(blk_ref, exp_ref, nvalid_ref, xs_ref, wgu_ref, bgu_ref, wd_ref, bd_ref, ys_ref,
                    wgu_bf, wd_bf):
    i = pl.program_id(0)
    valid = i < nvalid_ref[0]
    changed = jnp.logical_or(i == 0, exp_ref[i] != exp_ref[jnp.maximum(i - 1, 0)])

    @pl.when(jnp.logical_and(valid, changed))
    def _():
        wgu_bf[...] = wgu_ref[...].astype(BF16)
        wd_bf[...] = wd_ref[...].astype(BF16)

    @pl.when(valid)
    def _():
        f = wd_bf.shape[0]
        gu = _dot(xs_ref[...].astype(BF16), wgu_bf[...]) + bgu_ref[...]
        gate = jnp.minimum(gu[:, :f], SWIGLU_LIMIT)
        lin = jnp.clip(gu[:, f:], -SWIGLU_LIMIT, SWIGLU_LIMIT)
        act = (lin + 1.0) * (gate * jax.nn.sigmoid(SWIGLU_ALPHA * gate))
        ys_ref[...] = _dot(act.astype(BF16), wd_bf[...]) + bd_ref[...]

    @pl.when(jnp.logical_not(valid))
    def _():
        ys_ref[...] = jnp.zeros_like(ys_ref)


def _experts(blk_idx, blk_exp, n_valid, xs, w_gu, b_gu, w_down, b_down, blk):
    m_pad, d = xs.shape
    e, _, f2 = w_gu.shape
    f = f2 // 2
    n_steps = m_pad // blk
    return pl.pallas_call(
        _experts_kernel,
        grid_spec=pltpu.PrefetchScalarGridSpec(
            num_scalar_prefetch=3,
            grid=(n_steps,),
            in_specs=[pl.BlockSpec((blk, d), lambda i, b, x, nv: (b[i], 0)),
                      pl.BlockSpec((None, d, f2), lambda i, b, x, nv: (x[i], 0, 0)),
                      pl.BlockSpec((None, 1, f2), lambda i, b, x, nv: (x[i], 0, 0)),
                      pl.BlockSpec((None, f, d), lambda i, b, x, nv: (x[i], 0, 0)),
                      pl.BlockSpec((None, 1, d), lambda i, b, x, nv: (x[i], 0, 0))],
            out_specs=pl.BlockSpec((blk, d), lambda i, b, x, nv: (b[i], 0)),
            scratch_shapes=[pltpu.VMEM((d, f2), BF16), pltpu.VMEM((f, d), BF16)],
        ),
        out_shape=jax.ShapeDtypeStruct((m_pad, d), F32),
        compiler_params=_params(("arbitrary",)),
        name="experts",
    )(blk_idx, blk_exp, n_valid, xs, w_gu, b_gu.reshape(e, 1, f2), w_down, b_down.reshape(e, 1, d))


def _combine_kernel(pos_ref, ys_ref, x2_ref, gate_ref, gfin_ref, y_ref, buf_ref, sems, *, tm):
    i = pl.program_id(0)
    n_steps = pl.num_programs(0)

    def row_copy(step, slot, t, k):
        p = pos_ref[(step * tm + t) * TOP_K + k]
        return pltpu.make_async_copy(ys_ref.at[pl.ds(p, 1), :], buf_ref.at[slot, k, pl.ds(t, 1), :], sems.at[slot])

    def issue(step, slot):
        def body(t, _):
            for k in range(TOP_K):
                row_copy(step, slot, t, k).start()
            return 0
        lax.fori_loop(0, tm, body, 0)

    @pl.when(i == 0)
    def _():
        issue(0, 0)

    @pl.when(i + 1 < n_steps)
    def _():
        issue(i + 1, (i + 1) % 2)

    slot = i % 2

    def drain(t, _):
        for k in range(TOP_K):
            row_copy(i, slot, t, k).wait()
        return 0

    lax.fori_loop(0, tm, drain, 0)
    acc = x2_ref[...]
    for k in range(TOP_K):
        acc = acc + gate_ref[:, k:k + 1] * buf_ref[slot, k]
    y_ref[...] = _rmsnorm(acc, gfin_ref[...])


def _combine(pos_flat, ys, x2, gates, gfin, tm):
    n, d = x2.shape
    return pl.pallas_call(
        functools.partial(_combine_kernel, tm=tm),
        grid_spec=pltpu.PrefetchScalarGridSpec(
            num_scalar_prefetch=1,
            grid=(n // tm,),
            in_specs=[pl.BlockSpec(memory_space=pl.ANY),
                      pl.BlockSpec((tm, d), lambda i, p: (i, 0)),
                      pl.BlockSpec((tm, TOP_K), lambda i, p: (i, 0)),
                      pl.BlockSpec(gfin.shape, lambda i, p: (0, 0))],
            out_specs=pl.BlockSpec((tm, d), lambda i, p: (i, 0)),
            scratch_shapes=[pltpu.VMEM((2, TOP_K, tm, d), F32), pltpu.SemaphoreType.DMA((2,))],
        ),
        out_shape=jax.ShapeDtypeStruct((n, d), F32),
        compiler_params=_params(("arbitrary",)),
        name="combine",
    )(pos_flat, ys, x2, gates, gfin)


def _pick(n, pref):
    t = min(pref, n)
    while n % t:
        t //= 2
    return t


def _layer(x2d, mem, batch, seq, g_mix, w_in, w_gla_gate, b_gla_gate, g_gla_head, w_out, g_mem_q, g_mem_kv,
           w_mq, w_mk, w_mv, w_mo, g_ffn, w_router, b_router, w_gu, b_gu, w_down, b_down, g_out):
    n, d = x2d.shape
    row = lambda v: v.reshape(1, -1).astype(F32)

    o_qg, o_kg, o_vg, o_lr, o_rg, o_qs, o_ks, o_vs = np.cumsum((0, Q_G, Q_G, V_G, GLA_GATE_RANK, V_G, SB_W, SB_W))
    lr_pad = jnp.zeros((d, LANES - GLA_GATE_RANK), w_in.dtype)
    wg = jnp.concatenate([w_in[:, o_qg:o_vg + V_G], w_in[:, o_rg:o_rg + V_G],
                          w_in[:, o_lr:o_lr + GLA_GATE_RANK], lr_pad], axis=1).astype(BF16)
    wq, wk, wv = (w_in[:, o:o + SB_W].astype(BF16) for o in (o_qs, o_ks, o_vs))
    wgate = jnp.concatenate([w_gla_gate, jnp.zeros((LANES - GLA_GATE_RANK, Q_G), F32)], axis=0)

    slab, qs, ks, vs = _in_proj(x2d, row(g_mix), wg, wq, wk, wv, _pick(n, 512))
    o_gla = _gla(slab, wgate, row(b_gla_gate), row(g_gla_head), batch, seq, _pick(seq, 512))
    o_sb = _sb(qs, ks, vs, batch, seq, 2 * LANES)
    km, vm = _mem_kv(mem, row(g_mem_kv), w_mk.astype(BF16), w_mv.astype(BF16))
    w_out_bf = w_out.astype(BF16)
    x2, hf = _mid(x2d, o_gla, o_sb, w_out_bf[:V_G], w_out_bf[V_G:], row(g_mem_q), w_mq.astype(BF16), km, vm,
                  w_mo.astype(BF16), row(g_ffn), seq, _pick(seq, 512))

    tm_r = _pick(n, 256)
    idx, gates, rank, cnt = _router(hf, w_router.T, b_router.reshape(-1, 1), tm_r)

    blk = 256
    counts = cnt[:, 0].astype(jnp.int32)
    padded = ((counts + blk - 1) // blk) * blk
    pad_end = jnp.cumsum(padded)
    pad_start = pad_end - padded
    first = jnp.sum(jnp.where(idx[..., None] == jnp.arange(N_EXPERTS), pad_start, 0), axis=-1)
    pos = (first + rank).T.reshape(-1)
    n_steps = -(-(n * TOP_K + N_EXPERTS * (blk - 1)) // blk)
    m_pad = n_steps * blk
    n_valid = pad_end[-1] // blk
    blk_idx = jnp.arange(n_steps, dtype=jnp.int32)
    used = jnp.minimum(blk_idx, n_valid - 1)
    blk_exp = jnp.minimum(jnp.sum(used[:, None] * blk >= pad_end[None, :], axis=1), N_EXPERTS - 1).astype(jnp.int32)
    n_valid = n_valid.reshape(1)

    xs = _dispatch(pos, pad_start + counts, padded - counts, n_valid, hf, m_pad, _pick(n, 256), blk)
    ys = _experts(blk_idx, blk_exp, n_valid, xs, w_gu, b_gu, w_down, b_down, blk)
    return _combine(pos, ys, x2, gates.T, row(g_out), _pick(n, 128))


def kernel(x, mem, g_mix, w_in, w_gla_gate, b_gla_gate, g_gla_head, w_out, g_mem_q, g_mem_kv, w_mq, w_mk, w_mv,
           w_mo, g_ffn, w_router, b_router, w_gu, b_gu, w_down, b_down, g_final):
    batch, seq, d = x.shape
    depth = g_mix.shape[0]
    assert depth == 1, "the final rmsnorm is fused into the single layer's combine step"
    y = _layer(x.reshape(batch * seq, d), mem, batch, seq, g_mix[0], w_in[0], w_gla_gate[0], b_gla_gate[0],
               g_gla_head[0], w_out[0], g_mem_q[0], g_mem_kv[0], w_mq[0], w_mk[0], w_mv[0], w_mo[0], g_ffn[0],
               w_router[0], b_router[0], w_gu[0], b_gu[0], w_down[0], b_down[0], g_final)
    return y.reshape(batch, seq, d)
```

```python
import functools

import jax
import jax.numpy as jnp
import numpy as np
from jax import lax
from jax.experimental import pallas as pl
from jax.experimental.pallas import tpu as pltpu

F32 = jnp.float32
BF16 = jnp.bfloat16

EPS = 1e-5
CHUNK = 64
GLA_HEADS = 4
GLA_DK = 64
GLA_DV = 128
GLA_GATE_RANK = 16
GLA_TAU = 16.0
SB_HEADS = 8
SB_HD = 64
MEM_HEADS = 4
N_EXPERTS = 32
TOP_K = 4
SWIGLU_LIMIT = 7.0
SWIGLU_ALPHA = 1.702
SB_LOG_UNDERFLOW = -104.0

LANES = 128
Q_G = GLA_HEADS * GLA_DK
V_G = GLA_HEADS * GLA_DV
SB_W = SB_HEADS * SB_HD
GLA_SLAB = Q_G + Q_G + V_G + V_G + LANES

VMEM_LIMIT = 56 * 1024 * 1024
ISSUE_UNROLL = 8

NT_DIMS = (((1,), (1,)), ((), ()))
TN_DIMS = (((0,), (0,)), ((), ()))


def _dot(a, b):
    return jnp.dot(a, b, preferred_element_type=F32)


def _split(x):
    hi = x.astype(BF16)
    lo = (x - hi.astype(F32)).astype(BF16)
    return hi, lo


def _dot_exact_lhs(a_bf16, b_f32):
    hi, lo = _split(b_f32)
    return _dot(a_bf16, hi) + _dot(a_bf16, lo)


def _dot3(a, b):
    ah, al = _split(a)
    bh, bl = _split(b)
    return _dot(ah, bh) + (_dot(ah, bl) + _dot(al, bh))


def _rmsnorm(x, g):
    return x * lax.rsqrt(jnp.mean(x * x, axis=-1, keepdims=True) + EPS) * g


def _softplus(z):
    return jnp.maximum(z, 0.0) + jnp.log(1.0 + jnp.exp(-jnp.abs(z)))


def _params(sem, vmem=VMEM_LIMIT):
    return pltpu.CompilerParams(dimension_semantics=sem, vmem_limit_bytes=vmem)


def _inproj_kernel(x_ref, g_ref, wg_ref, wq_ref, wk_ref, wv_ref, slab_ref, q_ref, k_ref, v_ref):
    h = _rmsnorm(x_ref[...], g_ref[...]).astype(BF16)
    slab_ref[...] = _dot(h, wg_ref[...])
    q_ref[...] = (_dot(h, wq_ref[...]) * (SB_HD ** -0.5)).astype(BF16)
    k_ref[...] = _dot(h, wk_ref[...]).astype(BF16)
    v_ref[...] = _dot(h, wv_ref[...]).astype(BF16)


def _in_proj(x2d, g, wg, wq, wk, wv, tm):
    n, d = x2d.shape
    full = lambda a: pl.BlockSpec(a.shape, lambda i: (0, 0))
    return pl.pallas_call(
        _inproj_kernel,
        grid=(n // tm,),
        in_specs=[pl.BlockSpec((tm, d), lambda i: (i, 0)), full(g), full(wg), full(wq), full(wk), full(wv)],
        out_specs=[pl.BlockSpec((tm, GLA_SLAB), lambda i: (i, 0))] + [pl.BlockSpec((tm, SB_W), lambda i: (i, 0))] * 3,
        out_shape=[jax.ShapeDtypeStruct((n, GLA_SLAB), F32)] + [jax.ShapeDtypeStruct((n, SB_W), BF16)] * 3,
        compiler_params=_params(("parallel",)),
        name="in_proj",
    )(x2d, g, wg, wq, wk, wv)


def _gla_kernel(slab_ref, wgate_ref, bgate_ref, ghead_ref, umat_ref, o_ref, state_ref, *, n_chunks):
    @pl.when(pl.program_id(1) == 0)
    def _():
        state_ref[...] = jnp.zeros_like(state_ref)

    qg = slab_ref[:, 0:Q_G] * (GLA_DK ** -0.5)
    kg = slab_ref[:, Q_G:2 * Q_G]
    glr = slab_ref[:, 2 * Q_G + 2 * V_G:GLA_SLAB]
    log_a = -_softplus(-(_dot3(glr, wgate_ref[...]) + bgate_ref[...])) * (1.0 / GLA_TAU)
    to_end = _dot_exact_lhs(umat_ref[...], log_a)
    kdec = kg * jnp.exp(to_end)
    g_chunk = to_end + log_a

    lane = lax.broadcasted_iota(jnp.int32, (CHUNK, LANES), 1)
    ghead = ghead_ref[...]
    for c in range(n_chunks):
        rows = slice(c * CHUNK, (c + 1) * CHUNK)
        for h in range(GLA_HEADS):
            pair = slice((h // 2) * LANES, (h // 2 + 1) * LANES)
            mine = (lane >= GLA_DK) if h % 2 else (lane < GLA_DK)
            kd = jnp.where(mine, kdec[rows, pair], 0.0).astype(BF16)
            qm = jnp.where(mine, qg[rows, pair], 0.0).astype(BF16)
            vh = slab_ref[rows, 2 * Q_G + h * GLA_DV:2 * Q_G + (h + 1) * GLA_DV].astype(BF16)
            decay = jnp.exp(g_chunk[c * CHUNK:c * CHUNK + 1, pair])
            st = decay * state_ref[h] + lax.dot_general(vh, kd, TN_DIMS, preferred_element_type=F32)
            state_ref[h] = st
            o = lax.dot_general(qm, st.astype(BF16), NT_DIMS, preferred_element_type=F32)
            rg = slab_ref[rows, 2 * Q_G + V_G + h * GLA_DV:2 * Q_G + V_G + (h + 1) * GLA_DV]
            o = _rmsnorm(o, ghead) * (rg * jax.nn.sigmoid(rg))
            o_ref[rows, h * GLA_DV:(h + 1) * GLA_DV] = o.astype(BF16)


def _gla(slab, wgate, bgate, ghead, batch, seq, ts):
    n = slab.shape[0]
    n_chunks = ts // CHUNK
    r = np.arange(ts)
    umat = jnp.asarray((r[None, :] > r[:, None]) & (r[None, :] // CHUNK == r[:, None] // CHUNK), BF16)
    full = lambda a: pl.BlockSpec(a.shape, lambda b, i: (0,) * a.ndim)
    steps = seq // ts
    return pl.pallas_call(
        functools.partial(_gla_kernel, n_chunks=n_chunks),
        grid=(batch, steps),
        in_specs=[pl.BlockSpec((ts, GLA_SLAB), lambda b, i: (b * steps + i, 0)),
                  full(wgate), full(bgate), full(ghead), full(umat)],
        out_specs=pl.BlockSpec((ts, V_G), lambda b, i: (b * steps + i, 0)),
        out_shape=jax.ShapeDtypeStruct((n, V_G), BF16),
        scratch_shapes=[pltpu.VMEM((GLA_HEADS, GLA_DV, LANES), F32)],
        compiler_params=_params(("parallel", "arbitrary")),
        name="gla",
    )(slab, wgate, bgate, ghead, umat)


def _sb_kernel(q_ref, k_ref, v_ref, tmat_ref, o_ref, acc_ref, carry_ref, *, blk):
    i = pl.program_id(2)
    sub = blk // 2
    lane = lax.broadcasted_iota(jnp.int32, (blk, LANES), 1)
    causal = lax.broadcasted_iota(jnp.int32, (blk, blk), 1) < lax.broadcasted_iota(jnp.int32, (blk, blk), 0)
    q = q_ref[...]
    lo_head = lane < SB_HD
    q_heads = (jnp.where(lo_head, q, jnp.zeros_like(q)), jnp.where(lo_head, jnp.zeros_like(q), q))
    tmat = tmat_ref[...]

    def chunk(j, diag):
        start = pl.multiple_of(j * blk, blk)
        kj = k_ref[pl.ds(start, blk), :]
        vj = v_ref[pl.ds(start, blk), :]
        worst = None
        for h in range(2):
            z = lax.dot_general(q_heads[h], kj, NT_DIMS, preferred_element_type=F32)
            sp = _softplus(z)
            log1mb = -sp
            if diag:
                log1mb = jnp.where(causal, log1mb, 0.0)
            hi, lo = _split(log1mb)
            right = _dot(jnp.concatenate([hi[:, sub:], lo[:, sub:]], axis=1), tmat)
            left = _dot(jnp.concatenate([hi[:, :sub], lo[:, :sub]], axis=1), tmat)
            log_sig = z - sp
            if diag:
                after_right = right[:, sub:]
                log_a_right = log_sig[:, sub:] + right[:, :sub]
            else:
                carry = carry_ref[h]
                after_right = carry + right[:, sub:]
                log_a_right = log_sig[:, sub:] + right[:, :sub] + carry
            log_a_left = log_sig[:, :sub] + left[:, :sub] + after_right
            a = jnp.exp(jnp.concatenate([log_a_left, log_a_right], axis=1))
            if diag:
                a = jnp.where(causal, a, 0.0)
            pv = _dot(a.astype(BF16), vj)
            acc_ref[h] = pv if diag else acc_ref[h] + pv
            remaining = after_right + left[:, sub:]
            carry_ref[h] = remaining
            top = jnp.max(remaining)
            worst = top if worst is None else jnp.maximum(worst, top)
        return worst

    def more(state):
        return jnp.logical_and(state[0] >= 0, state[1] > SB_LOG_UNDERFLOW)

    lax.while_loop(more, lambda state: (state[0] - 1, chunk(state[0], False)), (i - 1, chunk(i, True)))
    o_ref[...] = jnp.where(lo_head, acc_ref[0], acc_ref[1]).astype(BF16)


def _sb(q, k, v, batch, seq, blk):
    sub = blk // 2
    assert sub == LANES
    r = np.arange(sub)
    tri = (r[:, None] > r[None, :]).astype(np.float32)
    half = np.concatenate([tri, np.ones((sub, sub), np.float32)], axis=1)
    tmat = jnp.asarray(np.concatenate([half, half], axis=0), BF16)
    nq = seq // blk
    pairs = SB_W // LANES
    q3, k3, v3 = (a.reshape(batch, seq, SB_W) for a in (q, k, v))
    out = pl.pallas_call(
        functools.partial(_sb_kernel, blk=blk),
        grid=(batch, pairs, nq),
        in_specs=[pl.BlockSpec((None, blk, LANES), lambda b, p, i: (b, i, p)),
                  pl.BlockSpec((None, seq, LANES), lambda b, p, i: (b, 0, p)),
                  pl.BlockSpec((None, seq, LANES), lambda b, p, i: (b, 0, p)),
                  pl.BlockSpec(tmat.shape, lambda b, p, i: (0, 0))],
        out_specs=pl.BlockSpec((None, blk, LANES), lambda b, p, i: (b, i, p)),
        out_shape=jax.ShapeDtypeStruct((batch, seq, SB_W), BF16),
        scratch_shapes=[pltpu.VMEM((2, blk, LANES), F32), pltpu.VMEM((2, blk, LANES), F32)],
        compiler_params=_params(("parallel", "parallel", "arbitrary")),
        name="sb",
    )(q3, k3, v3, tmat)
    return out.reshape(batch * seq, SB_W)


def _memkv_kernel(m_ref, g_ref, wk_ref, wv_ref, k_ref, v_ref):
    hm = _rmsnorm(m_ref[...], g_ref[...]).astype(BF16)
    k_ref[...] = _dot(hm, wk_ref[...]).astype(BF16)
    v_ref[...] = _dot(hm, wv_ref[...]).astype(BF16)


def _mem_kv(mem, g, wk, wv):
    b, m, d = mem.shape
    full = lambda a: pl.BlockSpec(a.shape, lambda i: (0, 0))
    blk = pl.BlockSpec((None, m, d), lambda i: (i, 0, 0))
    return pl.pallas_call(
        _memkv_kernel,
        grid=(b,),
        in_specs=[blk, full(g), full(wk), full(wv)],
        out_specs=[blk, blk],
        out_shape=[jax.ShapeDtypeStruct((b, m, d), BF16)] * 2,
        compiler_params=_params(("parallel",)),
        name="mem_kv",
    )(mem, g, wk, wv)


def _mid_kernel(x_ref, og_ref, os_ref, wog_ref, wos_ref, gq_ref, wmq_ref, km_ref, vm_ref, wmo_ref, gf_ref,
                x2_ref, hf_ref):
    x1 = x_ref[...] + _dot(og_ref[...], wog_ref[...]) + _dot(os_ref[...], wos_ref[...])
    hq = _rmsnorm(x1, gq_ref[...]).astype(BF16)
    d = x1.shape[-1]
    hd = d // MEM_HEADS
    q = (_dot(hq, wmq_ref[...]) * (hd ** -0.5)).astype(BF16)
    outs = []
    for h in range(MEM_HEADS):
        cols = slice(h * hd, (h + 1) * hd)
        s = lax.dot_general(q[:, cols], km_ref[:, cols], NT_DIMS, preferred_element_type=F32)
        e = jnp.exp(s - jnp.max(s, axis=-1, keepdims=True))
        p = (e / jnp.sum(e, axis=-1, keepdims=True)).astype(BF16)
        outs.append(_dot(p, vm_ref[:, cols]).astype(BF16))
    x2 = x1 + _dot(jnp.concatenate(outs, axis=1), wmo_ref[...])
    x2_ref[...] = x2
    hf_ref[...] = _rmsnorm(x2, gf_ref[...])


def _mid(x2d, og, osb, wog, wos, gq, wmq, km, vm, wmo, gf, seq, tm):
    n, d = x2d.shape
    m = km.shape[1]
    per_batch = seq // tm
    full = lambda a: pl.BlockSpec(a.shape, lambda i: (0, 0))
    rows = lambda w: pl.BlockSpec((tm, w), lambda i: (i, 0))
    mem = pl.BlockSpec((None, m, d), lambda i: (i // per_batch, 0, 0))
    return pl.pallas_call(
        _mid_kernel,
        grid=(n // tm,),
        in_specs=[rows(d), rows(V_G), rows(SB_W), full(wog), full(wos), full(gq), full(wmq), mem, mem,
                  full(wmo), full(gf)],
        out_specs=[rows(d), rows(d)],
        out_shape=[jax.ShapeDtypeStruct((n, d), F32)] * 2,
        compiler_params=_params(("parallel",)),
        name="mid",
    )(x2d, og, osb, wog, wos, gq, wmq, km, vm, wmo, gf)


def _router_kernel(hf_ref, wrt_ref, br_ref, cmat_ref, idx_ref, gate_ref, rank_ref, cnt_ref, carry_ref, *, tm):
    @pl.when(pl.program_id(0) == 0)
    def _():
        carry_ref[...] = jnp.zeros_like(carry_ref)

    hh, hl = _split(hf_ref[...])
    wh, wl = _split(wrt_ref[...])
    nt = lambda a, b: lax.dot_general(a, b, NT_DIMS, preferred_element_type=F32)
    vals = nt(wh, hh) + (nt(wh, hl) + nt(wl, hh)) + br_ref[...]
    eidx = lax.broadcasted_iota(jnp.int32, (N_EXPERTS, tm), 0)
    tops, sels, hots = [], [], []
    for _ in range(TOP_K):
        m = jnp.max(vals, axis=0, keepdims=True)
        sel = jnp.min(jnp.where(vals == m, eidx, N_EXPERTS), axis=0, keepdims=True)
        hot = eidx == sel
        vals = jnp.where(hot, -jnp.inf, vals)
        tops.append(m)
        sels.append(sel)
        hots.append(hot)
    exps = [jnp.exp(t - tops[0]) for t in tops]
    denom = exps[0] + exps[1] + exps[2] + exps[3]
    chosen = jnp.zeros((N_EXPERTS, tm), F32)
    for hot in hots:
        chosen = chosen + hot.astype(F32)
    sums = _dot(chosen.astype(BF16), cmat_ref[...])
    before = sums[:, :tm] + carry_ref[...]
    for k in range(TOP_K):
        idx_ref[k:k + 1, :] = sels[k]
        gate_ref[k:k + 1, :] = exps[k] / denom
        rank_ref[k:k + 1, :] = jnp.sum(jnp.where(hots[k], before, 0.0), axis=0, keepdims=True).astype(jnp.int32)
    carry_ref[...] = carry_ref[...] + sums[:, tm:]
    cnt_ref[...] = carry_ref[...]


def _router(hf, wrt, br, tm):
    n, d = hf.shape
    r = np.arange(tm)
    cmat = jnp.asarray(np.concatenate([(r[:, None] < r[None, :]).astype(np.float32),
                                       np.ones((tm, tm), np.float32)], axis=1), BF16)
    full = lambda a: pl.BlockSpec(a.shape, lambda i: (0, 0))
    tok = pl.BlockSpec((TOP_K, tm), lambda i: (0, i))
    return pl.pallas_call(
        functools.partial(_router_kernel, tm=tm),
        grid=(n // tm,),
        in_specs=[pl.BlockSpec((tm, d), lambda i: (i, 0)), full(wrt), full(br), full(cmat)],
        out_specs=[tok, tok, tok, pl.BlockSpec((N_EXPERTS, tm), lambda i: (0, 0))],
        out_shape=[jax.ShapeDtypeStruct((TOP_K, n), jnp.int32), jax.ShapeDtypeStruct((TOP_K, n), F32),
                   jax.ShapeDtypeStruct((TOP_K, n), jnp.int32), jax.ShapeDtypeStruct((N_EXPERTS, tm), F32)],
        scratch_shapes=[pltpu.VMEM((N_EXPERTS, tm), F32)],
        compiler_params=_params(("arbitrary",)),
        name="router",
    )(hf, wrt, br, cmat)


def _dispatch_kernel(pos_ref, fill_start_ref, fill_n_ref, nvalid_ref, hf_ref, xs_ref, zero_ref, sem, *, tm, blk):
    i = pl.program_id(0)
    n_blocks = xs_ref.shape[0] // blk

    def block_copy(b):
        return pltpu.make_async_copy(zero_ref, xs_ref.at[pl.ds(pl.multiple_of(b * blk, blk), blk), :], sem)

    def row_copy(src, t, p):
        return pltpu.make_async_copy(src.at[pl.ds(t, 1), :], xs_ref.at[pl.ds(p, 1), :], sem)

    @pl.when(i == 0)
    def _():
        zero_ref[...] = jnp.zeros_like(zero_ref)
        for e in range(N_EXPERTS):
            def fill(r, _):
                row_copy(zero_ref, 0, fill_start_ref[e] + r).start()
                return 0
            lax.fori_loop(0, fill_n_ref[e], fill, 0)
        for e in range(N_EXPERTS):
            def drain(r, _):
                row_copy(zero_ref, 0, fill_start_ref[e] + r).wait()
                return 0
            lax.fori_loop(0, fill_n_ref[e], drain, 0)
        lax.fori_loop(nvalid_ref[0], n_blocks, lambda b, _: (block_copy(b).start(), 0)[1], 0)
        lax.fori_loop(nvalid_ref[0], n_blocks, lambda b, _: (block_copy(b).wait(), 0)[1], 0)

    base = i * tm

    def issue(g, _):
        for u in range(ISSUE_UNROLL):
            t = g * ISSUE_UNROLL + u
            for k in range(TOP_K):
                row_copy(hf_ref, t, pos_ref[(base + t) * TOP_K + k]).start()
        return 0

    lax.fori_loop(0, tm // ISSUE_UNROLL, issue, 0)
    for _ in range(TOP_K):
        pltpu.make_async_copy(hf_ref, xs_ref.at[pl.ds(0, tm), :], sem).wait()


def _dispatch(pos_flat, fill_start, fill_n, n_valid, hf, m_pad, tm, blk):
    n, d = hf.shape
    return pl.pallas_call(
        functools.partial(_dispatch_kernel, tm=tm, blk=blk),
        grid_spec=pltpu.PrefetchScalarGridSpec(
            num_scalar_prefetch=4,
            grid=(n // tm,),
            in_specs=[pl.BlockSpec((tm, d), lambda i, *_: (i, 0))],
            out_specs=pl.BlockSpec(memory_space=pl.ANY),
            scratch_shapes=[pltpu.VMEM((blk, d), F32), pltpu.SemaphoreType.DMA],
        ),
        out_shape=jax.ShapeDtypeStruct((m_pad, d), F32),
        compiler_params=_params(("arbitrary",)),
        name="dispatch",
    )(pos_flat, fill_start, fill_n, n_valid, hf)


def _experts_kernel(exp_ref, first_ref, slot_ref, next_ref, nvalid_ref, xs_ref, wgu_hbm, bgu_ref, wd_hbm, bd_ref,
                    ys_ref, wgu_f32, wd_f32, wgu_bf, wd_bf, wsem):
    i = pl.program_id(0)
    valid = i < nvalid_ref[0]
    slot = slot_ref[i]

    def weight_copies(e, s):
        return (pltpu.make_async_copy(wgu_hbm.at[e], wgu_f32.at[s], wsem.at[s]),
                pltpu.make_async_copy(wd_hbm.at[e], wd_f32.at[s], wsem.at[s]))

    @pl.when(jnp.logical_and(valid, first_ref[i] == 1))
    def _():
        @pl.when(i == 0)
        def _():
            for c in weight_copies(exp_ref[0], 0):
                c.start()

        for c in weight_copies(exp_ref[i], slot):
            c.wait()
        wgu_bf[...] = wgu_f32[slot].astype(BF16)
        wd_bf[...] = wd_f32[slot].astype(BF16)

        @pl.when(next_ref[i] >= 0)
        def _():
            for c in weight_copies(next_ref[i], 1 - slot):
                c.start()

    @pl.when(valid)
    def _():
        f = wd_bf.shape[0]
        gu = _dot(xs_ref[...].astype(BF16), wgu_bf[...]) + bgu_ref[...]
        gate = jnp.minimum(gu[:, :f], SWIGLU_LIMIT)
        lin = jnp.clip(gu[:, f:], -SWIGLU_LIMIT, SWIGLU_LIMIT)
        act = (lin + 1.0) * (gate * jax.nn.sigmoid(SWIGLU_ALPHA * gate))
        ys_ref[...] = _dot(act.astype(BF16), wd_bf[...]) + bd_ref[...]

    @pl.when(jnp.logical_not(valid))
    def _():
        ys_ref[...] = jnp.zeros_like(ys_ref)


def _experts(blk_exp, run_first, run_slot, run_next, n_valid, xs, w_gu, b_gu, w_down, b_down, blk):
    m_pad, d = xs.shape
    e, _, f2 = w_gu.shape
    f = f2 // 2
    n_steps = m_pad // blk
    rows = pl.BlockSpec((blk, d), lambda i, *_: (i, 0))
    return pl.pallas_call(
        _experts_kernel,
        grid_spec=pltpu.PrefetchScalarGridSpec(
            num_scalar_prefetch=5,
            grid=(n_steps,),
            in_specs=[rows,
                      pl.BlockSpec(memory_space=pl.ANY),
                      pl.BlockSpec((None, 1, f2), lambda i, x, *_: (x[i], 0, 0)),
                      pl.BlockSpec(memory_space=pl.ANY),
                      pl.BlockSpec((None, 1, d), lambda i, x, *_: (x[i], 0, 0))],
            out_specs=rows,
            scratch_shapes=[pltpu.VMEM((2, d, f2), F32), pltpu.VMEM((2, f, d), F32),
                            pltpu.VMEM((d, f2), BF16), pltpu.VMEM((f, d), BF16),
                            pltpu.SemaphoreType.DMA((2,))],
        ),
        out_shape=jax.ShapeDtypeStruct((m_pad, d), F32),
        compiler_params=_params(("arbitrary",)),
        name="experts",
    )(blk_exp, run_first, run_slot, run_next, n_valid, xs, w_gu, b_gu.reshape(e, 1, f2), w_down,
      b_down.reshape(e, 1, d))


def _combine_kernel(pos_ref, ys_ref, x2_ref, gate_ref, gfin_ref, y_ref, buf_ref, sems, *, tm):
    i = pl.program_id(0)
    n_steps = pl.num_programs(0)

    def row_copy(step, slot, t, k):
        p = pos_ref[(step * tm + t) * TOP_K + k]
        return pltpu.make_async_copy(ys_ref.at[pl.ds(p, 1), :], buf_ref.at[slot, k, pl.ds(t, 1), :], sems.at[slot])

    def issue(step, slot):
        def body(g, _):
            for u in range(ISSUE_UNROLL):
                for k in range(TOP_K):
                    row_copy(step, slot, g * ISSUE_UNROLL + u, k).start()
            return 0
        lax.fori_loop(0, tm // ISSUE_UNROLL, body, 0)

    @pl.when(i == 0)
    def _():
        issue(0, 0)

    @pl.when(i + 1 < n_steps)
    def _():
        issue(i + 1, (i + 1) % 2)

    slot = i % 2
    for k in range(TOP_K):
        pltpu.make_async_copy(ys_ref.at[pl.ds(0, tm), :], buf_ref.at[slot, k], sems.at[slot]).wait()
    acc = x2_ref[...]
    for k in range(TOP_K):
        acc = acc + gate_ref[:, k:k + 1] * buf_ref[slot, k]
    y_ref[...] = _rmsnorm(acc, gfin_ref[...])


def _combine(pos_flat, ys, x2, gates, gfin, tm):
    n, d = x2.shape
    return pl.pallas_call(
        functools.partial(_combine_kernel, tm=tm),
        grid_spec=pltpu.PrefetchScalarGridSpec(
            num_scalar_prefetch=1,
            grid=(n // tm,),
            in_specs=[pl.BlockSpec(memory_space=pl.ANY),
                      pl.BlockSpec((tm, d), lambda i, p: (i, 0)),
                      pl.BlockSpec((tm, TOP_K), lambda i, p: (i, 0)),
                      pl.BlockSpec(gfin.shape, lambda i, p: (0, 0))],
            out_specs=pl.BlockSpec((tm, d), lambda i, p: (i, 0)),
            scratch_shapes=[pltpu.VMEM((2, TOP_K, tm, d), F32), pltpu.SemaphoreType.DMA((2,))],
        ),
        out_shape=jax.ShapeDtypeStruct((n, d), F32),
        compiler_params=_params(("arbitrary",)),
        name="combine",
    )(pos_flat, ys, x2, gates, gfin)


def _pick(n, pref):
    t = min(pref, n)
    while n % t:
        t //= 2
    return t


def _layer(x2d, mem, batch, seq, g_mix, w_in, w_gla_gate, b_gla_gate, g_gla_head, w_out, g_mem_q, g_mem_kv,
           w_mq, w_mk, w_mv, w_mo, g_ffn, w_router, b_router, w_gu, b_gu, w_down, b_down, g_out):
    n, d = x2d.shape
    row = lambda v: v.reshape(1, -1).astype(F32)

    o_qg, o_kg, o_vg, o_lr, o_rg, o_qs, o_ks, o_vs = np.cumsum((0, Q_G, Q_G, V_G, GLA_GATE_RANK, V_G, SB_W, SB_W))
    lr_pad = jnp.zeros((d, LANES - GLA_GATE_RANK), w_in.dtype)
    wg = jnp.concatenate([w_in[:, o_qg:o_vg + V_G], w_in[:, o_rg:o_rg + V_G],
                          w_in[:, o_lr:o_lr + GLA_GATE_RANK], lr_pad], axis=1).astype(BF16)
    wq, wk, wv = (w_in[:, o:o + SB_W].astype(BF16) for o in (o_qs, o_ks, o_vs))
    wgate = jnp.concatenate([w_gla_gate, jnp.zeros((LANES - GLA_GATE_RANK, Q_G), F32)], axis=0)

    slab, qs, ks, vs = _in_proj(x2d, row(g_mix), wg, wq, wk, wv, _pick(n, 512))
    o_gla = _gla(slab, wgate, row(b_gla_gate), row(g_gla_head), batch, seq, _pick(seq, 512))
    o_sb = _sb(qs, ks, vs, batch, seq, 2 * LANES)
    km, vm = _mem_kv(mem, row(g_mem_kv), w_mk.astype(BF16), w_mv.astype(BF16))
    w_out_bf = w_out.astype(BF16)
    x2, hf = _mid(x2d, o_gla, o_sb, w_out_bf[:V_G], w_out_bf[V_G:], row(g_mem_q), w_mq.astype(BF16), km, vm,
                  w_mo.astype(BF16), row(g_ffn), seq, _pick(seq, 512))

    tm_r = _pick(n, 256)
    idx, gates, rank, cnt = _router(hf, w_router.T, b_router.reshape(-1, 1), tm_r)

    blk = 256
    counts = cnt[:, 0].astype(jnp.int32)
    padded = ((counts + blk - 1) // blk) * blk
    pad_end = jnp.cumsum(padded)
    pad_start = pad_end - padded
    first = jnp.sum(jnp.where(idx[..., None] == jnp.arange(N_EXPERTS), pad_start, 0), axis=-1)
    pos = (first + rank).T.reshape(-1)
    n_steps = -(-(n * TOP_K + N_EXPERTS * (blk - 1)) // blk)
    m_pad = n_steps * blk
    n_valid = pad_end[-1] // blk
    blk_idx = jnp.arange(n_steps, dtype=jnp.int32)
    used = jnp.minimum(blk_idx, n_valid - 1)
    blk_exp = jnp.minimum(jnp.sum(used[:, None] * blk >= pad_end[None, :], axis=1), N_EXPERTS - 1).astype(jnp.int32)
    run_first = jnp.logical_and(blk_idx < n_valid, blk_exp != jnp.concatenate([blk_exp[:1] - 1, blk_exp[:-1]]))
    run_first = run_first.astype(jnp.int32)
    run_slot = (jnp.cumsum(run_first) - 1) % 2
    e_ids = jnp.arange(N_EXPERTS, dtype=jnp.int32)
    later = jnp.where(jnp.logical_and(padded[None, :] > 0, e_ids[None, :] > e_ids[:, None]), e_ids[None, :], N_EXPERTS)
    following = jnp.min(later, axis=1)
    run_next = jnp.where(following == N_EXPERTS, -1, following)[blk_exp]
    n_valid = n_valid.reshape(1)

    xs = _dispatch(pos, pad_start + counts, padded - counts, n_valid, hf, m_pad, _pick(n, 256), blk)
    ys = _experts(blk_exp, run_first, run_slot.astype(jnp.int32), run_next.astype(jnp.int32), n_valid, xs,
                  w_gu, b_gu, w_down, b_down, blk)
    return _combine(pos, ys, x2, gates.T, row(g_out), _pick(n, 128))


def kernel(x, mem, g_mix, w_in, w_gla_gate, b_gla_gate, g_gla_head, w_out, g_mem_q, g_mem_kv, w_mq, w_mk, w_mv,
           w_mo, g_ffn, w_router, b_router, w_gu, b_gu, w_down, b_down, g_final):
    batch, seq, d = x.shape
    depth = g_mix.shape[0]
    assert depth == 1, "the final rmsnorm is fused into the single layer's combine step"
    y = _layer(x.reshape(batch * seq, d), mem, batch, seq, g_mix[0], w_in[0], w_gla_gate[0], b_gla_gate[0],
               g_gla_head[0], w_out[0], g_mem_q[0], g_mem_kv[0], w_mq[0], w_mk[0], w_mv[0], w_mo[0], g_ffn[0],
               w_router[0], b_router[0], w_gu[0], b_gu[0], w_down[0], b_down[0], g_final)
    return y.reshape(batch, seq, d)
```

```python
import functools

import jax
import jax.numpy as jnp
import numpy as np
from jax import lax
from jax.experimental import pallas as pl
from jax.experimental.pallas import tpu as pltpu

F32 = jnp.float32
BF16 = jnp.bfloat16

EPS = 1e-5
CHUNK = 64
GLA_HEADS = 4
GLA_DK = 64
GLA_DV = 128
GLA_GATE_RANK = 16
GLA_TAU = 16.0
SB_HEADS = 8
SB_HD = 64
MEM_HEADS = 4
N_EXPERTS = 32
TOP_K = 4
SWIGLU_LIMIT = 7.0
SWIGLU_ALPHA = 1.702
SB_LOG_UNDERFLOW = -104.0

LANES = 128
Q_G = GLA_HEADS * GLA_DK
V_G = GLA_HEADS * GLA_DV
SB_W = SB_HEADS * SB_HD
GLA_SLAB = Q_G + Q_G + V_G + V_G + LANES

VMEM_LIMIT = 56 * 1024 * 1024
SUBLANES = 8

NT_DIMS = (((1,), (1,)), ((), ()))
TN_DIMS = (((0,), (0,)), ((), ()))


def _dot(a, b):
    return jnp.dot(a, b, preferred_element_type=F32)


def _split(x):
    hi = x.astype(BF16)
    lo = (x - hi.astype(F32)).astype(BF16)
    return hi, lo


def _dot_exact_lhs(a_bf16, b_f32):
    hi, lo = _split(b_f32)
    return _dot(a_bf16, hi) + _dot(a_bf16, lo)


def _dot3(a, b):
    ah, al = _split(a)
    bh, bl = _split(b)
    return _dot(ah, bh) + (_dot(ah, bl) + _dot(al, bh))


def _rmsnorm(x, g):
    return x * lax.rsqrt(jnp.mean(x * x, axis=-1, keepdims=True) + EPS) * g


def _softplus(z):
    return jnp.maximum(z, 0.0) + jnp.log(1.0 + jnp.exp(-jnp.abs(z)))


def _params(sem, vmem=VMEM_LIMIT):
    return pltpu.CompilerParams(dimension_semantics=sem, vmem_limit_bytes=vmem)


def _inproj_kernel(x_ref, g_ref, wg_ref, wq_ref, wk_ref, wv_ref, slab_ref, q_ref, k_ref, v_ref):
    h = _rmsnorm(x_ref[...], g_ref[...]).astype(BF16)
    slab_ref[...] = _dot(h, wg_ref[...])
    q_ref[...] = (_dot(h, wq_ref[...]) * (SB_HD ** -0.5)).astype(BF16)
    k_ref[...] = _dot(h, wk_ref[...]).astype(BF16)
    v_ref[...] = _dot(h, wv_ref[...]).astype(BF16)


def _in_proj(x2d, g, wg, wq, wk, wv, tm):
    n, d = x2d.shape
    full = lambda a: pl.BlockSpec(a.shape, lambda i: (0, 0))
    return pl.pallas_call(
        _inproj_kernel,
        grid=(n // tm,),
        in_specs=[pl.BlockSpec((tm, d), lambda i: (i, 0)), full(g), full(wg), full(wq), full(wk), full(wv)],
        out_specs=[pl.BlockSpec((tm, GLA_SLAB), lambda i: (i, 0))] + [pl.BlockSpec((tm, SB_W), lambda i: (i, 0))] * 3,
        out_shape=[jax.ShapeDtypeStruct((n, GLA_SLAB), F32)] + [jax.ShapeDtypeStruct((n, SB_W), BF16)] * 3,
        compiler_params=_params(("parallel",)),
        name="in_proj",
    )(x2d, g, wg, wq, wk, wv)


def _gla_kernel(slab_ref, wgate_ref, bgate_ref, ghead_ref, umat_ref, o_ref, state_ref, *, n_chunks):
    @pl.when(pl.program_id(1) == 0)
    def _():
        state_ref[...] = jnp.zeros_like(state_ref)

    qg = slab_ref[:, 0:Q_G] * (GLA_DK ** -0.5)
    kg = slab_ref[:, Q_G:2 * Q_G]
    glr = slab_ref[:, 2 * Q_G + 2 * V_G:GLA_SLAB]
    log_a = -_softplus(-(_dot3(glr, wgate_ref[...]) + bgate_ref[...])) * (1.0 / GLA_TAU)
    to_end = _dot_exact_lhs(umat_ref[...], log_a)
    kdec = kg * jnp.exp(to_end)
    g_chunk = to_end + log_a

    lane = lax.broadcasted_iota(jnp.int32, (CHUNK, LANES), 1)
    ghead = ghead_ref[...]
    for c in range(n_chunks):
        rows = slice(c * CHUNK, (c + 1) * CHUNK)
        for h in range(GLA_HEADS):
            pair = slice((h // 2) * LANES, (h // 2 + 1) * LANES)
            mine = (lane >= GLA_DK) if h % 2 else (lane < GLA_DK)
            kd = jnp.where(mine, kdec[rows, pair], 0.0).astype(BF16)
            qm = jnp.where(mine, qg[rows, pair], 0.0).astype(BF16)
            vh = slab_ref[rows, 2 * Q_G + h * GLA_DV:2 * Q_G + (h + 1) * GLA_DV].astype(BF16)
            decay = jnp.exp(g_chunk[c * CHUNK:c * CHUNK + 1, pair])
            st = decay * state_ref[h] + lax.dot_general(vh, kd, TN_DIMS, preferred_element_type=F32)
            state_ref[h] = st
            o = lax.dot_general(qm, st.astype(BF16), NT_DIMS, preferred_element_type=F32)
            rg = slab_ref[rows, 2 * Q_G + V_G + h * GLA_DV:2 * Q_G + V_G + (h + 1) * GLA_DV]
            o = _rmsnorm(o, ghead) * (rg * jax.nn.sigmoid(rg))
            o_ref[rows, h * GLA_DV:(h + 1) * GLA_DV] = o.astype(BF16)


def _gla(slab, wgate, bgate, ghead, batch, seq, ts):
    n = slab.shape[0]
    n_chunks = ts // CHUNK
    r = np.arange(ts)
    umat = jnp.asarray((r[None, :] > r[:, None]) & (r[None, :] // CHUNK == r[:, None] // CHUNK), BF16)
    full = lambda a: pl.BlockSpec(a.shape, lambda b, i: (0,) * a.ndim)
    steps = seq // ts
    return pl.pallas_call(
        functools.partial(_gla_kernel, n_chunks=n_chunks),
        grid=(batch, steps),
        in_specs=[pl.BlockSpec((ts, GLA_SLAB), lambda b, i: (b * steps + i, 0)),
                  full(wgate), full(bgate), full(ghead), full(umat)],
        out_specs=pl.BlockSpec((ts, V_G), lambda b, i: (b * steps + i, 0)),
        out_shape=jax.ShapeDtypeStruct((n, V_G), BF16),
        scratch_shapes=[pltpu.VMEM((GLA_HEADS, GLA_DV, LANES), F32)],
        compiler_params=_params(("parallel", "arbitrary")),
        name="gla",
    )(slab, wgate, bgate, ghead, umat)


def _sb_kernel(q_ref, k_ref, v_ref, tmat_ref, o_ref, acc_ref, carry_ref, *, blk):
    i = pl.program_id(2)
    sub = blk // 2
    lane = lax.broadcasted_iota(jnp.int32, (blk, LANES), 1)
    causal = lax.broadcasted_iota(jnp.int32, (blk, blk), 1) < lax.broadcasted_iota(jnp.int32, (blk, blk), 0)
    q = q_ref[...]
    lo_head = lane < SB_HD
    q_heads = (jnp.where(lo_head, q, jnp.zeros_like(q)), jnp.where(lo_head, jnp.zeros_like(q), q))
    tmat = tmat_ref[...]

    def chunk(j, diag):
        start = pl.multiple_of(j * blk, blk)
        kj = k_ref[pl.ds(start, blk), :]
        vj = v_ref[pl.ds(start, blk), :]
        worst = None
        for h in range(2):
            z = lax.dot_general(q_heads[h], kj, NT_DIMS, preferred_element_type=F32)
            sp = _softplus(z)
            log1mb = -sp
            if diag:
                log1mb = jnp.where(causal, log1mb, 0.0)
            hi, lo = _split(log1mb)
            right = _dot(jnp.concatenate([hi[:, sub:], lo[:, sub:]], axis=1), tmat)
            left = _dot(jnp.concatenate([hi[:, :sub], lo[:, :sub]], axis=1), tmat)
            log_sig = z - sp
            if diag:
                after_right = right[:, sub:]
                log_a_right = log_sig[:, sub:] + right[:, :sub]
            else:
                carry = carry_ref[h]
                after_right = carry + right[:, sub:]
                log_a_right = log_sig[:, sub:] + right[:, :sub] + carry
            log_a_left = log_sig[:, :sub] + left[:, :sub] + after_right
            a = jnp.exp(jnp.concatenate([log_a_left, log_a_right], axis=1))
            if diag:
                a = jnp.where(causal, a, 0.0)
            pv = _dot(a.astype(BF16), vj)
            acc_ref[h] = pv if diag else acc_ref[h] + pv
            remaining = after_right + left[:, sub:]
            carry_ref[h] = remaining
            top = jnp.max(remaining)
            worst = top if worst is None else jnp.maximum(worst, top)
        return worst

    def more(state):
        return jnp.logical_and(state[0] >= 0, state[1] > SB_LOG_UNDERFLOW)

    def first_two():
        chunk(i, True)
        return chunk(i - 1, False)

    start = lax.cond(i >= 1, first_two, lambda: chunk(i, True))
    lax.while_loop(more, lambda state: (state[0] - 1, chunk(state[0], False)), (i - 2, start))
    o_ref[...] = jnp.where(lo_head, acc_ref[0], acc_ref[1]).astype(BF16)


def _sb(q, k, v, batch, seq, blk):
    sub = blk // 2
    assert sub == LANES
    r = np.arange(sub)
    tri = (r[:, None] > r[None, :]).astype(np.float32)
    half = np.concatenate([tri, np.ones((sub, sub), np.float32)], axis=1)
    tmat = jnp.asarray(np.concatenate([half, half], axis=0), BF16)
    nq = seq // blk
    pairs = SB_W // LANES
    q3, k3, v3 = (a.reshape(batch, seq, SB_W) for a in (q, k, v))
    out = pl.pallas_call(
        functools.partial(_sb_kernel, blk=blk),
        grid=(batch, pairs, nq),
        in_specs=[pl.BlockSpec((None, blk, LANES), lambda b, p, i: (b, i, p)),
                  pl.BlockSpec((None, seq, LANES), lambda b, p, i: (b, 0, p)),
                  pl.BlockSpec((None, seq, LANES), lambda b, p, i: (b, 0, p)),
                  pl.BlockSpec(tmat.shape, lambda b, p, i: (0, 0))],
        out_specs=pl.BlockSpec((None, blk, LANES), lambda b, p, i: (b, i, p)),
        out_shape=jax.ShapeDtypeStruct((batch, seq, SB_W), BF16),
        scratch_shapes=[pltpu.VMEM((2, blk, LANES), F32), pltpu.VMEM((2, blk, LANES), F32)],
        compiler_params=_params(("parallel", "parallel", "arbitrary")),
        name="sb",
    )(q3, k3, v3, tmat)
    return out.reshape(batch * seq, SB_W)


def _memkv_kernel(m_ref, g_ref, wk_ref, wv_ref, k_ref, v_ref):
    hm = _rmsnorm(m_ref[...], g_ref[...]).astype(BF16)
    k_ref[...] = _dot(hm, wk_ref[...]).astype(BF16)
    v_ref[...] = _dot(hm, wv_ref[...]).astype(BF16)


def _mem_kv(mem, g, wk, wv):
    b, m, d = mem.shape
    full = lambda a: pl.BlockSpec(a.shape, lambda i: (0, 0))
    blk = pl.BlockSpec((None, m, d), lambda i: (i, 0, 0))
    return pl.pallas_call(
        _memkv_kernel,
        grid=(b,),
        in_specs=[blk, full(g), full(wk), full(wv)],
        out_specs=[blk, blk],
        out_shape=[jax.ShapeDtypeStruct((b, m, d), BF16)] * 2,
        compiler_params=_params(("parallel",)),
        name="mem_kv",
    )(mem, g, wk, wv)


def _mid_kernel(x_ref, og_ref, os_ref, wog_ref, wos_ref, gq_ref, wmq_ref, km_ref, vm_ref, wmo_ref, gf_ref,
                x2_ref, hf_ref):
    x1 = x_ref[...] + _dot(og_ref[...], wog_ref[...]) + _dot(os_ref[...], wos_ref[...])
    hq = _rmsnorm(x1, gq_ref[...]).astype(BF16)
    d = x1.shape[-1]
    hd = d // MEM_HEADS
    q = (_dot(hq, wmq_ref[...]) * (hd ** -0.5)).astype(BF16)
    outs = []
    for h in range(MEM_HEADS):
        cols = slice(h * hd, (h + 1) * hd)
        s = lax.dot_general(q[:, cols], km_ref[:, cols], NT_DIMS, preferred_element_type=F32)
        e = jnp.exp(s - jnp.max(s, axis=-1, keepdims=True))
        p = (e / jnp.sum(e, axis=-1, keepdims=True)).astype(BF16)
        outs.append(_dot(p, vm_ref[:, cols]).astype(BF16))
    x2 = x1 + _dot(jnp.concatenate(outs, axis=1), wmo_ref[...])
    x2_ref[...] = x2
    hf_ref[...] = _rmsnorm(x2, gf_ref[...])


def _mid(x2d, og, osb, wog, wos, gq, wmq, km, vm, wmo, gf, seq, tm):
    n, d = x2d.shape
    m = km.shape[1]
    per_batch = seq // tm
    full = lambda a: pl.BlockSpec(a.shape, lambda i: (0, 0))
    rows = lambda w: pl.BlockSpec((tm, w), lambda i: (i, 0))
    mem = pl.BlockSpec((None, m, d), lambda i: (i // per_batch, 0, 0))
    return pl.pallas_call(
        _mid_kernel,
        grid=(n // tm,),
        in_specs=[rows(d), rows(V_G), rows(SB_W), full(wog), full(wos), full(gq), full(wmq), mem, mem,
                  full(wmo), full(gf)],
        out_specs=[rows(d), rows(d)],
        out_shape=[jax.ShapeDtypeStruct((n, d), F32)] * 2,
        compiler_params=_params(("parallel",)),
        name="mid",
    )(x2d, og, osb, wog, wos, gq, wmq, km, vm, wmo, gf)


def _router_kernel(hf_ref, wrt_ref, br_ref, cmat_ref, idx_ref, gate_ref, rank_ref, cnt_ref, carry_ref, *, tm):
    @pl.when(pl.program_id(0) == 0)
    def _():
        carry_ref[...] = jnp.zeros_like(carry_ref)

    hh, hl = _split(hf_ref[...])
    wh, wl = _split(wrt_ref[...])
    nt = lambda a, b: lax.dot_general(a, b, NT_DIMS, preferred_element_type=F32)
    vals = nt(wh, hh) + (nt(wh, hl) + nt(wl, hh)) + br_ref[...]
    eidx = lax.broadcasted_iota(jnp.int32, (N_EXPERTS, tm), 0)
    tops, sels, hots = [], [], []
    for _ in range(TOP_K):
        m = jnp.max(vals, axis=0, keepdims=True)
        sel = jnp.min(jnp.where(vals == m, eidx, N_EXPERTS), axis=0, keepdims=True)
        hot = eidx == sel
        vals = jnp.where(hot, -jnp.inf, vals)
        tops.append(m)
        sels.append(sel)
        hots.append(hot)
    exps = [jnp.exp(t - tops[0]) for t in tops]
    denom = exps[0] + exps[1] + exps[2] + exps[3]
    chosen = jnp.zeros((N_EXPERTS, tm), F32)
    for hot in hots:
        chosen = chosen + hot.astype(F32)
    sums = _dot(chosen.astype(BF16), cmat_ref[...])
    before = sums[:, :tm] + carry_ref[...]
    for k in range(TOP_K):
        idx_ref[k:k + 1, :] = sels[k]
        gate_ref[k:k + 1, :] = exps[k] / denom
        rank_ref[k:k + 1, :] = jnp.sum(jnp.where(hots[k], before, 0.0), axis=0, keepdims=True).astype(jnp.int32)
    carry_ref[...] = carry_ref[...] + sums[:, tm:]
    cnt_ref[...] = carry_ref[...]


def _router(hf, wrt, br, tm):
    n, d = hf.shape
    r = np.arange(tm)
    cmat = jnp.asarray(np.concatenate([(r[:, None] < r[None, :]).astype(np.float32),
                                       np.ones((tm, tm), np.float32)], axis=1), BF16)
    full = lambda a: pl.BlockSpec(a.shape, lambda i: (0, 0))
    tok = pl.BlockSpec((TOP_K, tm), lambda i: (0, i))
    return pl.pallas_call(
        functools.partial(_router_kernel, tm=tm),
        grid=(n // tm,),
        in_specs=[pl.BlockSpec((tm, d), lambda i: (i, 0)), full(wrt), full(br), full(cmat)],
        out_specs=[tok, tok, tok, pl.BlockSpec((N_EXPERTS, tm), lambda i: (0, 0))],
        out_shape=[jax.ShapeDtypeStruct((TOP_K, n), jnp.int32), jax.ShapeDtypeStruct((TOP_K, n), F32),
                   jax.ShapeDtypeStruct((TOP_K, n), jnp.int32), jax.ShapeDtypeStruct((N_EXPERTS, tm), F32)],
        scratch_shapes=[pltpu.VMEM((N_EXPERTS, tm), F32)],
        compiler_params=_params(("arbitrary",)),
        name="router",
    )(hf, wrt, br, cmat)


def _dispatch_kernel(pos_ref, fill_start_ref, fill_n_ref, nvalid_ref, hf_ref, xs_ref, zero_ref, sem, *, tm, blk):
    i = pl.program_id(0)
    n_blocks = xs_ref.shape[0] // blk

    def block_copy(b):
        return pltpu.make_async_copy(zero_ref, xs_ref.at[pl.ds(pl.multiple_of(b * blk, blk), blk), :], sem)

    def row_copy(src_row, p):
        return pltpu.make_async_copy(src_row, xs_ref.at[pl.ds(p, 1), :], sem)

    zero_row = zero_ref.at[pl.ds(0, 1), :]

    @pl.when(i == 0)
    def _():
        zero_ref[...] = jnp.zeros_like(zero_ref)
        for e in range(N_EXPERTS):
            def fill(r, _):
                row_copy(zero_row, fill_start_ref[e] + r).start()
                return 0
            lax.fori_loop(0, fill_n_ref[e], fill, 0)
        for e in range(N_EXPERTS):
            def drain(r, _):
                row_copy(zero_row, fill_start_ref[e] + r).wait()
                return 0
            lax.fori_loop(0, fill_n_ref[e], drain, 0)
        lax.fori_loop(nvalid_ref[0], n_blocks, lambda b, _: (block_copy(b).start(), 0)[1], 0)
        lax.fori_loop(nvalid_ref[0], n_blocks, lambda b, _: (block_copy(b).wait(), 0)[1], 0)

    base = i * tm

    def issue(g, _):
        for u in range(SUBLANES):
            t = g * SUBLANES + u
            for k in range(TOP_K):
                row_copy(hf_ref.at[g, pl.ds(u, 1), :], pos_ref[(base + t) * TOP_K + k]).start()
        return 0

    lax.fori_loop(0, tm // SUBLANES, issue, 0)
    for _ in range(tm * TOP_K // blk):
        block_copy(0).wait()


def _dispatch(pos_flat, fill_start, fill_n, n_valid, hf, m_pad, tm, blk):
    n, d = hf.shape
    assert (tm * TOP_K) % blk == 0
    return pl.pallas_call(
        functools.partial(_dispatch_kernel, tm=tm, blk=blk),
        grid_spec=pltpu.PrefetchScalarGridSpec(
            num_scalar_prefetch=4,
            grid=(n // tm,),
            in_specs=[pl.BlockSpec((tm // SUBLANES, SUBLANES, d), lambda i, *_: (i, 0, 0))],
            out_specs=pl.BlockSpec(memory_space=pl.ANY),
            scratch_shapes=[pltpu.VMEM((blk, d), F32), pltpu.SemaphoreType.DMA],
        ),
        out_shape=jax.ShapeDtypeStruct((m_pad, d), F32),
        compiler_params=_params(("arbitrary",)),
        name="dispatch",
    )(pos_flat, fill_start, fill_n, n_valid, hf.reshape(n // SUBLANES, SUBLANES, d))


def _experts_kernel(exp_ref, first_ref, slot_ref, next_ref, nvalid_ref, xs_ref, wgu_hbm, bgu_ref, wd_hbm, bd_ref,
                    ys_ref, wgu_f32, wd_f32, wgu_bf, wd_bf, wsem):
    i = pl.program_id(0)
    valid = i < nvalid_ref[0]
    slot = slot_ref[i]

    def weight_copies(e, s):
        return (pltpu.make_async_copy(wgu_hbm.at[e], wgu_f32.at[s], wsem.at[s]),
                pltpu.make_async_copy(wd_hbm.at[e], wd_f32.at[s], wsem.at[s]))

    @pl.when(jnp.logical_and(valid, first_ref[i] == 1))
    def _():
        @pl.when(i == 0)
        def _():
            for c in weight_copies(exp_ref[0], 0):
                c.start()

        for c in weight_copies(exp_ref[i], slot):
            c.wait()
        wgu_bf[...] = wgu_f32[slot].astype(BF16)
        wd_bf[...] = wd_f32[slot].astype(BF16)

        @pl.when(next_ref[i] >= 0)
        def _():
            for c in weight_copies(next_ref[i], 1 - slot):
                c.start()

    @pl.when(valid)
    def _():
        f = wd_bf.shape[0]
        gu = _dot(xs_ref[...].astype(BF16), wgu_bf[...]) + bgu_ref[...]
        gate = jnp.minimum(gu[:, :f], SWIGLU_LIMIT)
        lin = jnp.clip(gu[:, f:], -SWIGLU_LIMIT, SWIGLU_LIMIT)
        act = (lin + 1.0) * (gate * jax.nn.sigmoid(SWIGLU_ALPHA * gate))
        ys_ref[...] = _dot(act.astype(BF16), wd_bf[...]) + bd_ref[...]

    @pl.when(jnp.logical_not(valid))
    def _():
        ys_ref[...] = jnp.zeros_like(ys_ref)


def _experts(blk_exp, run_first, run_slot, run_next, n_valid, xs, w_gu, b_gu, w_down, b_down, blk):
    m_pad, d = xs.shape
    e, _, f2 = w_gu.shape
    f = f2 // 2
    n_steps = m_pad // blk
    rows = pl.BlockSpec((blk, d), lambda i, *_: (i, 0))
    return pl.pallas_call(
        _experts_kernel,
        grid_spec=pltpu.PrefetchScalarGridSpec(
            num_scalar_prefetch=5,
            grid=(n_steps,),
            in_specs=[rows,
                      pl.BlockSpec(memory_space=pl.ANY),
                      pl.BlockSpec((None, 1, f2), lambda i, x, *_: (x[i], 0, 0)),
                      pl.BlockSpec(memory_space=pl.ANY),
                      pl.BlockSpec((None, 1, d), lambda i, x, *_: (x[i], 0, 0))],
            out_specs=rows,
            scratch_shapes=[pltpu.VMEM((2, d, f2), F32), pltpu.VMEM((2, f, d), F32),
                            pltpu.VMEM((d, f2), BF16), pltpu.VMEM((f, d), BF16),
                            pltpu.SemaphoreType.DMA((2,))],
        ),
        out_shape=jax.ShapeDtypeStruct((m_pad, d), F32),
        compiler_params=_params(("arbitrary",)),
        name="experts",
    )(blk_exp, run_first, run_slot, run_next, n_valid, xs, w_gu, b_gu.reshape(e, 1, f2), w_down,
      b_down.reshape(e, 1, d))


def _combine_kernel(pos_ref, ys_ref, x2_ref, gate_ref, gfin_ref, y_ref, buf_ref, sems, *, tm):
    i = pl.program_id(0)
    n_steps = pl.num_programs(0)

    def row_copy(step, slot, t, k):
        p = pos_ref[(step * tm + t) * TOP_K + k]
        return pltpu.make_async_copy(ys_ref.at[pl.ds(p, 1), :], buf_ref.at[slot, k, pl.ds(t, 1), :], sems.at[slot])

    def issue(step, slot):
        def body(g, _):
            for u in range(SUBLANES):
                for k in range(TOP_K):
                    row_copy(step, slot, g * SUBLANES + u, k).start()
            return 0
        lax.fori_loop(0, tm // SUBLANES, body, 0)

    @pl.when(i == 0)
    def _():
        issue(0, 0)

    @pl.when(i + 1 < n_steps)
    def _():
        issue(i + 1, (i + 1) % 2)

    slot = i % 2
    for k in range(TOP_K):
        pltpu.make_async_copy(ys_ref.at[pl.ds(0, tm), :], buf_ref.at[slot, k], sems.at[slot]).wait()
    acc = x2_ref[...]
    for k in range(TOP_K):
        acc = acc + gate_ref[:, k:k + 1] * buf_ref[slot, k]
    y_ref[...] = _rmsnorm(acc, gfin_ref[...])


def _combine(pos_flat, ys, x2, gates, gfin, tm):
    n, d = x2.shape
    return pl.pallas_call(
        functools.partial(_combine_kernel, tm=tm),
        grid_spec=pltpu.PrefetchScalarGridSpec(
            num_scalar_prefetch=1,
            grid=(n // tm,),
            in_specs=[pl.BlockSpec(memory_space=pl.ANY),
                      pl.BlockSpec((tm, d), lambda i, p: (i, 0)),
                      pl.BlockSpec((tm, TOP_K), lambda i, p: (i, 0)),
                      pl.BlockSpec(gfin.shape, lambda i, p: (0, 0))],
            out_specs=pl.BlockSpec((tm, d), lambda i, p: (i, 0)),
            scratch_shapes=[pltpu.VMEM((2, TOP_K, tm, d), F32), pltpu.SemaphoreType.DMA((2,))],
        ),
        out_shape=jax.ShapeDtypeStruct((n, d), F32),
        compiler_params=_params(("arbitrary",)),
        name="combine",
    )(pos_flat, ys, x2, gates, gfin)


def _pick(n, pref):
    t = min(pref, n)
    while n % t:
        t //= 2
    return t


def _layer(x2d, mem, batch, seq, g_mix, w_in, w_gla_gate, b_gla_gate, g_gla_head, w_out, g_mem_q, g_mem_kv,
           w_mq, w_mk, w_mv, w_mo, g_ffn, w_router, b_router, w_gu, b_gu, w_down, b_down, g_out):
    n, d = x2d.shape
    row = lambda v: v.reshape(1, -1).astype(F32)

    o_qg, o_kg, o_vg, o_lr, o_rg, o_qs, o_ks, o_vs = np.cumsum((0, Q_G, Q_G, V_G, GLA_GATE_RANK, V_G, SB_W, SB_W))
    lr_pad = jnp.zeros((d, LANES - GLA_GATE_RANK), w_in.dtype)
    wg = jnp.concatenate([w_in[:, o_qg:o_vg + V_G], w_in[:, o_rg:o_rg + V_G],
                          w_in[:, o_lr:o_lr + GLA_GATE_RANK], lr_pad], axis=1).astype(BF16)
    wq, wk, wv = (w_in[:, o:o + SB_W].astype(BF16) for o in (o_qs, o_ks, o_vs))
    wgate = jnp.concatenate([w_gla_gate, jnp.zeros((LANES - GLA_GATE_RANK, Q_G), F32)], axis=0)

    slab, qs, ks, vs = _in_proj(x2d, row(g_mix), wg, wq, wk, wv, _pick(n, 512))
    o_gla = _gla(slab, wgate, row(b_gla_gate), row(g_gla_head), batch, seq, _pick(seq, 512))
    o_sb = _sb(qs, ks, vs, batch, seq, 2 * LANES)
    km, vm = _mem_kv(mem, row(g_mem_kv), w_mk.astype(BF16), w_mv.astype(BF16))
    w_out_bf = w_out.astype(BF16)
    x2, hf = _mid(x2d, o_gla, o_sb, w_out_bf[:V_G], w_out_bf[V_G:], row(g_mem_q), w_mq.astype(BF16), km, vm,
                  w_mo.astype(BF16), row(g_ffn), seq, _pick(seq, 512))

    tm_r = _pick(n, 256)
    idx, gates, rank, cnt = _router(hf, w_router.T, b_router.reshape(-1, 1), tm_r)

    blk = 256
    counts = cnt[:, 0].astype(jnp.int32)
    padded = ((counts + blk - 1) // blk) * blk
    pad_end = jnp.cumsum(padded)
    pad_start = pad_end - padded
    first = jnp.sum(jnp.where(idx[..., None] == jnp.arange(N_EXPERTS), pad_start, 0), axis=-1)
    pos = (first + rank).T.reshape(-1)
    n_steps = -(-(n * TOP_K + N_EXPERTS * (blk - 1)) // blk)
    m_pad = n_steps * blk
    n_valid = pad_end[-1] // blk
    blk_idx = jnp.arange(n_steps, dtype=jnp.int32)
    used = jnp.minimum(blk_idx, n_valid - 1)
    blk_exp = jnp.minimum(jnp.sum(used[:, None] * blk >= pad_end[None, :], axis=1), N_EXPERTS - 1).astype(jnp.int32)
    run_first = jnp.logical_and(blk_idx < n_valid, blk_exp != jnp.concatenate([blk_exp[:1] - 1, blk_exp[:-1]]))
    run_first = run_first.astype(jnp.int32)
    run_slot = (jnp.cumsum(run_first) - 1) % 2
    e_ids = jnp.arange(N_EXPERTS, dtype=jnp.int32)
    later = jnp.where(jnp.logical_and(padded[None, :] > 0, e_ids[None, :] > e_ids[:, None]), e_ids[None, :], N_EXPERTS)
    following = jnp.min(later, axis=1)
    run_next = jnp.where(following == N_EXPERTS, -1, following)[blk_exp]
    n_valid = n_valid.reshape(1)

    xs = _dispatch(pos, pad_start + counts, padded - counts, n_valid, hf, m_pad, _pick(n, 512), blk)
    ys = _experts(blk_exp, run_first, run_slot.astype(jnp.int32), run_next.astype(jnp.int32), n_valid, xs,
                  w_gu, b_gu, w_down, b_down, blk)
    return _combine(pos, ys, x2, gates.T, row(g_out), _pick(n, 256))


def kernel(x, mem, g_mix, w_in, w_gla_gate, b_gla_gate, g_gla_head, w_out, g_mem_q, g_mem_kv, w_mq, w_mk, w_mv,
           w_mo, g_ffn, w_router, b_router, w_gu, b_gu, w_down, b_down, g_final):
    batch, seq, d = x.shape
    depth = g_mix.shape[0]
    assert depth == 1, "the final rmsnorm is fused into the single layer's combine step"
    y = _layer(x.reshape(batch * seq, d), mem, batch, seq, g_mix[0], w_in[0], w_gla_gate[0], b_gla_gate[0],
               g_gla_head[0], w_out[0], g_mem_q[0], g_mem_kv[0], w_mq[0], w_mk[0], w_mv[0], w_mo[0], g_ffn[0],
               w_router[0], b_router[0], w_gu[0], b_gu[0], w_down[0], b_down[0], g_final)
    return y.reshape(batch, seq, d)
```

```python
import functools

import jax
import jax.numpy as jnp
import numpy as np
from jax import lax
from jax.experimental import pallas as pl
from jax.experimental.pallas import tpu as pltpu

F32 = jnp.float32
BF16 = jnp.bfloat16

EPS = 1e-5
CHUNK = 64
GLA_HEADS = 4
GLA_DK = 64
GLA_DV = 128
GLA_GATE_RANK = 16
GLA_TAU = 16.0
SB_HEADS = 8
SB_HD = 64
MEM_HEADS = 4
N_EXPERTS = 32
TOP_K = 4
SWIGLU_LIMIT = 7.0
SWIGLU_ALPHA = 1.702
SB_LOG_UNDERFLOW = -104.0

LANES = 128
Q_G = GLA_HEADS * GLA_DK
V_G = GLA_HEADS * GLA_DV
SB_W = SB_HEADS * SB_HD
GLA_SLAB = Q_G + Q_G + V_G + V_G + LANES

VMEM_LIMIT = 56 * 1024 * 1024
SUBLANES = 8

NT_DIMS = (((1,), (1,)), ((), ()))
TN_DIMS = (((0,), (0,)), ((), ()))


def _dot(a, b):
    return jnp.dot(a, b, preferred_element_type=F32)


def _split(x):
    hi = x.astype(BF16)
    lo = (x - hi.astype(F32)).astype(BF16)
    return hi, lo


def _dot_exact_lhs(a_bf16, b_f32):
    hi, lo = _split(b_f32)
    return _dot(a_bf16, hi) + _dot(a_bf16, lo)


def _dot3(a, b):
    ah, al = _split(a)
    bh, bl = _split(b)
    return _dot(ah, bh) + (_dot(ah, bl) + _dot(al, bh))


def _rmsnorm(x, g):
    return x * lax.rsqrt(jnp.mean(x * x, axis=-1, keepdims=True) + EPS) * g


def _softplus(z):
    return jnp.maximum(z, 0.0) + jnp.log(1.0 + jnp.exp(-jnp.abs(z)))


def _params(sem, vmem=VMEM_LIMIT):
    return pltpu.CompilerParams(dimension_semantics=sem, vmem_limit_bytes=vmem)


def _inproj_kernel(x_ref, g_ref, wg_ref, wq_ref, wk_ref, wv_ref, slab_ref, q_ref, k_ref, v_ref):
    h = _rmsnorm(x_ref[...], g_ref[...]).astype(BF16)
    slab_ref[...] = _dot(h, wg_ref[...])
    q_ref[...] = (_dot(h, wq_ref[...]) * (SB_HD ** -0.5)).astype(BF16)
    k_ref[...] = _dot(h, wk_ref[...]).astype(BF16)
    v_ref[...] = _dot(h, wv_ref[...]).astype(BF16)


def _in_proj(x2d, g, wg, wq, wk, wv, tm):
    n, d = x2d.shape
    full = lambda a: pl.BlockSpec(a.shape, lambda i: (0, 0))
    return pl.pallas_call(
        _inproj_kernel,
        grid=(n // tm,),
        in_specs=[pl.BlockSpec((tm, d), lambda i: (i, 0)), full(g), full(wg), full(wq), full(wk), full(wv)],
        out_specs=[pl.BlockSpec((tm, GLA_SLAB), lambda i: (i, 0))] + [pl.BlockSpec((tm, SB_W), lambda i: (i, 0))] * 3,
        out_shape=[jax.ShapeDtypeStruct((n, GLA_SLAB), F32)] + [jax.ShapeDtypeStruct((n, SB_W), BF16)] * 3,
        compiler_params=_params(("parallel",)),
        name="in_proj",
    )(x2d, g, wg, wq, wk, wv)


def _gla_kernel(slab_ref, wgate_ref, bgate_ref, ghead_ref, umat_ref, o_ref, state_ref, *, n_chunks):
    @pl.when(pl.program_id(1) == 0)
    def _():
        state_ref[...] = jnp.zeros_like(state_ref)

    qg = slab_ref[:, 0:Q_G] * (GLA_DK ** -0.5)
    kg = slab_ref[:, Q_G:2 * Q_G]
    glr = slab_ref[:, 2 * Q_G + 2 * V_G:GLA_SLAB]
    log_a = -_softplus(-(_dot3(glr, wgate_ref[...]) + bgate_ref[...])) * (1.0 / GLA_TAU)
    to_end = _dot_exact_lhs(umat_ref[...], log_a)
    kdec = kg * jnp.exp(to_end)
    g_chunk = to_end + log_a

    lane = lax.broadcasted_iota(jnp.int32, (CHUNK, LANES), 1)
    ghead = ghead_ref[...]
    for c in range(n_chunks):
        rows = slice(c * CHUNK, (c + 1) * CHUNK)
        for h in range(GLA_HEADS):
            pair = slice((h // 2) * LANES, (h // 2 + 1) * LANES)
            mine = (lane >= GLA_DK) if h % 2 else (lane < GLA_DK)
            kd = jnp.where(mine, kdec[rows, pair], 0.0).astype(BF16)
            qm = jnp.where(mine, qg[rows, pair], 0.0).astype(BF16)
            vh = slab_ref[rows, 2 * Q_G + h * GLA_DV:2 * Q_G + (h + 1) * GLA_DV].astype(BF16)
            decay = jnp.exp(g_chunk[c * CHUNK:c * CHUNK + 1, pair])
            st = decay * state_ref[h] + lax.dot_general(vh, kd, TN_DIMS, preferred_element_type=F32)
            state_ref[h] = st
            o = lax.dot_general(qm, st.astype(BF16), NT_DIMS, preferred_element_type=F32)
            rg = slab_ref[rows, 2 * Q_G + V_G + h * GLA_DV:2 * Q_G + V_G + (h + 1) * GLA_DV]
            o = _rmsnorm(o, ghead) * (rg * jax.nn.sigmoid(rg))
            o_ref[rows, h * GLA_DV:(h + 1) * GLA_DV] = o.astype(BF16)


def _gla(slab, wgate, bgate, ghead, batch, seq, ts):
    n = slab.shape[0]
    n_chunks = ts // CHUNK
    r = np.arange(ts)
    umat = jnp.asarray((r[None, :] > r[:, None]) & (r[None, :] // CHUNK == r[:, None] // CHUNK), BF16)
    full = lambda a: pl.BlockSpec(a.shape, lambda b, i: (0,) * a.ndim)
    steps = seq // ts
    return pl.pallas_call(
        functools.partial(_gla_kernel, n_chunks=n_chunks),
        grid=(batch, steps),
        in_specs=[pl.BlockSpec((ts, GLA_SLAB), lambda b, i: (b * steps + i, 0)),
                  full(wgate), full(bgate), full(ghead), full(umat)],
        out_specs=pl.BlockSpec((ts, V_G), lambda b, i: (b * steps + i, 0)),
        out_shape=jax.ShapeDtypeStruct((n, V_G), BF16),
        scratch_shapes=[pltpu.VMEM((GLA_HEADS, GLA_DV, LANES), F32)],
        compiler_params=_params(("parallel", "arbitrary")),
        name="gla",
    )(slab, wgate, bgate, ghead, umat)


def _sb_kernel(q_ref, k_ref, v_ref, tmat_ref, o_ref, acc_ref, carry_ref, *, blk):
    i = pl.program_id(2)
    sub = blk // 2
    lane = lax.broadcasted_iota(jnp.int32, (blk, LANES), 1)
    causal = lax.broadcasted_iota(jnp.int32, (blk, blk), 1) < lax.broadcasted_iota(jnp.int32, (blk, blk), 0)
    q = q_ref[...]
    lo_head = lane < SB_HD
    q_heads = (jnp.where(lo_head, q, jnp.zeros_like(q)), jnp.where(lo_head, jnp.zeros_like(q), q))
    tmat = tmat_ref[...]

    def chunk(j, diag):
        start = pl.multiple_of(j * blk, blk)
        kj = k_ref[pl.ds(start, blk), :]
        vj = v_ref[pl.ds(start, blk), :]
        worst = None
        for h in range(2):
            z = lax.dot_general(q_heads[h], kj, NT_DIMS, preferred_element_type=F32)
            sp = _softplus(z)
            log1mb = -sp
            if diag:
                log1mb = jnp.where(causal, log1mb, 0.0)
            hi, lo = _split(log1mb)
            right = _dot(jnp.concatenate([hi[:, sub:], lo[:, sub:]], axis=1), tmat)
            left = _dot(jnp.concatenate([hi[:, :sub], lo[:, :sub]], axis=1), tmat)
            log_sig = z - sp
            if diag:
                after_right = right[:, sub:]
                log_a_right = log_sig[:, sub:] + right[:, :sub]
            else:
                carry = carry_ref[h]
                after_right = carry + right[:, sub:]
                log_a_right = log_sig[:, sub:] + right[:, :sub] + carry
            log_a_left = log_sig[:, :sub] + left[:, :sub] + after_right
            a = jnp.exp(jnp.concatenate([log_a_left, log_a_right], axis=1))
            if diag:
                a = jnp.where(causal, a, 0.0)
            pv = _dot(a.astype(BF16), vj)
            acc_ref[h] = pv if diag else acc_ref[h] + pv
            remaining = after_right + left[:, sub:]
            carry_ref[h] = remaining
            top = jnp.max(remaining)
            worst = top if worst is None else jnp.maximum(worst, top)
        return worst

    def more(state):
        return jnp.logical_and(state[0] >= 0, state[1] > SB_LOG_UNDERFLOW)

    def first_two():
        chunk(i, True)
        return chunk(i - 1, False)

    start = lax.cond(i >= 1, first_two, lambda: chunk(i, True))
    lax.while_loop(more, lambda state: (state[0] - 1, chunk(state[0], False)), (i - 2, start))
    o_ref[...] = jnp.where(lo_head, acc_ref[0], acc_ref[1]).astype(BF16)


def _sb(q, k, v, batch, seq, blk):
    sub = blk // 2
    assert sub == LANES
    r = np.arange(sub)
    tri = (r[:, None] > r[None, :]).astype(np.float32)
    half = np.concatenate([tri, np.ones((sub, sub), np.float32)], axis=1)
    tmat = jnp.asarray(np.concatenate([half, half], axis=0), BF16)
    nq = seq // blk
    pairs = SB_W // LANES
    q3, k3, v3 = (a.reshape(batch, seq, SB_W) for a in (q, k, v))
    out = pl.pallas_call(
        functools.partial(_sb_kernel, blk=blk),
        grid=(batch, pairs, nq),
        in_specs=[pl.BlockSpec((None, blk, LANES), lambda b, p, i: (b, i, p)),
                  pl.BlockSpec((None, seq, LANES), lambda b, p, i: (b, 0, p)),
                  pl.BlockSpec((None, seq, LANES), lambda b, p, i: (b, 0, p)),
                  pl.BlockSpec(tmat.shape, lambda b, p, i: (0, 0))],
        out_specs=pl.BlockSpec((None, blk, LANES), lambda b, p, i: (b, i, p)),
        out_shape=jax.ShapeDtypeStruct((batch, seq, SB_W), BF16),
        scratch_shapes=[pltpu.VMEM((2, blk, LANES), F32), pltpu.VMEM((2, blk, LANES), F32)],
        compiler_params=_params(("parallel", "parallel", "arbitrary")),
        name="sb",
    )(q3, k3, v3, tmat)
    return out.reshape(batch * seq, SB_W)


def _memkv_kernel(m_ref, g_ref, wk_ref, wv_ref, k_ref, v_ref):
    hm = _rmsnorm(m_ref[...], g_ref[...]).astype(BF16)
    k_ref[...] = _dot(hm, wk_ref[...]).astype(BF16)
    v_ref[...] = _dot(hm, wv_ref[...]).astype(BF16)


def _mem_kv(mem, g, wk, wv):
    b, m, d = mem.shape
    full = lambda a: pl.BlockSpec(a.shape, lambda i: (0, 0))
    blk = pl.BlockSpec((None, m, d), lambda i: (i, 0, 0))
    return pl.pallas_call(
        _memkv_kernel,
        grid=(b,),
        in_specs=[blk, full(g), full(wk), full(wv)],
        out_specs=[blk, blk],
        out_shape=[jax.ShapeDtypeStruct((b, m, d), BF16)] * 2,
        compiler_params=_params(("parallel",)),
        name="mem_kv",
    )(mem, g, wk, wv)


def _mid_kernel(x_ref, og_ref, os_ref, wog_ref, wos_ref, gq_ref, wmq_ref, km_ref, vm_ref, wmo_ref, gf_ref,
                x2_ref, hf_ref):
    x1 = x_ref[...] + _dot(og_ref[...], wog_ref[...]) + _dot(os_ref[...], wos_ref[...])
    hq = _rmsnorm(x1, gq_ref[...]).astype(BF16)
    d = x1.shape[-1]
    hd = d // MEM_HEADS
    q = (_dot(hq, wmq_ref[...]) * (hd ** -0.5)).astype(BF16)
    outs = []
    for h in range(MEM_HEADS):
        cols = slice(h * hd, (h + 1) * hd)
        s = lax.dot_general(q[:, cols], km_ref[:, cols], NT_DIMS, preferred_element_type=F32)
        e = jnp.exp(s - jnp.max(s, axis=-1, keepdims=True))
        p = (e / jnp.sum(e, axis=-1, keepdims=True)).astype(BF16)
        outs.append(_dot(p, vm_ref[:, cols]).astype(BF16))
    x2 = x1 + _dot(jnp.concatenate(outs, axis=1), wmo_ref[...])
    x2_ref[...] = x2
    hf_ref[...] = _rmsnorm(x2, gf_ref[...])


def _mid(x2d, og, osb, wog, wos, gq, wmq, km, vm, wmo, gf, seq, tm):
    n, d = x2d.shape
    m = km.shape[1]
    per_batch = seq // tm
    full = lambda a: pl.BlockSpec(a.shape, lambda i: (0, 0))
    rows = lambda w: pl.BlockSpec((tm, w), lambda i: (i, 0))
    mem = pl.BlockSpec((None, m, d), lambda i: (i // per_batch, 0, 0))
    return pl.pallas_call(
        _mid_kernel,
        grid=(n // tm,),
        in_specs=[rows(d), rows(V_G), rows(SB_W), full(wog), full(wos), full(gq), full(wmq), mem, mem,
                  full(wmo), full(gf)],
        out_specs=[rows(d), rows(d)],
        out_shape=[jax.ShapeDtypeStruct((n, d), F32)] * 2,
        compiler_params=_params(("parallel",)),
        name="mid",
    )(x2d, og, osb, wog, wos, gq, wmq, km, vm, wmo, gf)


def _router_kernel(hf_ref, wrt_ref, br_ref, cmat_ref, idx_ref, gate_ref, rank_ref, cnt_ref, carry_ref, *, tm):
    @pl.when(pl.program_id(0) == 0)
    def _():
        carry_ref[...] = jnp.zeros_like(carry_ref)

    hh, hl = _split(hf_ref[...])
    wh, wl = _split(wrt_ref[...])
    nt = lambda a, b: lax.dot_general(a, b, NT_DIMS, preferred_element_type=F32)
    vals = nt(wh, hh) + (nt(wh, hl) + nt(wl, hh)) + br_ref[...]
    eidx = lax.broadcasted_iota(jnp.int32, (N_EXPERTS, tm), 0)
    tops, sels, hots = [], [], []
    for _ in range(TOP_K):
        m = jnp.max(vals, axis=0, keepdims=True)
        sel = jnp.min(jnp.where(vals == m, eidx, N_EXPERTS), axis=0, keepdims=True)
        hot = eidx == sel
        vals = jnp.where(hot, -jnp.inf, vals)
        tops.append(m)
        sels.append(sel)
        hots.append(hot)
    exps = [jnp.exp(t - tops[0]) for t in tops]
    denom = exps[0] + exps[1] + exps[2] + exps[3]
    chosen = jnp.zeros((N_EXPERTS, tm), F32)
    for hot in hots:
        chosen = chosen + hot.astype(F32)
    sums = _dot(chosen.astype(BF16), cmat_ref[...])
    before = sums[:, :tm] + carry_ref[...]
    for k in range(TOP_K):
        idx_ref[k:k + 1, :] = sels[k]
        gate_ref[k:k + 1, :] = exps[k] / denom
        rank_ref[k:k + 1, :] = jnp.sum(jnp.where(hots[k], before, 0.0), axis=0, keepdims=True).astype(jnp.int32)
    carry_ref[...] = carry_ref[...] + sums[:, tm:]
    cnt_ref[...] = carry_ref[...]


def _router(hf, wrt, br, tm):
    n, d = hf.shape
    r = np.arange(tm)
    cmat = jnp.asarray(np.concatenate([(r[:, None] < r[None, :]).astype(np.float32),
                                       np.ones((tm, tm), np.float32)], axis=1), BF16)
    full = lambda a: pl.BlockSpec(a.shape, lambda i: (0, 0))
    tok = pl.BlockSpec((TOP_K, tm), lambda i: (0, i))
    return pl.pallas_call(
        functools.partial(_router_kernel, tm=tm),
        grid=(n // tm,),
        in_specs=[pl.BlockSpec((tm, d), lambda i: (i, 0)), full(wrt), full(br), full(cmat)],
        out_specs=[tok, tok, tok, pl.BlockSpec((N_EXPERTS, tm), lambda i: (0, 0))],
        out_shape=[jax.ShapeDtypeStruct((TOP_K, n), jnp.int32), jax.ShapeDtypeStruct((TOP_K, n), F32),
                   jax.ShapeDtypeStruct((TOP_K, n), jnp.int32), jax.ShapeDtypeStruct((N_EXPERTS, tm), F32)],
        scratch_shapes=[pltpu.VMEM((N_EXPERTS, tm), F32)],
        compiler_params=_params(("arbitrary",)),
        name="router",
    )(hf, wrt, br, cmat)


def _dispatch_kernel(pos_ref, fill_start_ref, fill_n_ref, nvalid_ref, hf_ref, xs_ref, zero_ref, stage_ref, sem,
                     row_sems, load_sems, *, tm, blk):
    i = pl.program_id(0)
    n_blocks = xs_ref.shape[0] // blk

    def block_copy(b, on=None):
        return pltpu.make_async_copy(zero_ref, xs_ref.at[pl.ds(pl.multiple_of(b * blk, blk), blk), :],
                                     sem if on is None else on)

    def row_copy(src_row, p, on=None):
        return pltpu.make_async_copy(src_row, xs_ref.at[pl.ds(p, 1), :], sem if on is None else on)

    zero_row = zero_ref.at[pl.ds(0, 1), :]

    @pl.when(i == 0)
    def _():
        zero_ref[...] = jnp.zeros_like(zero_ref)
        for e in range(N_EXPERTS):
            def fill(r, _):
                row_copy(zero_row, fill_start_ref[e] + r).start()
                return 0
            lax.fori_loop(0, fill_n_ref[e], fill, 0)
        for e in range(N_EXPERTS):
            def drain(r, _):
                row_copy(zero_row, fill_start_ref[e] + r).wait()
                return 0
            lax.fori_loop(0, fill_n_ref[e], drain, 0)
        lax.fori_loop(nvalid_ref[0], n_blocks, lambda b, _: (block_copy(b).start(), 0)[1], 0)
        lax.fori_loop(nvalid_ref[0], n_blocks, lambda b, _: (block_copy(b).wait(), 0)[1], 0)

    base = i * tm

    n_tiles = pl.num_programs(0)
    rows_per_tile = tm // SUBLANES

    def tile_load(j):
        s = j % 3
        return pltpu.make_async_copy(hf_ref.at[pl.ds(j * rows_per_tile, rows_per_tile)], stage_ref.at[s],
                                     load_sems.at[s])

    @pl.when(i == 0)
    def _():
        tile_load(0).start()

    @pl.when(i + 1 < n_tiles)
    def _():
        tile_load(i + 1).start()

    tile_load(i).wait()
    mine = row_sems.at[i % 2]
    tile = stage_ref.at[i % 3]

    def issue(g, _):
        for u in range(SUBLANES):
            t = base + g * SUBLANES + u
            for k in range(TOP_K):
                row_copy(tile.at[g, pl.ds(u, 1), :], pos_ref[t * TOP_K + k], mine).start()
        return 0

    lax.fori_loop(0, rows_per_tile, issue, 0)

    def drain(on):
        for _ in range(tm * TOP_K // blk):
            block_copy(0, on).wait()

    @pl.when(i >= 1)
    def _():
        drain(row_sems.at[(i + 1) % 2])

    @pl.when(i == n_tiles - 1)
    def _():
        drain(mine)


def _dispatch(pos_flat, fill_start, fill_n, n_valid, hf, m_pad, tm, blk):
    n, d = hf.shape
    assert (tm * TOP_K) % blk == 0
    return pl.pallas_call(
        functools.partial(_dispatch_kernel, tm=tm, blk=blk),
        grid_spec=pltpu.PrefetchScalarGridSpec(
            num_scalar_prefetch=4,
            grid=(n // tm,),
            in_specs=[pl.BlockSpec(memory_space=pl.ANY)],
            out_specs=pl.BlockSpec(memory_space=pl.ANY),
            scratch_shapes=[pltpu.VMEM((blk, d), F32), pltpu.VMEM((3, tm // SUBLANES, SUBLANES, d), F32),
                            pltpu.SemaphoreType.DMA, pltpu.SemaphoreType.DMA((2,)), pltpu.SemaphoreType.DMA((3,))],
        ),
        out_shape=jax.ShapeDtypeStruct((m_pad, d), F32),
        compiler_params=_params(("arbitrary",)),
        name="dispatch",
    )(pos_flat, fill_start, fill_n, n_valid, hf.reshape(n // SUBLANES, SUBLANES, d))


def _experts_kernel(exp_ref, first_ref, slot_ref, next_ref, nvalid_ref, xs_ref, wgu_hbm, bgu_ref, wd_hbm, bd_ref,
                    ys_ref, wgu_f32, wd_f32, wgu_bf, wd_bf, wsem):
    i = pl.program_id(0)
    valid = i < nvalid_ref[0]
    slot = slot_ref[i]

    def weight_copies(e, s):
        return (pltpu.make_async_copy(wgu_hbm.at[e], wgu_f32.at[s], wsem.at[s]),
                pltpu.make_async_copy(wd_hbm.at[e], wd_f32.at[s], wsem.at[s]))

    @pl.when(jnp.logical_and(valid, first_ref[i] == 1))
    def _():
        @pl.when(i == 0)
        def _():
            for c in weight_copies(exp_ref[0], 0):
                c.start()

        for c in weight_copies(exp_ref[i], slot):
            c.wait()
        wgu_bf[...] = wgu_f32[slot].astype(BF16)
        wd_bf[...] = wd_f32[slot].astype(BF16)

        @pl.when(next_ref[i] >= 0)
        def _():
            for c in weight_copies(next_ref[i], 1 - slot):
                c.start()

    @pl.when(valid)
    def _():
        f = wd_bf.shape[0]
        gu = _dot(xs_ref[...].astype(BF16), wgu_bf[...]) + bgu_ref[...]
        gate = jnp.minimum(gu[:, :f], SWIGLU_LIMIT)
        lin = jnp.clip(gu[:, f:], -SWIGLU_LIMIT, SWIGLU_LIMIT)
        act = (lin + 1.0) * (gate * jax.nn.sigmoid(SWIGLU_ALPHA * gate))
        ys_ref[...] = _dot(act.astype(BF16), wd_bf[...]) + bd_ref[...]

    @pl.when(jnp.logical_not(valid))
    def _():
        ys_ref[...] = jnp.zeros_like(ys_ref)


def _experts(blk_exp, run_first, run_slot, run_next, n_valid, xs, w_gu, b_gu, w_down, b_down, blk):
    m_pad, d = xs.shape
    e, _, f2 = w_gu.shape
    f = f2 // 2
    n_steps = m_pad // blk
    rows = pl.BlockSpec((blk, d), lambda i, *_: (i, 0))
    return pl.pallas_call(
        _experts_kernel,
        grid_spec=pltpu.PrefetchScalarGridSpec(
            num_scalar_prefetch=5,
            grid=(n_steps,),
            in_specs=[rows,
                      pl.BlockSpec(memory_space=pl.ANY),
                      pl.BlockSpec((None, 1, f2), lambda i, x, *_: (x[i], 0, 0)),
                      pl.BlockSpec(memory_space=pl.ANY),
                      pl.BlockSpec((None, 1, d), lambda i, x, *_: (x[i], 0, 0))],
            out_specs=rows,
            scratch_shapes=[pltpu.VMEM((2, d, f2), F32), pltpu.VMEM((2, f, d), F32),
                            pltpu.VMEM((d, f2), BF16), pltpu.VMEM((f, d), BF16),
                            pltpu.SemaphoreType.DMA((2,))],
        ),
        out_shape=jax.ShapeDtypeStruct((m_pad, d), F32),
        compiler_params=_params(("arbitrary",)),
        name="experts",
    )(blk_exp, run_first, run_slot, run_next, n_valid, xs, w_gu, b_gu.reshape(e, 1, f2), w_down,
      b_down.reshape(e, 1, d))


def _combine_kernel(pos_ref, ys_ref, x2_ref, gate_ref, gfin_ref, y_ref, buf_ref, sems, *, tm):
    i = pl.program_id(0)
    n_steps = pl.num_programs(0)

    def row_copy(step, slot, t, k):
        p = pos_ref[(step * tm + t) * TOP_K + k]
        return pltpu.make_async_copy(ys_ref.at[pl.ds(p, 1), :], buf_ref.at[slot, k, pl.ds(t, 1), :], sems.at[slot])

    def issue(step, slot):
        def body(g, _):
            for u in range(SUBLANES):
                for k in range(TOP_K):
                    row_copy(step, slot, g * SUBLANES + u, k).start()
            return 0
        lax.fori_loop(0, tm // SUBLANES, body, 0)

    @pl.when(i == 0)
    def _():
        issue(0, 0)

    @pl.when(i + 1 < n_steps)
    def _():
        issue(i + 1, (i + 1) % 2)

    slot = i % 2
    for k in range(TOP_K):
        pltpu.make_async_copy(ys_ref.at[pl.ds(0, tm), :], buf_ref.at[slot, k], sems.at[slot]).wait()
    acc = x2_ref[...]
    for k in range(TOP_K):
        acc = acc + gate_ref[:, k:k + 1] * buf_ref[slot, k]
    y_ref[...] = _rmsnorm(acc, gfin_ref[...])


def _combine(pos_flat, ys, x2, gates, gfin, tm):
    n, d = x2.shape
    return pl.pallas_call(
        functools.partial(_combine_kernel, tm=tm),
        grid_spec=pltpu.PrefetchScalarGridSpec(
            num_scalar_prefetch=1,
            grid=(n // tm,),
            in_specs=[pl.BlockSpec(memory_space=pl.ANY),
                      pl.BlockSpec((tm, d), lambda i, p: (i, 0)),
                      pl.BlockSpec((tm, TOP_K), lambda i, p: (i, 0)),
                      pl.BlockSpec(gfin.shape, lambda i, p: (0, 0))],
            out_specs=pl.BlockSpec((tm, d), lambda i, p: (i, 0)),
            scratch_shapes=[pltpu.VMEM((2, TOP_K, tm, d), F32), pltpu.SemaphoreType.DMA((2,))],
        ),
        out_shape=jax.ShapeDtypeStruct((n, d), F32),
        compiler_params=_params(("arbitrary",)),
        name="combine",
    )(pos_flat, ys, x2, gates, gfin)


def _pick(n, pref):
    t = min(pref, n)
    while n % t:
        t //= 2
    return t


def _layer(x2d, mem, batch, seq, g_mix, w_in, w_gla_gate, b_gla_gate, g_gla_head, w_out, g_mem_q, g_mem_kv,
           w_mq, w_mk, w_mv, w_mo, g_ffn, w_router, b_router, w_gu, b_gu, w_down, b_down, g_out):
    n, d = x2d.shape
    row = lambda v: v.reshape(1, -1).astype(F32)

    o_qg, o_kg, o_vg, o_lr, o_rg, o_qs, o_ks, o_vs = np.cumsum((0, Q_G, Q_G, V_G, GLA_GATE_RANK, V_G, SB_W, SB_W))
    lr_pad = jnp.zeros((d, LANES - GLA_GATE_RANK), w_in.dtype)
    wg = jnp.concatenate([w_in[:, o_qg:o_vg + V_G], w_in[:, o_rg:o_rg + V_G],
                          w_in[:, o_lr:o_lr + GLA_GATE_RANK], lr_pad], axis=1).astype(BF16)
    wq, wk, wv = (w_in[:, o:o + SB_W].astype(BF16) for o in (o_qs, o_ks, o_vs))
    wgate = jnp.concatenate([w_gla_gate, jnp.zeros((LANES - GLA_GATE_RANK, Q_G), F32)], axis=0)

    slab, qs, ks, vs = _in_proj(x2d, row(g_mix), wg, wq, wk, wv, _pick(n, 512))
    o_gla = _gla(slab, wgate, row(b_gla_gate), row(g_gla_head), batch, seq, _pick(seq, 512))
    o_sb = _sb(qs, ks, vs, batch, seq, 2 * LANES)
    km, vm = _mem_kv(mem, row(g_mem_kv), w_mk.astype(BF16), w_mv.astype(BF16))
    w_out_bf = w_out.astype(BF16)
    x2, hf = _mid(x2d, o_gla, o_sb, w_out_bf[:V_G], w_out_bf[V_G:], row(g_mem_q), w_mq.astype(BF16), km, vm,
                  w_mo.astype(BF16), row(g_ffn), seq, _pick(seq, 512))

    tm_r = _pick(n, 256)
    idx, gates, rank, cnt = _router(hf, w_router.T, b_router.reshape(-1, 1), tm_r)

    blk = 512
    counts = cnt[:, 0].astype(jnp.int32)
    padded = ((counts + blk - 1) // blk) * blk
    pad_end = jnp.cumsum(padded)
    pad_start = pad_end - padded
    first = jnp.sum(jnp.where(idx[..., None] == jnp.arange(N_EXPERTS), pad_start, 0), axis=-1)
    pos = (first + rank).T.reshape(-1)
    n_steps = -(-(n * TOP_K + N_EXPERTS * (blk - 1)) // blk)
    m_pad = n_steps * blk
    n_valid = pad_end[-1] // blk
    blk_idx = jnp.arange(n_steps, dtype=jnp.int32)
    used = jnp.minimum(blk_idx, n_valid - 1)
    blk_exp = jnp.minimum(jnp.sum(used[:, None] * blk >= pad_end[None, :], axis=1), N_EXPERTS - 1).astype(jnp.int32)
    run_first = jnp.logical_and(blk_idx < n_valid, blk_exp != jnp.concatenate([blk_exp[:1] - 1, blk_exp[:-1]]))
    run_first = run_first.astype(jnp.int32)
    run_slot = (jnp.cumsum(run_first) - 1) % 2
    e_ids = jnp.arange(N_EXPERTS, dtype=jnp.int32)
    later = jnp.where(jnp.logical_and(padded[None, :] > 0, e_ids[None, :] > e_ids[:, None]), e_ids[None, :], N_EXPERTS)
    following = jnp.min(later, axis=1)
    run_next = jnp.where(following == N_EXPERTS, -1, following)[blk_exp]
    n_valid = n_valid.reshape(1)

    xs = _dispatch(pos, pad_start + counts, padded - counts, n_valid, hf, m_pad, _pick(n, 512), blk)
    ys = _experts(blk_exp, run_first, run_slot.astype(jnp.int32), run_next.astype(jnp.int32), n_valid, xs,
                  w_gu, b_gu, w_down, b_down, blk)
    return _combine(pos, ys, x2, gates.T, row(g_out), _pick(n, 256))


def kernel(x, mem, g_mix, w_in, w_gla_gate, b_gla_gate, g_gla_head, w_out, g_mem_q, g_mem_kv, w_mq, w_mk, w_mv,
           w_mo, g_ffn, w_router, b_router, w_gu, b_gu, w_down, b_down, g_final):
    batch, seq, d = x.shape
    depth = g_mix.shape[0]
    assert depth == 1, "the final rmsnorm is fused into the single layer's combine step"
    y = _layer(x.reshape(batch * seq, d), mem, batch, seq, g_mix[0], w_in[0], w_gla_gate[0], b_gla_gate[0],
               g_gla_head[0], w_out[0], g_mem_q[0], g_mem_kv[0], w_mq[0], w_mk[0], w_mv[0], w_mo[0], g_ffn[0],
               w_router[0], b_router[0], w_gu[0], b_gu[0], w_down[0], b_down[0], g_final)
    return y.reshape(batch, seq, d)
```

```python
import functools

import jax
import jax.numpy as jnp
import numpy as np
from jax import lax
from jax.experimental import pallas as pl
from jax.experimental.pallas import tpu as pltpu
from jax.experimental.pallas import tpu_sc as plsc

F32 = jnp.float32
BF16 = jnp.bfloat16

EPS = 1e-5
CHUNK = 64
GLA_HEADS = 4
GLA_DK = 64
GLA_DV = 128
GLA_GATE_RANK = 16
GLA_TAU = 16.0
SB_HEADS = 8
SB_HD = 64
MEM_HEADS = 4
N_EXPERTS = 32
TOP_K = 4
SWIGLU_LIMIT = 7.0
SWIGLU_ALPHA = 1.702
SB_LOG_UNDERFLOW = -104.0

LANES = 128
Q_G = GLA_HEADS * GLA_DK
V_G = GLA_HEADS * GLA_DV
SB_W = SB_HEADS * SB_HD
GLA_SLAB = Q_G + Q_G + V_G + V_G + LANES

VMEM_LIMIT = 56 * 1024 * 1024
SUBLANES = 8

NT_DIMS = (((1,), (1,)), ((), ()))
TN_DIMS = (((0,), (0,)), ((), ()))


def _dot(a, b):
    return jnp.dot(a, b, preferred_element_type=F32)


def _split(x):
    hi = x.astype(BF16)
    lo = (x - hi.astype(F32)).astype(BF16)
    return hi, lo


def _dot_exact_lhs(a_bf16, b_f32):
    hi, lo = _split(b_f32)
    return _dot(a_bf16, hi) + _dot(a_bf16, lo)


def _dot3(a, b):
    ah, al = _split(a)
    bh, bl = _split(b)
    return _dot(ah, bh) + (_dot(ah, bl) + _dot(al, bh))


def _rmsnorm(x, g):
    return x * lax.rsqrt(jnp.mean(x * x, axis=-1, keepdims=True) + EPS) * g


def _softplus(z):
    return jnp.maximum(z, 0.0) + jnp.log(1.0 + jnp.exp(-jnp.abs(z)))


def _params(sem, vmem=VMEM_LIMIT):
    return pltpu.CompilerParams(dimension_semantics=sem, vmem_limit_bytes=vmem)


def _inproj_kernel(x_ref, g_ref, wg_ref, wq_ref, wk_ref, wv_ref, slab_ref, q_ref, k_ref, v_ref):
    h = _rmsnorm(x_ref[...], g_ref[...]).astype(BF16)
    slab_ref[...] = _dot(h, wg_ref[...])
    q_ref[...] = (_dot(h, wq_ref[...]) * (SB_HD ** -0.5)).astype(BF16)
    k_ref[...] = _dot(h, wk_ref[...]).astype(BF16)
    v_ref[...] = _dot(h, wv_ref[...]).astype(BF16)


def _in_proj(x2d, g, wg, wq, wk, wv, tm):
    n, d = x2d.shape
    full = lambda a: pl.BlockSpec(a.shape, lambda i: (0, 0))
    return pl.pallas_call(
        _inproj_kernel,
        grid=(n // tm,),
        in_specs=[pl.BlockSpec((tm, d), lambda i: (i, 0)), full(g), full(wg), full(wq), full(wk), full(wv)],
        out_specs=[pl.BlockSpec((tm, GLA_SLAB), lambda i: (i, 0))] + [pl.BlockSpec((tm, SB_W), lambda i: (i, 0))] * 3,
        out_shape=[jax.ShapeDtypeStruct((n, GLA_SLAB), F32)] + [jax.ShapeDtypeStruct((n, SB_W), BF16)] * 3,
        compiler_params=_params(("parallel",)),
        name="in_proj",
    )(x2d, g, wg, wq, wk, wv)


def _gla_kernel(slab_ref, wgate_ref, bgate_ref, ghead_ref, umat_ref, o_ref, state_ref, *, n_chunks):
    @pl.when(pl.program_id(1) == 0)
    def _():
        state_ref[...] = jnp.zeros_like(state_ref)

    qg = slab_ref[:, 0:Q_G] * (GLA_DK ** -0.5)
    kg = slab_ref[:, Q_G:2 * Q_G]
    glr = slab_ref[:, 2 * Q_G + 2 * V_G:GLA_SLAB]
    log_a = -_softplus(-(_dot3(glr, wgate_ref[...]) + bgate_ref[...])) * (1.0 / GLA_TAU)
    to_end = _dot_exact_lhs(umat_ref[...], log_a)
    kdec = kg * jnp.exp(to_end)
    g_chunk = to_end + log_a

    lane = lax.broadcasted_iota(jnp.int32, (CHUNK, LANES), 1)
    ghead = ghead_ref[...]
    for c in range(n_chunks):
        rows = slice(c * CHUNK, (c + 1) * CHUNK)
        for h in range(GLA_HEADS):
            pair = slice((h // 2) * LANES, (h // 2 + 1) * LANES)
            mine = (lane >= GLA_DK) if h % 2 else (lane < GLA_DK)
            kd = jnp.where(mine, kdec[rows, pair], 0.0).astype(BF16)
            qm = jnp.where(mine, qg[rows, pair], 0.0).astype(BF16)
            vh = slab_ref[rows, 2 * Q_G + h * GLA_DV:2 * Q_G + (h + 1) * GLA_DV].astype(BF16)
            decay = jnp.exp(g_chunk[c * CHUNK:c * CHUNK + 1, pair])
            st = decay * state_ref[h] + lax.dot_general(vh, kd, TN_DIMS, preferred_element_type=F32)
            state_ref[h] = st
            o = lax.dot_general(qm, st.astype(BF16), NT_DIMS, preferred_element_type=F32)
            rg = slab_ref[rows, 2 * Q_G + V_G + h * GLA_DV:2 * Q_G + V_G + (h + 1) * GLA_DV]
            o = _rmsnorm(o, ghead) * (rg * jax.nn.sigmoid(rg))
            o_ref[rows, h * GLA_DV:(h + 1) * GLA_DV] = o.astype(BF16)


def _gla(slab, wgate, bgate, ghead, batch, seq, ts):
    n = slab.shape[0]
    n_chunks = ts // CHUNK
    r = np.arange(ts)
    umat = jnp.asarray((r[None, :] > r[:, None]) & (r[None, :] // CHUNK == r[:, None] // CHUNK), BF16)
    full = lambda a: pl.BlockSpec(a.shape, lambda b, i: (0,) * a.ndim)
    steps = seq // ts
    return pl.pallas_call(
        functools.partial(_gla_kernel, n_chunks=n_chunks),
        grid=(batch, steps),
        in_specs=[pl.BlockSpec((ts, GLA_SLAB), lambda b, i: (b * steps + i, 0)),
                  full(wgate), full(bgate), full(ghead), full(umat)],
        out_specs=pl.BlockSpec((ts, V_G), lambda b, i: (b * steps + i, 0)),
        out_shape=jax.ShapeDtypeStruct((n, V_G), BF16),
        scratch_shapes=[pltpu.VMEM((GLA_HEADS, GLA_DV, LANES), F32)],
        compiler_params=_params(("parallel", "arbitrary")),
        name="gla",
    )(slab, wgate, bgate, ghead, umat)


def _sb_kernel(q_ref, k_ref, v_ref, tmat_ref, o_ref, acc_ref, carry_ref, *, blk):
    i = pl.program_id(2)
    sub = blk // 2
    lane = lax.broadcasted_iota(jnp.int32, (blk, LANES), 1)
    causal = lax.broadcasted_iota(jnp.int32, (blk, blk), 1) < lax.broadcasted_iota(jnp.int32, (blk, blk), 0)
    q = q_ref[...]
    lo_head = lane < SB_HD
    q_heads = (jnp.where(lo_head, q, jnp.zeros_like(q)), jnp.where(lo_head, jnp.zeros_like(q), q))
    tmat = tmat_ref[...]

    def chunk(j, diag):
        start = pl.multiple_of(j * blk, blk)
        kj = k_ref[pl.ds(start, blk), :]
        vj = v_ref[pl.ds(start, blk), :]
        worst = None
        for h in range(2):
            z = lax.dot_general(q_heads[h], kj, NT_DIMS, preferred_element_type=F32)
            sp = _softplus(z)
            log1mb = -sp
            if diag:
                log1mb = jnp.where(causal, log1mb, 0.0)
            hi, lo = _split(log1mb)
            right = _dot(jnp.concatenate([hi[:, sub:], lo[:, sub:]], axis=1), tmat)
            left = _dot(jnp.concatenate([hi[:, :sub], lo[:, :sub]], axis=1), tmat)
            log_sig = z - sp
            if diag:
                after_right = right[:, sub:]
                log_a_right = log_sig[:, sub:] + right[:, :sub]
            else:
                carry = carry_ref[h]
                after_right = carry + right[:, sub:]
                log_a_right = log_sig[:, sub:] + right[:, :sub] + carry
            log_a_left = log_sig[:, :sub] + left[:, :sub] + after_right
            a = jnp.exp(jnp.concatenate([log_a_left, log_a_right], axis=1))
            if diag:
                a = jnp.where(causal, a, 0.0)
            pv = _dot(a.astype(BF16), vj)
            acc_ref[h] = pv if diag else acc_ref[h] + pv
            remaining = after_right + left[:, sub:]
            carry_ref[h] = remaining
            top = jnp.max(remaining)
            worst = top if worst is None else jnp.maximum(worst, top)
        return worst

    def more(state):
        return jnp.logical_and(state[0] >= 0, state[1] > SB_LOG_UNDERFLOW)

    def first_two():
        chunk(i, True)
        return chunk(i - 1, False)

    start = lax.cond(i >= 1, first_two, lambda: chunk(i, True))
    lax.while_loop(more, lambda state: (state[0] - 1, chunk(state[0], False)), (i - 2, start))
    o_ref[...] = jnp.where(lo_head, acc_ref[0], acc_ref[1]).astype(BF16)


def _sb(q, k, v, batch, seq, blk):
    sub = blk // 2
    assert sub == LANES
    r = np.arange(sub)
    tri = (r[:, None] > r[None, :]).astype(np.float32)
    half = np.concatenate([tri, np.ones((sub, sub), np.float32)], axis=1)
    tmat = jnp.asarray(np.concatenate([half, half], axis=0), BF16)
    nq = seq // blk
    pairs = SB_W // LANES
    q3, k3, v3 = (a.reshape(batch, seq, SB_W) for a in (q, k, v))
    out = pl.pallas_call(
        functools.partial(_sb_kernel, blk=blk),
        grid=(batch, pairs, nq),
        in_specs=[pl.BlockSpec((None, blk, LANES), lambda b, p, i: (b, i, p)),
                  pl.BlockSpec((None, seq, LANES), lambda b, p, i: (b, 0, p)),
                  pl.BlockSpec((None, seq, LANES), lambda b, p, i: (b, 0, p)),
                  pl.BlockSpec(tmat.shape, lambda b, p, i: (0, 0))],
        out_specs=pl.BlockSpec((None, blk, LANES), lambda b, p, i: (b, i, p)),
        out_shape=jax.ShapeDtypeStruct((batch, seq, SB_W), BF16),
        scratch_shapes=[pltpu.VMEM((2, blk, LANES), F32), pltpu.VMEM((2, blk, LANES), F32)],
        compiler_params=_params(("parallel", "parallel", "arbitrary")),
        name="sb",
    )(q3, k3, v3, tmat)
    return out.reshape(batch * seq, SB_W)


def _memkv_kernel(m_ref, g_ref, wk_ref, wv_ref, k_ref, v_ref):
    hm = _rmsnorm(m_ref[...], g_ref[...]).astype(BF16)
    k_ref[...] = _dot(hm, wk_ref[...]).astype(BF16)
    v_ref[...] = _dot(hm, wv_ref[...]).astype(BF16)


def _mem_kv(mem, g, wk, wv):
    b, m, d = mem.shape
    full = lambda a: pl.BlockSpec(a.shape, lambda i: (0, 0))
    blk = pl.BlockSpec((None, m, d), lambda i: (i, 0, 0))
    return pl.pallas_call(
        _memkv_kernel,
        grid=(b,),
        in_specs=[blk, full(g), full(wk), full(wv)],
        out_specs=[blk, blk],
        out_shape=[jax.ShapeDtypeStruct((b, m, d), BF16)] * 2,
        compiler_params=_params(("parallel",)),
        name="mem_kv",
    )(mem, g, wk, wv)


def _mid_kernel(x_ref, og_ref, os_ref, wog_ref, wos_ref, gq_ref, wmq_ref, km_ref, vm_ref, wmo_ref, gf_ref,
                x2_ref, hf_ref):
    x1 = x_ref[...] + _dot(og_ref[...], wog_ref[...]) + _dot(os_ref[...], wos_ref[...])
    hq = _rmsnorm(x1, gq_ref[...]).astype(BF16)
    d = x1.shape[-1]
    hd = d // MEM_HEADS
    q = (_dot(hq, wmq_ref[...]) * (hd ** -0.5)).astype(BF16)
    outs = []
    for h in range(MEM_HEADS):
        cols = slice(h * hd, (h + 1) * hd)
        s = lax.dot_general(q[:, cols], km_ref[:, cols], NT_DIMS, preferred_element_type=F32)
        e = jnp.exp(s - jnp.max(s, axis=-1, keepdims=True))
        p = (e / jnp.sum(e, axis=-1, keepdims=True)).astype(BF16)
        outs.append(_dot(p, vm_ref[:, cols]).astype(BF16))
    x2 = x1 + _dot(jnp.concatenate(outs, axis=1), wmo_ref[...])
    x2_ref[...] = x2
    hf_ref[...] = _rmsnorm(x2, gf_ref[...])


def _mid(x2d, og, osb, wog, wos, gq, wmq, km, vm, wmo, gf, seq, tm):
    n, d = x2d.shape
    m = km.shape[1]
    per_batch = seq // tm
    full = lambda a: pl.BlockSpec(a.shape, lambda i: (0, 0))
    rows = lambda w: pl.BlockSpec((tm, w), lambda i: (i, 0))
    mem = pl.BlockSpec((None, m, d), lambda i: (i // per_batch, 0, 0))
    return pl.pallas_call(
        _mid_kernel,
        grid=(n // tm,),
        in_specs=[rows(d), rows(V_G), rows(SB_W), full(wog), full(wos), full(gq), full(wmq), mem, mem,
                  full(wmo), full(gf)],
        out_specs=[rows(d), rows(d)],
        out_shape=[jax.ShapeDtypeStruct((n, d), F32)] * 2,
        compiler_params=_params(("parallel",)),
        name="mid",
    )(x2d, og, osb, wog, wos, gq, wmq, km, vm, wmo, gf)


def _router_kernel(hf_ref, wrt_ref, br_ref, cmat_ref, idx_ref, gate_ref, rank_ref, cnt_ref, carry_ref, *, tm):
    @pl.when(pl.program_id(0) == 0)
    def _():
        carry_ref[...] = jnp.zeros_like(carry_ref)

    hh, hl = _split(hf_ref[...])
    wh, wl = _split(wrt_ref[...])
    nt = lambda a, b: lax.dot_general(a, b, NT_DIMS, preferred_element_type=F32)
    vals = nt(wh, hh) + (nt(wh, hl) + nt(wl, hh)) + br_ref[...]
    eidx = lax.broadcasted_iota(jnp.int32, (N_EXPERTS, tm), 0)
    tops, sels, hots = [], [], []
    for _ in range(TOP_K):
        m = jnp.max(vals, axis=0, keepdims=True)
        sel = jnp.min(jnp.where(vals == m, eidx, N_EXPERTS), axis=0, keepdims=True)
        hot = eidx == sel
        vals = jnp.where(hot, -jnp.inf, vals)
        tops.append(m)
        sels.append(sel)
        hots.append(hot)
    exps = [jnp.exp(t - tops[0]) for t in tops]
    denom = exps[0] + exps[1] + exps[2] + exps[3]
    chosen = jnp.zeros((N_EXPERTS, tm), F32)
    for hot in hots:
        chosen = chosen + hot.astype(F32)
    sums = _dot(chosen.astype(BF16), cmat_ref[...])
    before = sums[:, :tm] + carry_ref[...]
    for k in range(TOP_K):
        idx_ref[k:k + 1, :] = sels[k]
        gate_ref[k:k + 1, :] = exps[k] / denom
        rank_ref[k:k + 1, :] = jnp.sum(jnp.where(hots[k], before, 0.0), axis=0, keepdims=True).astype(jnp.int32)
    carry_ref[...] = carry_ref[...] + sums[:, tm:]
    cnt_ref[...] = carry_ref[...]


def _router(hf, wrt, br, tm):
    n, d = hf.shape
    r = np.arange(tm)
    cmat = jnp.asarray(np.concatenate([(r[:, None] < r[None, :]).astype(np.float32),
                                       np.ones((tm, tm), np.float32)], axis=1), BF16)
    full = lambda a: pl.BlockSpec(a.shape, lambda i: (0, 0))
    tok = pl.BlockSpec((TOP_K, tm), lambda i: (0, i))
    return pl.pallas_call(
        functools.partial(_router_kernel, tm=tm),
        grid=(n // tm,),
        in_specs=[pl.BlockSpec((tm, d), lambda i: (i, 0)), full(wrt), full(br), full(cmat)],
        out_specs=[tok, tok, tok, pl.BlockSpec((N_EXPERTS, tm), lambda i: (0, 0))],
        out_shape=[jax.ShapeDtypeStruct((TOP_K, n), jnp.int32), jax.ShapeDtypeStruct((TOP_K, n), F32),
                   jax.ShapeDtypeStruct((TOP_K, n), jnp.int32), jax.ShapeDtypeStruct((N_EXPERTS, tm), F32)],
        scratch_shapes=[pltpu.VMEM((N_EXPERTS, tm), F32)],
        compiler_params=_params(("arbitrary",)),
        name="router",
    )(hf, wrt, br, cmat)


def _dispatch_kernel(pos_ref, fill_start_ref, fill_n_ref, nvalid_ref, hf_ref, xs_ref, zero_ref, stage_ref, sem,
                     row_sems, load_sems, *, tm, blk):
    i = pl.program_id(0)
    n_blocks = xs_ref.shape[0] // blk

    def block_copy(b, on=None):
        return pltpu.make_async_copy(zero_ref, xs_ref.at[pl.ds(pl.multiple_of(b * blk, blk), blk), :],
                                     sem if on is None else on)

    def row_copy(src_row, p, on=None):
        return pltpu.make_async_copy(src_row, xs_ref.at[pl.ds(p, 1), :], sem if on is None else on)

    zero_row = zero_ref.at[pl.ds(0, 1), :]

    @pl.when(i == 0)
    def _():
        zero_ref[...] = jnp.zeros_like(zero_ref)
        for e in range(N_EXPERTS):
            def fill(r, _):
                row_copy(zero_row, fill_start_ref[e] + r).start()
                return 0
            lax.fori_loop(0, fill_n_ref[e], fill, 0)
        for e in range(N_EXPERTS):
            def drain(r, _):
                row_copy(zero_row, fill_start_ref[e] + r).wait()
                return 0
            lax.fori_loop(0, fill_n_ref[e], drain, 0)
        lax.fori_loop(nvalid_ref[0], n_blocks, lambda b, _: (block_copy(b).start(), 0)[1], 0)
        lax.fori_loop(nvalid_ref[0], n_blocks, lambda b, _: (block_copy(b).wait(), 0)[1], 0)

    base = i * tm

    n_tiles = pl.num_programs(0)
    rows_per_tile = tm // SUBLANES

    def tile_load(j):
        s = j % 3
        return pltpu.make_async_copy(hf_ref.at[pl.ds(j * rows_per_tile, rows_per_tile)], stage_ref.at[s],
                                     load_sems.at[s])

    @pl.when(i == 0)
    def _():
        tile_load(0).start()

    @pl.when(i + 1 < n_tiles)
    def _():
        tile_load(i + 1).start()

    tile_load(i).wait()
    mine = row_sems.at[i % 2]
    tile = stage_ref.at[i % 3]

    def issue(g, _):
        for u in range(SUBLANES):
            t = base + g * SUBLANES + u
            for k in range(TOP_K):
                row_copy(tile.at[g, pl.ds(u, 1), :], pos_ref[t * TOP_K + k], mine).start()
        return 0

    lax.fori_loop(0, rows_per_tile, issue, 0)

    def drain(on):
        for _ in range(tm * TOP_K // blk):
            block_copy(0, on).wait()

    @pl.when(i >= 1)
    def _():
        drain(row_sems.at[(i + 1) % 2])

    @pl.when(i == n_tiles - 1)
    def _():
        drain(mine)


def _dispatch(pos_flat, fill_start, fill_n, n_valid, hf, m_pad, tm, blk):
    n, d = hf.shape
    assert (tm * TOP_K) % blk == 0
    return pl.pallas_call(
        functools.partial(_dispatch_kernel, tm=tm, blk=blk),
        grid_spec=pltpu.PrefetchScalarGridSpec(
            num_scalar_prefetch=4,
            grid=(n // tm,),
            in_specs=[pl.BlockSpec(memory_space=pl.ANY)],
            out_specs=pl.BlockSpec(memory_space=pl.ANY),
            scratch_shapes=[pltpu.VMEM((blk, d), F32), pltpu.VMEM((3, tm // SUBLANES, SUBLANES, d), F32),
                            pltpu.SemaphoreType.DMA, pltpu.SemaphoreType.DMA((2,)), pltpu.SemaphoreType.DMA((3,))],
        ),
        out_shape=jax.ShapeDtypeStruct((m_pad, d), F32),
        compiler_params=_params(("arbitrary",)),
        name="dispatch",
    )(pos_flat, fill_start, fill_n, n_valid, hf.reshape(n // SUBLANES, SUBLANES, d))


def _experts_kernel(exp_ref, first_ref, slot_ref, next_ref, nvalid_ref, xs_ref, wgu_hbm, bgu_ref, wd_hbm, bd_ref,
                    ys_ref, wgu_f32, wd_f32, wgu_bf, wd_bf, wsem):
    i = pl.program_id(0)
    valid = i < nvalid_ref[0]
    slot = slot_ref[i]

    def weight_copies(e, s):
        return (pltpu.make_async_copy(wgu_hbm.at[e], wgu_f32.at[s], wsem.at[s]),
                pltpu.make_async_copy(wd_hbm.at[e], wd_f32.at[s], wsem.at[s]))

    @pl.when(jnp.logical_and(valid, first_ref[i] == 1))
    def _():
        @pl.when(i == 0)
        def _():
            for c in weight_copies(exp_ref[0], 0):
                c.start()

        for c in weight_copies(exp_ref[i], slot):
            c.wait()
        wgu_bf[...] = wgu_f32[slot].astype(BF16)
        wd_bf[...] = wd_f32[slot].astype(BF16)

        @pl.when(next_ref[i] >= 0)
        def _():
            for c in weight_copies(next_ref[i], 1 - slot):
                c.start()

    @pl.when(valid)
    def _():
        f = wd_bf.shape[0]
        gu = _dot(xs_ref[...].astype(BF16), wgu_bf[...]) + bgu_ref[...]
        gate = jnp.minimum(gu[:, :f], SWIGLU_LIMIT)
        lin = jnp.clip(gu[:, f:], -SWIGLU_LIMIT, SWIGLU_LIMIT)
        act = (lin + 1.0) * (gate * jax.nn.sigmoid(SWIGLU_ALPHA * gate))
        ys_ref[...] = _dot(act.astype(BF16), wd_bf[...]) + bd_ref[...]

    @pl.when(jnp.logical_not(valid))
    def _():
        ys_ref[...] = jnp.zeros_like(ys_ref)


def _experts(blk_exp, run_first, run_slot, run_next, n_valid, xs, w_gu, b_gu, w_down, b_down, blk):
    m_pad, d = xs.shape
    e, _, f2 = w_gu.shape
    f = f2 // 2
    n_steps = m_pad // blk
    rows = pl.BlockSpec((blk, d), lambda i, *_: (i, 0))
    return pl.pallas_call(
        _experts_kernel,
        grid_spec=pltpu.PrefetchScalarGridSpec(
            num_scalar_prefetch=5,
            grid=(n_steps,),
            in_specs=[rows,
                      pl.BlockSpec(memory_space=pl.ANY),
                      pl.BlockSpec((None, 1, f2), lambda i, x, *_: (x[i], 0, 0)),
                      pl.BlockSpec(memory_space=pl.ANY),
                      pl.BlockSpec((None, 1, d), lambda i, x, *_: (x[i], 0, 0))],
            out_specs=rows,
            scratch_shapes=[pltpu.VMEM((2, d, f2), F32), pltpu.VMEM((2, f, d), F32),
                            pltpu.VMEM((d, f2), BF16), pltpu.VMEM((f, d), BF16),
                            pltpu.SemaphoreType.DMA((2,))],
        ),
        out_shape=jax.ShapeDtypeStruct((m_pad, d), F32),
        compiler_params=_params(("arbitrary",)),
        name="experts",
    )(blk_exp, run_first, run_slot, run_next, n_valid, xs, w_gu, b_gu.reshape(e, 1, f2), w_down,
      b_down.reshape(e, 1, d))


SC_CHUNK = 32


def _gather_rows(table, idx):
    info = plsc.get_sparse_core_info()
    n_workers = info.num_cores * info.num_subcores
    n_rows, d = idx.shape[0], table.shape[1]
    per_worker = n_rows // n_workers
    n_chunks = per_worker // SC_CHUNK
    assert n_rows == n_workers * n_chunks * SC_CHUNK and n_chunks % 2 == 0

    def body(table_hbm, idx_hbm, out_hbm, idx_v, rows_v, gsem, wsem):
        wid = lax.axis_index("s") * info.num_cores + lax.axis_index("c")
        base = wid * per_worker
        pltpu.sync_copy(idx_hbm.at[wid], idx_v)

        def gather(j, b):
            return pltpu.make_async_copy(table_hbm.at[idx_v.at[j]], rows_v.at[b], gsem.at[b])

        def put(j, b):
            return pltpu.make_async_copy(rows_v.at[b], out_hbm.at[pl.ds(base + j * SC_CHUNK, SC_CHUNK)], wsem.at[b])

        gather(0, 0).start()

        @pl.loop(0, n_chunks, step=2)
        def _(j0):
            for b in range(2):
                j = j0 + b

                @pl.when(j >= 1)
                def _():
                    put(j - 1, 1 - b).wait()

                @pl.when(j + 1 < n_chunks)
                def _():
                    gather(j + 1, 1 - b).start()

                gather(j, b).wait()
                put(j, b).start()

        put(n_chunks - 1, 1).wait()

    mesh = plsc.VectorSubcoreMesh(core_axis_name="c", subcore_axis_name="s")
    call = pl.kernel(
        body,
        out_type=jax.ShapeDtypeStruct((n_rows, d), table.dtype),
        mesh=mesh,
        scratch_types=[pltpu.VMEM((n_chunks, SC_CHUNK), jnp.int32), pltpu.VMEM((2, SC_CHUNK, d), table.dtype),
                       pltpu.SemaphoreType.DMA((2,)), pltpu.SemaphoreType.DMA((2,))],
    )
    return call(table, idx.reshape(n_workers, n_chunks, SC_CHUNK))


def _combine_kernel(o0_ref, o1_ref, o2_ref, o3_ref, x2_ref, gate_ref, gfin_ref, y_ref):
    acc = x2_ref[...]
    for k, o_ref in enumerate((o0_ref, o1_ref, o2_ref, o3_ref)):
        acc = acc + gate_ref[:, k:k + 1] * o_ref[...]
    y_ref[...] = _rmsnorm(acc, gfin_ref[...])


def _combine(rows, x2, gates, gfin, tm):
    n, d = x2.shape
    per_k = n // tm
    planes = [pl.BlockSpec((tm, d), functools.partial(lambda i, k: (k * per_k + i, 0), k=k)) for k in range(TOP_K)]
    return pl.pallas_call(
        _combine_kernel,
        grid=(per_k,),
        in_specs=planes + [pl.BlockSpec((tm, d), lambda i: (i, 0)),
                           pl.BlockSpec((tm, TOP_K), lambda i: (i, 0)),
                           pl.BlockSpec(gfin.shape, lambda i: (0, 0))],
        out_specs=pl.BlockSpec((tm, d), lambda i: (i, 0)),
        out_shape=jax.ShapeDtypeStruct((n, d), F32),
        compiler_params=_params(("parallel",)),
        name="combine",
    )(rows, rows, rows, rows, x2, gates, gfin)


def _pick(n, pref):
    t = min(pref, n)
    while n % t:
        t //= 2
    return t


def _layer(x2d, mem, batch, seq, g_mix, w_in, w_gla_gate, b_gla_gate, g_gla_head, w_out, g_mem_q, g_mem_kv,
           w_mq, w_mk, w_mv, w_mo, g_ffn, w_router, b_router, w_gu, b_gu, w_down, b_down, g_out):
    n, d = x2d.shape
    row = lambda v: v.reshape(1, -1).astype(F32)

    o_qg, o_kg, o_vg, o_lr, o_rg, o_qs, o_ks, o_vs = np.cumsum((0, Q_G, Q_G, V_G, GLA_GATE_RANK, V_G, SB_W, SB_W))
    lr_pad = jnp.zeros((d, LANES - GLA_GATE_RANK), w_in.dtype)
    wg = jnp.concatenate([w_in[:, o_qg:o_vg + V_G], w_in[:, o_rg:o_rg + V_G],
                          w_in[:, o_lr:o_lr + GLA_GATE_RANK], lr_pad], axis=1).astype(BF16)
    wq, wk, wv = (w_in[:, o:o + SB_W].astype(BF16) for o in (o_qs, o_ks, o_vs))
    wgate = jnp.concatenate([w_gla_gate, jnp.zeros((LANES - GLA_GATE_RANK, Q_G), F32)], axis=0)

    slab, qs, ks, vs = _in_proj(x2d, row(g_mix), wg, wq, wk, wv, _pick(n, 512))
    o_gla = _gla(slab, wgate, row(b_gla_gate), row(g_gla_head), batch, seq, _pick(seq, 512))
    o_sb = _sb(qs, ks, vs, batch, seq, 2 * LANES)
    km, vm = _mem_kv(mem, row(g_mem_kv), w_mk.astype(BF16), w_mv.astype(BF16))
    w_out_bf = w_out.astype(BF16)
    x2, hf = _mid(x2d, o_gla, o_sb, w_out_bf[:V_G], w_out_bf[V_G:], row(g_mem_q), w_mq.astype(BF16), km, vm,
                  w_mo.astype(BF16), row(g_ffn), seq, _pick(seq, 512))

    tm_r = _pick(n, 256)
    idx, gates, rank, cnt = _router(hf, w_router.T, b_router.reshape(-1, 1), tm_r)

    blk = 512
    counts = cnt[:, 0].astype(jnp.int32)
    padded = ((counts + blk - 1) // blk) * blk
    pad_end = jnp.cumsum(padded)
    pad_start = pad_end - padded
    first = jnp.sum(jnp.where(idx[..., None] == jnp.arange(N_EXPERTS), pad_start, 0), axis=-1)
    slot = first + rank
    pos = slot.T.reshape(-1)
    n_steps = -(-(n * TOP_K + N_EXPERTS * (blk - 1)) // blk)
    m_pad = n_steps * blk
    n_valid = pad_end[-1] // blk
    blk_idx = jnp.arange(n_steps, dtype=jnp.int32)
    used = jnp.minimum(blk_idx, n_valid - 1)
    blk_exp = jnp.minimum(jnp.sum(used[:, None] * blk >= pad_end[None, :], axis=1), N_EXPERTS - 1).astype(jnp.int32)
    run_first = jnp.logical_and(blk_idx < n_valid, blk_exp != jnp.concatenate([blk_exp[:1] - 1, blk_exp[:-1]]))
    run_first = run_first.astype(jnp.int32)
    run_slot = (jnp.cumsum(run_first) - 1) % 2
    e_ids = jnp.arange(N_EXPERTS, dtype=jnp.int32)
    later = jnp.where(jnp.logical_and(padded[None, :] > 0, e_ids[None, :] > e_ids[:, None]), e_ids[None, :], N_EXPERTS)
    following = jnp.min(later, axis=1)
    run_next = jnp.where(following == N_EXPERTS, -1, following)[blk_exp]
    n_valid = n_valid.reshape(1)

    xs = _dispatch(pos, pad_start + counts, padded - counts, n_valid, hf, m_pad, _pick(n, 512), blk)
    ys = _experts(blk_exp, run_first, run_slot.astype(jnp.int32), run_next.astype(jnp.int32), n_valid, xs,
                  w_gu, b_gu, w_down, b_down, blk)
    rows = _gather_rows(ys, slot.reshape(-1))
    return _combine(rows, x2, gates.T, row(g_out), _pick(n, 256))


def kernel(x, mem, g_mix, w_in, w_gla_gate, b_gla_gate, g_gla_head, w_out, g_mem_q, g_mem_kv, w_mq, w_mk, w_mv,
           w_mo, g_ffn, w_router, b_router, w_gu, b_gu, w_down, b_down, g_final):
    batch, seq, d = x.shape
    depth = g_mix.shape[0]
    assert depth == 1, "the final rmsnorm is fused into the single layer's combine step"
    y = _layer(x.reshape(batch * seq, d), mem, batch, seq, g_mix[0], w_in[0], w_gla_gate[0], b_gla_gate[0],
               g_gla_head[0], w_out[0], g_mem_q[0], g_mem_kv[0], w_mq[0], w_mk[0], w_mv[0], w_mo[0], g_ffn[0],
               w_router[0], b_router[0], w_gu[0], b_gu[0], w_down[0], b_down[0], g_final)
    return y.reshape(batch, seq, d)
```

```python
import functools

import jax
import jax.numpy as jnp
import numpy as np
from jax import lax
from jax.experimental import pallas as pl
from jax.experimental.pallas import tpu as pltpu
from jax.experimental.pallas import tpu_sc as plsc

F32 = jnp.float32
BF16 = jnp.bfloat16

EPS = 1e-5
CHUNK = 64
GLA_HEADS = 4
GLA_DK = 64
GLA_DV = 128
GLA_GATE_RANK = 16
GLA_TAU = 16.0
SB_HEADS = 8
SB_HD = 64
MEM_HEADS = 4
N_EXPERTS = 32
TOP_K = 4
SWIGLU_LIMIT = 7.0
SWIGLU_ALPHA = 1.702
SB_LOG_UNDERFLOW = -104.0

LANES = 128
Q_G = GLA_HEADS * GLA_DK
V_G = GLA_HEADS * GLA_DV
SB_W = SB_HEADS * SB_HD
GLA_SLAB = Q_G + Q_G + V_G + V_G + LANES

VMEM_LIMIT = 56 * 1024 * 1024

NT_DIMS = (((1,), (1,)), ((), ()))
TN_DIMS = (((0,), (0,)), ((), ()))


def _dot(a, b):
    return jnp.dot(a, b, preferred_element_type=F32)


def _split(x):
    hi = x.astype(BF16)
    lo = (x - hi.astype(F32)).astype(BF16)
    return hi, lo


def _dot_exact_lhs(a_bf16, b_f32):
    hi, lo = _split(b_f32)
    return _dot(a_bf16, hi) + _dot(a_bf16, lo)


def _dot3(a, b):
    ah, al = _split(a)
    bh, bl = _split(b)
    return _dot(ah, bh) + (_dot(ah, bl) + _dot(al, bh))


def _rmsnorm(x, g):
    return x * lax.rsqrt(jnp.mean(x * x, axis=-1, keepdims=True) + EPS) * g


def _softplus(z):
    return jnp.maximum(z, 0.0) + jnp.log(1.0 + jnp.exp(-jnp.abs(z)))


def _params(sem, vmem=VMEM_LIMIT):
    return pltpu.CompilerParams(dimension_semantics=sem, vmem_limit_bytes=vmem)


def _inproj_kernel(x_ref, g_ref, wg_ref, wq_ref, wk_ref, wv_ref, slab_ref, q_ref, k_ref, v_ref):
    h = _rmsnorm(x_ref[...], g_ref[...]).astype(BF16)
    slab_ref[...] = _dot(h, wg_ref[...])
    q_ref[...] = (_dot(h, wq_ref[...]) * (SB_HD ** -0.5)).astype(BF16)
    k_ref[...] = _dot(h, wk_ref[...]).astype(BF16)
    v_ref[...] = _dot(h, wv_ref[...]).astype(BF16)


def _in_proj(x2d, g, wg, wq, wk, wv, tm):
    n, d = x2d.shape
    full = lambda a: pl.BlockSpec(a.shape, lambda i: (0, 0))
    return pl.pallas_call(
        _inproj_kernel,
        grid=(n // tm,),
        in_specs=[pl.BlockSpec((tm, d), lambda i: (i, 0)), full(g), full(wg), full(wq), full(wk), full(wv)],
        out_specs=[pl.BlockSpec((tm, GLA_SLAB), lambda i: (i, 0))] + [pl.BlockSpec((tm, SB_W), lambda i: (i, 0))] * 3,
        out_shape=[jax.ShapeDtypeStruct((n, GLA_SLAB), F32)] + [jax.ShapeDtypeStruct((n, SB_W), BF16)] * 3,
        compiler_params=_params(("parallel",)),
        name="in_proj",
    )(x2d, g, wg, wq, wk, wv)


def _gla_kernel(slab_ref, wgate_ref, bgate_ref, ghead_ref, umat_ref, o_ref, state_ref, *, n_chunks):
    @pl.when(pl.program_id(1) == 0)
    def _():
        state_ref[...] = jnp.zeros_like(state_ref)

    qg = slab_ref[:, 0:Q_G] * (GLA_DK ** -0.5)
    kg = slab_ref[:, Q_G:2 * Q_G]
    glr = slab_ref[:, 2 * Q_G + 2 * V_G:GLA_SLAB]
    log_a = -_softplus(-(_dot3(glr, wgate_ref[...]) + bgate_ref[...])) * (1.0 / GLA_TAU)
    to_end = _dot_exact_lhs(umat_ref[...], log_a)
    kdec = kg * jnp.exp(to_end)
    g_chunk = to_end + log_a

    lane = lax.broadcasted_iota(jnp.int32, (CHUNK, LANES), 1)
    ghead = ghead_ref[...]
    for c in range(n_chunks):
        rows = slice(c * CHUNK, (c + 1) * CHUNK)
        for h in range(GLA_HEADS):
            pair = slice((h // 2) * LANES, (h // 2 + 1) * LANES)
            mine = (lane >= GLA_DK) if h % 2 else (lane < GLA_DK)
            kd = jnp.where(mine, kdec[rows, pair], 0.0).astype(BF16)
            qm = jnp.where(mine, qg[rows, pair], 0.0).astype(BF16)
            vh = slab_ref[rows, 2 * Q_G + h * GLA_DV:2 * Q_G + (h + 1) * GLA_DV].astype(BF16)
            decay = jnp.exp(g_chunk[c * CHUNK:c * CHUNK + 1, pair])
            st = decay * state_ref[h] + lax.dot_general(vh, kd, TN_DIMS, preferred_element_type=F32)
            state_ref[h] = st
            o = lax.dot_general(qm, st.astype(BF16), NT_DIMS, preferred_element_type=F32)
            rg = slab_ref[rows, 2 * Q_G + V_G + h * GLA_DV:2 * Q_G + V_G + (h + 1) * GLA_DV]
            o = _rmsnorm(o, ghead) * (rg * jax.nn.sigmoid(rg))
            o_ref[rows, h * GLA_DV:(h + 1) * GLA_DV] = o.astype(BF16)


def _gla(slab, wgate, bgate, ghead, batch, seq, ts):
    n = slab.shape[0]
    n_chunks = ts // CHUNK
    r = np.arange(ts)
    umat = jnp.asarray((r[None, :] > r[:, None]) & (r[None, :] // CHUNK == r[:, None] // CHUNK), BF16)
    full = lambda a: pl.BlockSpec(a.shape, lambda b, i: (0,) * a.ndim)
    steps = seq // ts
    return pl.pallas_call(
        functools.partial(_gla_kernel, n_chunks=n_chunks),
        grid=(batch, steps),
        in_specs=[pl.BlockSpec((ts, GLA_SLAB), lambda b, i: (b * steps + i, 0)),
                  full(wgate), full(bgate), full(ghead), full(umat)],
        out_specs=pl.BlockSpec((ts, V_G), lambda b, i: (b * steps + i, 0)),
        out_shape=jax.ShapeDtypeStruct((n, V_G), BF16),
        scratch_shapes=[pltpu.VMEM((GLA_HEADS, GLA_DV, LANES), F32)],
        compiler_params=_params(("parallel", "arbitrary")),
        name="gla",
    )(slab, wgate, bgate, ghead, umat)


def _sb_kernel(q_ref, k_ref, v_ref, tmat_ref, o_ref, acc_ref, carry_ref, *, blk):
    i = pl.program_id(2)
    sub = blk // 2
    lane = lax.broadcasted_iota(jnp.int32, (blk, LANES), 1)
    causal = lax.broadcasted_iota(jnp.int32, (blk, blk), 1) < lax.broadcasted_iota(jnp.int32, (blk, blk), 0)
    q = q_ref[...]
    lo_head = lane < SB_HD
    q_heads = (jnp.where(lo_head, q, jnp.zeros_like(q)), jnp.where(lo_head, jnp.zeros_like(q), q))
    tmat = tmat_ref[...]

    def chunk(j, diag):
        start = pl.multiple_of(j * blk, blk)
        kj = k_ref[pl.ds(start, blk), :]
        vj = v_ref[pl.ds(start, blk), :]
        worst = None
        for h in range(2):
            z = lax.dot_general(q_heads[h], kj, NT_DIMS, preferred_element_type=F32)
            sp = _softplus(z)
            log1mb = -sp
            if diag:
                log1mb = jnp.where(causal, log1mb, 0.0)
            hi, lo = _split(log1mb)
            right = _dot(jnp.concatenate([hi[:, sub:], lo[:, sub:]], axis=1), tmat)
            left = _dot(jnp.concatenate([hi[:, :sub], lo[:, :sub]], axis=1), tmat)
            log_sig = z - sp
            if diag:
                after_right = right[:, sub:]
                log_a_right = log_sig[:, sub:] + right[:, :sub]
            else:
                carry = carry_ref[h]
                after_right = carry + right[:, sub:]
                log_a_right = log_sig[:, sub:] + right[:, :sub] + carry
            log_a_left = log_sig[:, :sub] + left[:, :sub] + after_right
            a = jnp.exp(jnp.concatenate([log_a_left, log_a_right], axis=1))
            if diag:
                a = jnp.where(causal, a, 0.0)
            pv = _dot(a.astype(BF16), vj)
            acc_ref[h] = pv if diag else acc_ref[h] + pv
            remaining = after_right + left[:, sub:]
            carry_ref[h] = remaining
            top = jnp.max(remaining)
            worst = top if worst is None else jnp.maximum(worst, top)
        return worst

    def more(state):
        return jnp.logical_and(state[0] >= 0, state[1] > SB_LOG_UNDERFLOW)

    def first_two():
        chunk(i, True)
        return chunk(i - 1, False)

    start = lax.cond(i >= 1, first_two, lambda: chunk(i, True))
    lax.while_loop(more, lambda state: (state[0] - 1, chunk(state[0], False)), (i - 2, start))
    o_ref[...] = jnp.where(lo_head, acc_ref[0], acc_ref[1]).astype(BF16)


def _sb(q, k, v, batch, seq, blk):
    sub = blk // 2
    assert sub == LANES
    r = np.arange(sub)
    tri = (r[:, None] > r[None, :]).astype(np.float32)
    half = np.concatenate([tri, np.ones((sub, sub), np.float32)], axis=1)
    tmat = jnp.asarray(np.concatenate([half, half], axis=0), BF16)
    nq = seq // blk
    pairs = SB_W // LANES
    q3, k3, v3 = (a.reshape(batch, seq, SB_W) for a in (q, k, v))
    out = pl.pallas_call(
        functools.partial(_sb_kernel, blk=blk),
        grid=(batch, pairs, nq),
        in_specs=[pl.BlockSpec((None, blk, LANES), lambda b, p, i: (b, i, p)),
                  pl.BlockSpec((None, seq, LANES), lambda b, p, i: (b, 0, p)),
                  pl.BlockSpec((None, seq, LANES), lambda b, p, i: (b, 0, p)),
                  pl.BlockSpec(tmat.shape, lambda b, p, i: (0, 0))],
        out_specs=pl.BlockSpec((None, blk, LANES), lambda b, p, i: (b, i, p)),
        out_shape=jax.ShapeDtypeStruct((batch, seq, SB_W), BF16),
        scratch_shapes=[pltpu.VMEM((2, blk, LANES), F32), pltpu.VMEM((2, blk, LANES), F32)],
        compiler_params=_params(("parallel", "parallel", "arbitrary")),
        name="sb",
    )(q3, k3, v3, tmat)
    return out.reshape(batch * seq, SB_W)


def _memkv_kernel(m_ref, g_ref, wk_ref, wv_ref, k_ref, v_ref):
    hm = _rmsnorm(m_ref[...], g_ref[...]).astype(BF16)
    k_ref[...] = _dot(hm, wk_ref[...]).astype(BF16)
    v_ref[...] = _dot(hm, wv_ref[...]).astype(BF16)


def _mem_kv(mem, g, wk, wv):
    b, m, d = mem.shape
    full = lambda a: pl.BlockSpec(a.shape, lambda i: (0, 0))
    blk = pl.BlockSpec((None, m, d), lambda i: (i, 0, 0))
    return pl.pallas_call(
        _memkv_kernel,
        grid=(b,),
        in_specs=[blk, full(g), full(wk), full(wv)],
        out_specs=[blk, blk],
        out_shape=[jax.ShapeDtypeStruct((b, m, d), BF16)] * 2,
        compiler_params=_params(("parallel",)),
        name="mem_kv",
    )(mem, g, wk, wv)


def _mid_kernel(x_ref, og_ref, os_ref, wog_ref, wos_ref, gq_ref, wmq_ref, km_ref, vm_ref, wmo_ref, gf_ref,
                x2_ref, hf_ref):
    x1 = x_ref[...] + _dot(og_ref[...], wog_ref[...]) + _dot(os_ref[...], wos_ref[...])
    hq = _rmsnorm(x1, gq_ref[...]).astype(BF16)
    d = x1.shape[-1]
    hd = d // MEM_HEADS
    q = (_dot(hq, wmq_ref[...]) * (hd ** -0.5)).astype(BF16)
    outs = []
    for h in range(MEM_HEADS):
        cols = slice(h * hd, (h + 1) * hd)
        s = lax.dot_general(q[:, cols], km_ref[:, cols], NT_DIMS, preferred_element_type=F32)
        e = jnp.exp(s - jnp.max(s, axis=-1, keepdims=True))
        p = (e / jnp.sum(e, axis=-1, keepdims=True)).astype(BF16)
        outs.append(_dot(p, vm_ref[:, cols]).astype(BF16))
    x2 = x1 + _dot(jnp.concatenate(outs, axis=1), wmo_ref[...])
    x2_ref[...] = x2
    hf_ref[...] = _rmsnorm(x2, gf_ref[...])


def _mid(x2d, og, osb, wog, wos, gq, wmq, km, vm, wmo, gf, seq, tm):
    n, d = x2d.shape
    m = km.shape[1]
    per_batch = seq // tm
    full = lambda a: pl.BlockSpec(a.shape, lambda i: (0, 0))
    rows = lambda w: pl.BlockSpec((tm, w), lambda i: (i, 0))
    mem = pl.BlockSpec((None, m, d), lambda i: (i // per_batch, 0, 0))
    return pl.pallas_call(
        _mid_kernel,
        grid=(n // tm,),
        in_specs=[rows(d), rows(V_G), rows(SB_W), full(wog), full(wos), full(gq), full(wmq), mem, mem,
                  full(wmo), full(gf)],
        out_specs=[rows(d), rows(d)],
        out_shape=[jax.ShapeDtypeStruct((n, d), F32)] * 2,
        compiler_params=_params(("parallel",)),
        name="mid",
    )(x2d, og, osb, wog, wos, gq, wmq, km, vm, wmo, gf)


def _router_kernel(hf_ref, wrt_ref, br_ref, cmat_ref, idx_ref, gate_ref, rank_ref, cnt_ref, carry_ref, *, tm):
    @pl.when(pl.program_id(0) == 0)
    def _():
        carry_ref[...] = jnp.zeros_like(carry_ref)

    hh, hl = _split(hf_ref[...])
    wh, wl = _split(wrt_ref[...])
    nt = lambda a, b: lax.dot_general(a, b, NT_DIMS, preferred_element_type=F32)
    vals = nt(wh, hh) + (nt(wh, hl) + nt(wl, hh)) + br_ref[...]
    eidx = lax.broadcasted_iota(jnp.int32, (N_EXPERTS, tm), 0)
    tops, sels, hots = [], [], []
    for _ in range(TOP_K):
        m = jnp.max(vals, axis=0, keepdims=True)
        sel = jnp.min(jnp.where(vals == m, eidx, N_EXPERTS), axis=0, keepdims=True)
        hot = eidx == sel
        vals = jnp.where(hot, -jnp.inf, vals)
        tops.append(m)
        sels.append(sel)
        hots.append(hot)
    exps = [jnp.exp(t - tops[0]) for t in tops]
    denom = exps[0] + exps[1] + exps[2] + exps[3]
    chosen = jnp.zeros((N_EXPERTS, tm), F32)
    for hot in hots:
        chosen = chosen + hot.astype(F32)
    sums = _dot(chosen.astype(BF16), cmat_ref[...])
    before = sums[:, :tm] + carry_ref[...]
    for k in range(TOP_K):
        idx_ref[k:k + 1, :] = sels[k]
        gate_ref[k:k + 1, :] = exps[k] / denom
        rank_ref[k:k + 1, :] = jnp.sum(jnp.where(hots[k], before, 0.0), axis=0, keepdims=True).astype(jnp.int32)
    carry_ref[...] = carry_ref[...] + sums[:, tm:]
    cnt_ref[...] = carry_ref[...]


def _router(hf, wrt, br, tm):
    n, d = hf.shape
    r = np.arange(tm)
    cmat = jnp.asarray(np.concatenate([(r[:, None] < r[None, :]).astype(np.float32),
                                       np.ones((tm, tm), np.float32)], axis=1), BF16)
    full = lambda a: pl.BlockSpec(a.shape, lambda i: (0, 0))
    tok = pl.BlockSpec((TOP_K, tm), lambda i: (0, i))
    return pl.pallas_call(
        functools.partial(_router_kernel, tm=tm),
        grid=(n // tm,),
        in_specs=[pl.BlockSpec((tm, d), lambda i: (i, 0)), full(wrt), full(br), full(cmat)],
        out_specs=[tok, tok, tok, pl.BlockSpec((N_EXPERTS, tm), lambda i: (0, 0))],
        out_shape=[jax.ShapeDtypeStruct((TOP_K, n), jnp.int32), jax.ShapeDtypeStruct((TOP_K, n), F32),
                   jax.ShapeDtypeStruct((TOP_K, n), jnp.int32), jax.ShapeDtypeStruct((N_EXPERTS, tm), F32)],
        scratch_shapes=[pltpu.VMEM((N_EXPERTS, tm), F32)],
        compiler_params=_params(("arbitrary",)),
        name="router",
    )(hf, wrt, br, cmat)


SC_CHUNK = 32


def _sc_workers():
    info = plsc.get_sparse_core_info()
    mesh = plsc.VectorSubcoreMesh(core_axis_name="c", subcore_axis_name="s")
    return info.num_cores, info.num_cores * info.num_subcores, mesh


def _scatter_rows(src, slot, n_out):
    n_cores, n_workers, mesh = _sc_workers()
    n, d = src.shape
    per_worker = n // n_workers
    n_chunks = per_worker // SC_CHUNK
    assert n == n_workers * n_chunks * SC_CHUNK and n_chunks % 2 == 0

    def body(src_hbm, idx_hbm, out_hbm, idx_v, rows_v, lsem, ssem):
        wid = lax.axis_index("s") * n_cores + lax.axis_index("c")
        base = wid * per_worker
        pltpu.sync_copy(idx_hbm.at[wid], idx_v)

        def load(j, b):
            return pltpu.make_async_copy(src_hbm.at[pl.ds(base + j * SC_CHUNK, SC_CHUNK)], rows_v.at[b], lsem.at[b])

        def scatters(j, b):
            return [pltpu.make_async_copy(rows_v.at[b], out_hbm.at[idx_v.at[k, j]], ssem.at[b]) for k in range(TOP_K)]

        load(0, 0).start()

        @pl.loop(0, n_chunks, step=2)
        def _(j0):
            for b in range(2):
                j = j0 + b

                @pl.when(j >= 1)
                def _():
                    for c in scatters(j - 1, 1 - b):
                        c.wait()

                @pl.when(j + 1 < n_chunks)
                def _():
                    load(j + 1, 1 - b).start()

                load(j, b).wait()
                for c in scatters(j, b):
                    c.start()

        for c in scatters(n_chunks - 1, 1):
            c.wait()

    call = pl.kernel(
        body,
        out_type=jax.ShapeDtypeStruct((n_out, d), src.dtype),
        mesh=mesh,
        scratch_types=[pltpu.VMEM((TOP_K, n_chunks, SC_CHUNK), jnp.int32), pltpu.VMEM((2, SC_CHUNK, d), src.dtype),
                       pltpu.SemaphoreType.DMA((2,)), pltpu.SemaphoreType.DMA((2,))],
    )
    idx = slot.reshape(TOP_K, n_workers, n_chunks, SC_CHUNK).transpose(1, 0, 2, 3)
    return call(src, idx)


def _fill_kernel(fill_start_ref, fill_n_ref, nvalid_ref, xs_in, xs_ref, zero_ref, sem, *, blk):
    del xs_in
    n_blocks = xs_ref.shape[0] // blk

    def block_copy(b):
        return pltpu.make_async_copy(zero_ref, xs_ref.at[pl.ds(pl.multiple_of(b * blk, blk), blk), :], sem)

    def row_copy(p):
        return pltpu.make_async_copy(zero_ref.at[pl.ds(0, 1), :], xs_ref.at[pl.ds(p, 1), :], sem)

    zero_ref[...] = jnp.zeros_like(zero_ref)
    for e in range(N_EXPERTS):
        def fill(r, _):
            row_copy(fill_start_ref[e] + r).start()
            return 0
        lax.fori_loop(0, fill_n_ref[e], fill, 0)
    for e in range(N_EXPERTS):
        def drain(r, _):
            row_copy(fill_start_ref[e] + r).wait()
            return 0
        lax.fori_loop(0, fill_n_ref[e], drain, 0)
    lax.fori_loop(nvalid_ref[0], n_blocks, lambda b, _: (block_copy(b).start(), 0)[1], 0)
    lax.fori_loop(nvalid_ref[0], n_blocks, lambda b, _: (block_copy(b).wait(), 0)[1], 0)


def _fill(fill_start, fill_n, n_valid, xs, blk):
    m_pad, d = xs.shape
    hbm = pl.BlockSpec(memory_space=pl.ANY)
    return pl.pallas_call(
        functools.partial(_fill_kernel, blk=blk),
        grid_spec=pltpu.PrefetchScalarGridSpec(
            num_scalar_prefetch=3,
            grid=(1,),
            in_specs=[hbm],
            out_specs=hbm,
            scratch_shapes=[pltpu.VMEM((blk, d), F32), pltpu.SemaphoreType.DMA],
        ),
        out_shape=jax.ShapeDtypeStruct((m_pad, d), F32),
        input_output_aliases={3: 0},
        compiler_params=_params(("arbitrary",)),
        name="fill",
    )(fill_start, fill_n, n_valid, xs)


def _experts_kernel(exp_ref, first_ref, slot_ref, next_ref, nvalid_ref, xs_ref, wgu_hbm, bgu_ref, wd_hbm, bd_ref,
                    ys_ref, wgu_f32, wd_f32, wgu_bf, wd_bf, wsem):
    i = pl.program_id(0)
    valid = i < nvalid_ref[0]
    slot = slot_ref[i]

    def weight_copies(e, s):
        return (pltpu.make_async_copy(wgu_hbm.at[e], wgu_f32.at[s], wsem.at[s]),
                pltpu.make_async_copy(wd_hbm.at[e], wd_f32.at[s], wsem.at[s]))

    @pl.when(jnp.logical_and(valid, first_ref[i] == 1))
    def _():
        @pl.when(i == 0)
        def _():
            for c in weight_copies(exp_ref[0], 0):
                c.start()

        for c in weight_copies(exp_ref[i], slot):
            c.wait()
        wgu_bf[...] = wgu_f32[slot].astype(BF16)
        wd_bf[...] = wd_f32[slot].astype(BF16)

        @pl.when(next_ref[i] >= 0)
        def _():
            for c in weight_copies(next_ref[i], 1 - slot):
                c.start()

    @pl.when(valid)
    def _():
        f = wd_bf.shape[0]
        gu = _dot(xs_ref[...].astype(BF16), wgu_bf[...]) + bgu_ref[...]
        gate = jnp.minimum(gu[:, :f], SWIGLU_LIMIT)
        lin = jnp.clip(gu[:, f:], -SWIGLU_LIMIT, SWIGLU_LIMIT)
        act = (lin + 1.0) * (gate * jax.nn.sigmoid(SWIGLU_ALPHA * gate))
        ys_ref[...] = _dot(act.astype(BF16), wd_bf[...]) + bd_ref[...]

    @pl.when(jnp.logical_not(valid))
    def _():
        ys_ref[...] = jnp.zeros_like(ys_ref)


def _experts(blk_exp, run_first, run_slot, run_next, n_valid, xs, w_gu, b_gu, w_down, b_down, blk):
    m_pad, d = xs.shape
    e, _, f2 = w_gu.shape
    f = f2 // 2
    n_steps = m_pad // blk
    rows = pl.BlockSpec((blk, d), lambda i, *_: (i, 0))
    return pl.pallas_call(
        _experts_kernel,
        grid_spec=pltpu.PrefetchScalarGridSpec(
            num_scalar_prefetch=5,
            grid=(n_steps,),
            in_specs=[rows,
                      pl.BlockSpec(memory_space=pl.ANY),
                      pl.BlockSpec((None, 1, f2), lambda i, x, *_: (x[i], 0, 0)),
                      pl.BlockSpec(memory_space=pl.ANY),
                      pl.BlockSpec((None, 1, d), lambda i, x, *_: (x[i], 0, 0))],
            out_specs=rows,
            scratch_shapes=[pltpu.VMEM((2, d, f2), F32), pltpu.VMEM((2, f, d), F32),
                            pltpu.VMEM((d, f2), BF16), pltpu.VMEM((f, d), BF16),
                            pltpu.SemaphoreType.DMA((2,))],
        ),
        out_shape=jax.ShapeDtypeStruct((m_pad, d), F32),
        compiler_params=_params(("arbitrary",)),
        name="experts",
    )(blk_exp, run_first, run_slot, run_next, n_valid, xs, w_gu, b_gu.reshape(e, 1, f2), w_down,
      b_down.reshape(e, 1, d))


def _gather_rows(table, idx):
    n_cores, n_workers, mesh = _sc_workers()
    n_rows, d = idx.shape[0], table.shape[1]
    per_worker = n_rows // n_workers
    n_chunks = per_worker // SC_CHUNK
    assert n_rows == n_workers * n_chunks * SC_CHUNK and n_chunks % 2 == 0

    def body(table_hbm, idx_hbm, out_hbm, idx_v, rows_v, gsem, wsem):
        wid = lax.axis_index("s") * n_cores + lax.axis_index("c")
        base = wid * per_worker
        pltpu.sync_copy(idx_hbm.at[wid], idx_v)

        def gather(j, b):
            return pltpu.make_async_copy(table_hbm.at[idx_v.at[j]], rows_v.at[b], gsem.at[b])

        def put(j, b):
            return pltpu.make_async_copy(rows_v.at[b], out_hbm.at[pl.ds(base + j * SC_CHUNK, SC_CHUNK)], wsem.at[b])

        gather(0, 0).start()

        @pl.loop(0, n_chunks, step=2)
        def _(j0):
            for b in range(2):
                j = j0 + b

                @pl.when(j >= 1)
                def _():
                    put(j - 1, 1 - b).wait()

                @pl.when(j + 1 < n_chunks)
                def _():
                    gather(j + 1, 1 - b).start()

                gather(j, b).wait()
                put(j, b).start()

        put(n_chunks - 1, 1).wait()

    call = pl.kernel(
        body,
        out_type=jax.ShapeDtypeStruct((n_rows, d), table.dtype),
        mesh=mesh,
        scratch_types=[pltpu.VMEM((n_chunks, SC_CHUNK), jnp.int32), pltpu.VMEM((2, SC_CHUNK, d), table.dtype),
                       pltpu.SemaphoreType.DMA((2,)), pltpu.SemaphoreType.DMA((2,))],
    )
    return call(table, idx.reshape(n_workers, n_chunks, SC_CHUNK))


def _combine_kernel(o0_ref, o1_ref, o2_ref, o3_ref, x2_ref, gate_ref, gfin_ref, y_ref):
    acc = x2_ref[...]
    for k, o_ref in enumerate((o0_ref, o1_ref, o2_ref, o3_ref)):
        acc = acc + gate_ref[:, k:k + 1] * o_ref[...]
    y_ref[...] = _rmsnorm(acc, gfin_ref[...])


def _combine(rows, x2, gates, gfin, tm):
    n, d = x2.shape
    per_k = n // tm
    planes = [pl.BlockSpec((tm, d), functools.partial(lambda i, k: (k * per_k + i, 0), k=k)) for k in range(TOP_K)]
    return pl.pallas_call(
        _combine_kernel,
        grid=(per_k,),
        in_specs=planes + [pl.BlockSpec((tm, d), lambda i: (i, 0)),
                           pl.BlockSpec((tm, TOP_K), lambda i: (i, 0)),
                           pl.BlockSpec(gfin.shape, lambda i: (0, 0))],
        out_specs=pl.BlockSpec((tm, d), lambda i: (i, 0)),
        out_shape=jax.ShapeDtypeStruct((n, d), F32),
        compiler_params=_params(("parallel",)),
        name="combine",
    )(rows, rows, rows, rows, x2, gates, gfin)


def _pick(n, pref):
    t = min(pref, n)
    while n % t:
        t //= 2
    return t


def _layer(x2d, mem, batch, seq, g_mix, w_in, w_gla_gate, b_gla_gate, g_gla_head, w_out, g_mem_q, g_mem_kv,
           w_mq, w_mk, w_mv, w_mo, g_ffn, w_router, b_router, w_gu, b_gu, w_down, b_down, g_out):
    n, d = x2d.shape
    row = lambda v: v.reshape(1, -1).astype(F32)

    o_qg, o_kg, o_vg, o_lr, o_rg, o_qs, o_ks, o_vs = np.cumsum((0, Q_G, Q_G, V_G, GLA_GATE_RANK, V_G, SB_W, SB_W))
    lr_pad = jnp.zeros((d, LANES - GLA_GATE_RANK), w_in.dtype)
    wg = jnp.concatenate([w_in[:, o_qg:o_vg + V_G], w_in[:, o_rg:o_rg + V_G],
                          w_in[:, o_lr:o_lr + GLA_GATE_RANK], lr_pad], axis=1).astype(BF16)
    wq, wk, wv = (w_in[:, o:o + SB_W].astype(BF16) for o in (o_qs, o_ks, o_vs))
    wgate = jnp.concatenate([w_gla_gate, jnp.zeros((LANES - GLA_GATE_RANK, Q_G), F32)], axis=0)

    slab, qs, ks, vs = _in_proj(x2d, row(g_mix), wg, wq, wk, wv, _pick(n, 512))
    o_gla = _gla(slab, wgate, row(b_gla_gate), row(g_gla_head), batch, seq, _pick(seq, 512))
    o_sb = _sb(qs, ks, vs, batch, seq, 2 * LANES)
    km, vm = _mem_kv(mem, row(g_mem_kv), w_mk.astype(BF16), w_mv.astype(BF16))
    w_out_bf = w_out.astype(BF16)
    x2, hf = _mid(x2d, o_gla, o_sb, w_out_bf[:V_G], w_out_bf[V_G:], row(g_mem_q), w_mq.astype(BF16), km, vm,
                  w_mo.astype(BF16), row(g_ffn), seq, _pick(seq, 512))

    tm_r = _pick(n, 256)
    idx, gates, rank, cnt = _router(hf, w_router.T, b_router.reshape(-1, 1), tm_r)

    blk = 512
    counts = cnt[:, 0].astype(jnp.int32)
    padded = ((counts + blk - 1) // blk) * blk
    pad_end = jnp.cumsum(padded)
    pad_start = pad_end - padded
    first = jnp.sum(jnp.where(idx[..., None] == jnp.arange(N_EXPERTS), pad_start, 0), axis=-1)
    slot = first + rank
    n_steps = -(-(n * TOP_K + N_EXPERTS * (blk - 1)) // blk)
    m_pad = n_steps * blk
    n_valid = pad_end[-1] // blk
    blk_idx = jnp.arange(n_steps, dtype=jnp.int32)
    used = jnp.minimum(blk_idx, n_valid - 1)
    blk_exp = jnp.minimum(jnp.sum(used[:, None] * blk >= pad_end[None, :], axis=1), N_EXPERTS - 1).astype(jnp.int32)
    run_first = jnp.logical_and(blk_idx < n_valid, blk_exp != jnp.concatenate([blk_exp[:1] - 1, blk_exp[:-1]]))
    run_first = run_first.astype(jnp.int32)
    run_slot = (jnp.cumsum(run_first) - 1) % 2
    e_ids = jnp.arange(N_EXPERTS, dtype=jnp.int32)
    later = jnp.where(jnp.logical_and(padded[None, :] > 0, e_ids[None, :] > e_ids[:, None]), e_ids[None, :], N_EXPERTS)
    following = jnp.min(later, axis=1)
    run_next = jnp.where(following == N_EXPERTS, -1, following)[blk_exp]
    n_valid = n_valid.reshape(1)

    xs = _fill(pad_start + counts, padded - counts, n_valid, _scatter_rows(hf, slot, m_pad), blk)
    ys = _experts(blk_exp, run_first, run_slot.astype(jnp.int32), run_next.astype(jnp.int32), n_valid, xs,
                  w_gu, b_gu, w_down, b_down, blk)
    rows = _gather_rows(ys, slot.reshape(-1))
    return _combine(rows, x2, gates.T, row(g_out), _pick(n, 256))


def kernel(x, mem, g_mix, w_in, w_gla_gate, b_gla_gate, g_gla_head, w_out, g_mem_q, g_mem_kv, w_mq, w_mk, w_mv,
           w_mo, g_ffn, w_router, b_router, w_gu, b_gu, w_down, b_down, g_final):
    batch, seq, d = x.shape
    depth = g_mix.shape[0]
    assert depth == 1, "the final rmsnorm is fused into the single layer's combine step"
    y = _layer(x.reshape(batch * seq, d), mem, batch, seq, g_mix[0], w_in[0], w_gla_gate[0], b_gla_gate[0],
               g_gla_head[0], w_out[0], g_mem_q[0], g_mem_kv[0], w_mq[0], w_mk[0], w_mv[0], w_mo[0], g_ffn[0],
               w_router[0], b_router[0], w_gu[0], b_gu[0], w_down[0], b_down[0], g_final)
    return y.reshape(batch, seq, d)
```

```python
import functools

import jax
import jax.numpy as jnp
import numpy as np
from jax import lax
from jax.experimental import pallas as pl
from jax.experimental.pallas import tpu as pltpu
from jax.experimental.pallas import tpu_sc as plsc

F32 = jnp.float32
BF16 = jnp.bfloat16

EPS = 1e-5
CHUNK = 64
GLA_HEADS = 4
GLA_DK = 64
GLA_DV = 128
GLA_GATE_RANK = 16
GLA_TAU = 16.0
SB_HEADS = 8
SB_HD = 64
MEM_HEADS = 4
N_EXPERTS = 32
TOP_K = 4
SWIGLU_LIMIT = 7.0
SWIGLU_ALPHA = 1.702
SB_LOG_UNDERFLOW = -104.0

LANES = 128
Q_G = GLA_HEADS * GLA_DK
V_G = GLA_HEADS * GLA_DV
SB_W = SB_HEADS * SB_HD
GLA_SLAB = Q_G + Q_G + V_G + V_G + LANES

VMEM_LIMIT = 56 * 1024 * 1024
TILE_ROWS = 8

NT_DIMS = (((1,), (1,)), ((), ()))
TN_DIMS = (((0,), (0,)), ((), ()))


def _dot(a, b):
    return jnp.dot(a, b, preferred_element_type=F32)


def _split(x):
    hi = x.astype(BF16)
    lo = (x - hi.astype(F32)).astype(BF16)
    return hi, lo


def _dot_exact_lhs(a_bf16, b_f32):
    hi, lo = _split(b_f32)
    return _dot(a_bf16, hi) + _dot(a_bf16, lo)


def _dot3(a, b):
    ah, al = _split(a)
    bh, bl = _split(b)
    return _dot(ah, bh) + (_dot(ah, bl) + _dot(al, bh))


def _rmsnorm(x, g):
    return x * lax.rsqrt(jnp.mean(x * x, axis=-1, keepdims=True) + EPS) * g


def _softplus(z):
    return jnp.maximum(z, 0.0) + jnp.log(1.0 + jnp.exp(-jnp.abs(z)))


def _params(sem, vmem=VMEM_LIMIT):
    return pltpu.CompilerParams(dimension_semantics=sem, vmem_limit_bytes=vmem)


def _inproj_kernel(x_ref, g_ref, wg_ref, wq_ref, wk_ref, wv_ref, slab_ref, q_ref, k_ref, v_ref):
    h = _rmsnorm(x_ref[...], g_ref[...]).astype(BF16)
    slab_ref[...] = _dot(h, wg_ref[...])
    q_ref[...] = (_dot(h, wq_ref[...]) * (SB_HD ** -0.5)).astype(BF16)
    k_ref[...] = _dot(h, wk_ref[...]).astype(BF16)
    v_ref[...] = _dot(h, wv_ref[...]).astype(BF16)


def _in_proj(x2d, g, wg, wq, wk, wv, tm):
    n, d = x2d.shape
    full = lambda a: pl.BlockSpec(a.shape, lambda i: (0, 0))
    return pl.pallas_call(
        _inproj_kernel,
        grid=(n // tm,),
        in_specs=[pl.BlockSpec((tm, d), lambda i: (i, 0)), full(g), full(wg), full(wq), full(wk), full(wv)],
        out_specs=[pl.BlockSpec((tm, GLA_SLAB), lambda i: (i, 0))] + [pl.BlockSpec((tm, SB_W), lambda i: (i, 0))] * 3,
        out_shape=[jax.ShapeDtypeStruct((n, GLA_SLAB), F32)] + [jax.ShapeDtypeStruct((n, SB_W), BF16)] * 3,
        compiler_params=_params(("parallel",)),
        name="in_proj",
    )(x2d, g, wg, wq, wk, wv)


def _gla_kernel(slab_ref, wgate_ref, bgate_ref, ghead_ref, umat_ref, o_ref, state_ref, *, n_chunks):
    @pl.when(pl.program_id(1) == 0)
    def _():
        state_ref[...] = jnp.zeros_like(state_ref)

    qg = slab_ref[:, 0:Q_G] * (GLA_DK ** -0.5)
    kg = slab_ref[:, Q_G:2 * Q_G]
    glr = slab_ref[:, 2 * Q_G + 2 * V_G:GLA_SLAB]
    log_a = -_softplus(-(_dot3(glr, wgate_ref[...]) + bgate_ref[...])) * (1.0 / GLA_TAU)
    to_end = _dot_exact_lhs(umat_ref[...], log_a)
    kdec = kg * jnp.exp(to_end)
    g_chunk = to_end + log_a

    lane = lax.broadcasted_iota(jnp.int32, (CHUNK, LANES), 1)
    ghead = ghead_ref[...]
    for c in range(n_chunks):
        rows = slice(c * CHUNK, (c + 1) * CHUNK)
        for h in range(GLA_HEADS):
            pair = slice((h // 2) * LANES, (h // 2 + 1) * LANES)
            mine = (lane >= GLA_DK) if h % 2 else (lane < GLA_DK)
            kd = jnp.where(mine, kdec[rows, pair], 0.0).astype(BF16)
            qm = jnp.where(mine, qg[rows, pair], 0.0).astype(BF16)
            vh = slab_ref[rows, 2 * Q_G + h * GLA_DV:2 * Q_G + (h + 1) * GLA_DV].astype(BF16)
            decay = jnp.exp(g_chunk[c * CHUNK:c * CHUNK + 1, pair])
            st = decay * state_ref[h] + lax.dot_general(vh, kd, TN_DIMS, preferred_element_type=F32)
            state_ref[h] = st
            o = lax.dot_general(qm, st.astype(BF16), NT_DIMS, preferred_element_type=F32)
            rg = slab_ref[rows, 2 * Q_G + V_G + h * GLA_DV:2 * Q_G + V_G + (h + 1) * GLA_DV]
            o = _rmsnorm(o, ghead) * (rg * jax.nn.sigmoid(rg))
            o_ref[rows, h * GLA_DV:(h + 1) * GLA_DV] = o.astype(BF16)


def _gla(slab, wgate, bgate, ghead, batch, seq, ts):
    n = slab.shape[0]
    n_chunks = ts // CHUNK
    r = np.arange(ts)
    umat = jnp.asarray((r[None, :] > r[:, None]) & (r[None, :] // CHUNK == r[:, None] // CHUNK), BF16)
    full = lambda a: pl.BlockSpec(a.shape, lambda b, i: (0,) * a.ndim)
    steps = seq // ts
    return pl.pallas_call(
        functools.partial(_gla_kernel, n_chunks=n_chunks),
        grid=(batch, steps),
        in_specs=[pl.BlockSpec((ts, GLA_SLAB), lambda b, i: (b * steps + i, 0)),
                  full(wgate), full(bgate), full(ghead), full(umat)],
        out_specs=pl.BlockSpec((ts, V_G), lambda b, i: (b * steps + i, 0)),
        out_shape=jax.ShapeDtypeStruct((n, V_G), BF16),
        scratch_shapes=[pltpu.VMEM((GLA_HEADS, GLA_DV, LANES), F32)],
        compiler_params=_params(("parallel", "arbitrary")),
        name="gla",
    )(slab, wgate, bgate, ghead, umat)


def _sb_kernel(q_ref, k_ref, v_ref, tmat_ref, o_ref, acc_ref, carry_ref, *, blk):
    i = pl.program_id(2)
    sub = blk // 2
    lane = lax.broadcasted_iota(jnp.int32, (blk, LANES), 1)
    causal = lax.broadcasted_iota(jnp.int32, (blk, blk), 1) < lax.broadcasted_iota(jnp.int32, (blk, blk), 0)
    q = q_ref[...]
    lo_head = lane < SB_HD
    q_heads = (jnp.where(lo_head, q, jnp.zeros_like(q)), jnp.where(lo_head, jnp.zeros_like(q), q))
    tmat = tmat_ref[...]

    def chunk(j, diag):
        start = pl.multiple_of(j * blk, blk)
        kj = k_ref[pl.ds(start, blk), :]
        vj = v_ref[pl.ds(start, blk), :]
        worst = None
        for h in range(2):
            z = lax.dot_general(q_heads[h], kj, NT_DIMS, preferred_element_type=F32)
            sp = _softplus(z)
            log1mb = -sp
            if diag:
                log1mb = jnp.where(causal, log1mb, 0.0)
            hi, lo = _split(log1mb)
            right = _dot(jnp.concatenate([hi[:, sub:], lo[:, sub:]], axis=1), tmat)
            left = _dot(jnp.concatenate([hi[:, :sub], lo[:, :sub]], axis=1), tmat)
            log_sig = z - sp
            if diag:
                after_right = right[:, sub:]
                log_a_right = log_sig[:, sub:] + right[:, :sub]
            else:
                carry = carry_ref[h]
                after_right = carry + right[:, sub:]
                log_a_right = log_sig[:, sub:] + right[:, :sub] + carry
            log_a_left = log_sig[:, :sub] + left[:, :sub] + after_right
            a = jnp.exp(jnp.concatenate([log_a_left, log_a_right], axis=1))
            if diag:
                a = jnp.where(causal, a, 0.0)
            pv = _dot(a.astype(BF16), vj)
            acc_ref[h] = pv if diag else acc_ref[h] + pv
            remaining = after_right + left[:, sub:]
            carry_ref[h] = remaining
            top = jnp.max(remaining)
            worst = top if worst is None else jnp.maximum(worst, top)
        return worst

    def more(state):
        return jnp.logical_and(state[0] >= 0, state[1] > SB_LOG_UNDERFLOW)

    def first_two():
        chunk(i, True)
        return chunk(i - 1, False)

    start = lax.cond(i >= 1, first_two, lambda: chunk(i, True))
    lax.while_loop(more, lambda state: (state[0] - 1, chunk(state[0], False)), (i - 2, start))
    o_ref[...] = jnp.where(lo_head, acc_ref[0], acc_ref[1]).astype(BF16)


def _sb(q, k, v, batch, seq, blk):
    sub = blk // 2
    assert sub == LANES
    r = np.arange(sub)
    tri = (r[:, None] > r[None, :]).astype(np.float32)
    half = np.concatenate([tri, np.ones((sub, sub), np.float32)], axis=1)
    tmat = jnp.asarray(np.concatenate([half, half], axis=0), BF16)
    nq = seq // blk
    pairs = SB_W // LANES
    q3, k3, v3 = (a.reshape(batch, seq, SB_W) for a in (q, k, v))
    out = pl.pallas_call(
        functools.partial(_sb_kernel, blk=blk),
        grid=(batch, pairs, nq),
        in_specs=[pl.BlockSpec((None, blk, LANES), lambda b, p, i: (b, i, p)),
                  pl.BlockSpec((None, seq, LANES), lambda b, p, i: (b, 0, p)),
                  pl.BlockSpec((None, seq, LANES), lambda b, p, i: (b, 0, p)),
                  pl.BlockSpec(tmat.shape, lambda b, p, i: (0, 0))],
        out_specs=pl.BlockSpec((None, blk, LANES), lambda b, p, i: (b, i, p)),
        out_shape=jax.ShapeDtypeStruct((batch, seq, SB_W), BF16),
        scratch_shapes=[pltpu.VMEM((2, blk, LANES), F32), pltpu.VMEM((2, blk, LANES), F32)],
        compiler_params=_params(("parallel", "parallel", "arbitrary")),
        name="sb",
    )(q3, k3, v3, tmat)
    return out.reshape(batch * seq, SB_W)


def _memkv_kernel(m_ref, g_ref, wk_ref, wv_ref, k_ref, v_ref):
    hm = _rmsnorm(m_ref[...], g_ref[...]).astype(BF16)
    k_ref[...] = _dot(hm, wk_ref[...]).astype(BF16)
    v_ref[...] = _dot(hm, wv_ref[...]).astype(BF16)


def _mem_kv(mem, g, wk, wv):
    b, m, d = mem.shape
    full = lambda a: pl.BlockSpec(a.shape, lambda i: (0, 0))
    blk = pl.BlockSpec((None, m, d), lambda i: (i, 0, 0))
    return pl.pallas_call(
        _memkv_kernel,
        grid=(b,),
        in_specs=[blk, full(g), full(wk), full(wv)],
        out_specs=[blk, blk],
        out_shape=[jax.ShapeDtypeStruct((b, m, d), BF16)] * 2,
        compiler_params=_params(("parallel",)),
        name="mem_kv",
    )(mem, g, wk, wv)


def _mid_kernel(x_ref, og_ref, os_ref, wog_ref, wos_ref, gq_ref, wmq_ref, km_ref, vm_ref, wmo_ref, gf_ref,
                x2_ref, hf_ref):
    x1 = x_ref[...] + _dot(og_ref[...], wog_ref[...]) + _dot(os_ref[...], wos_ref[...])
    hq = _rmsnorm(x1, gq_ref[...]).astype(BF16)
    d = x1.shape[-1]
    hd = d // MEM_HEADS
    q = (_dot(hq, wmq_ref[...]) * (hd ** -0.5)).astype(BF16)
    outs = []
    for h in range(MEM_HEADS):
        cols = slice(h * hd, (h + 1) * hd)
        s = lax.dot_general(q[:, cols], km_ref[:, cols], NT_DIMS, preferred_element_type=F32)
        e = jnp.exp(s - jnp.max(s, axis=-1, keepdims=True))
        p = (e / jnp.sum(e, axis=-1, keepdims=True)).astype(BF16)
        outs.append(_dot(p, vm_ref[:, cols]).astype(BF16))
    x2 = x1 + _dot(jnp.concatenate(outs, axis=1), wmo_ref[...])
    x2_ref[...] = x2
    hf_ref[...] = _rmsnorm(x2, gf_ref[...])


def _mid(x2d, og, osb, wog, wos, gq, wmq, km, vm, wmo, gf, seq, tm):
    n, d = x2d.shape
    m = km.shape[1]
    per_batch = seq // tm
    full = lambda a: pl.BlockSpec(a.shape, lambda i: (0, 0))
    rows = lambda w: pl.BlockSpec((tm, w), lambda i: (i, 0))
    mem = pl.BlockSpec((None, m, d), lambda i: (i // per_batch, 0, 0))
    return pl.pallas_call(
        _mid_kernel,
        grid=(n // tm,),
        in_specs=[rows(d), rows(V_G), rows(SB_W), full(wog), full(wos), full(gq), full(wmq), mem, mem,
                  full(wmo), full(gf)],
        out_specs=[rows(d), rows(d)],
        out_shape=[jax.ShapeDtypeStruct((n, d), F32)] * 2,
        compiler_params=_params(("parallel",)),
        name="mid",
    )(x2d, og, osb, wog, wos, gq, wmq, km, vm, wmo, gf)


def _router_kernel(hf_ref, wrt_ref, br_ref, cmat_ref, idx_ref, gate_ref, rank_ref, cnt_ref, carry_ref, *, tm):
    @pl.when(pl.program_id(0) == 0)
    def _():
        carry_ref[...] = jnp.zeros_like(carry_ref)

    hh, hl = _split(hf_ref[...])
    wh, wl = _split(wrt_ref[...])
    nt = lambda a, b: lax.dot_general(a, b, NT_DIMS, preferred_element_type=F32)
    vals = nt(wh, hh) + (nt(wh, hl) + nt(wl, hh)) + br_ref[...]
    eidx = lax.broadcasted_iota(jnp.int32, (N_EXPERTS, tm), 0)
    tops, sels, hots = [], [], []
    for _ in range(TOP_K):
        m = jnp.max(vals, axis=0, keepdims=True)
        sel = jnp.min(jnp.where(vals == m, eidx, N_EXPERTS), axis=0, keepdims=True)
        hot = eidx == sel
        vals = jnp.where(hot, -jnp.inf, vals)
        tops.append(m)
        sels.append(sel)
        hots.append(hot)
    exps = [jnp.exp(t - tops[0]) for t in tops]
    denom = exps[0] + exps[1] + exps[2] + exps[3]
    chosen = jnp.zeros((N_EXPERTS, tm), F32)
    for hot in hots:
        chosen = chosen + hot.astype(F32)
    sums = _dot(chosen.astype(BF16), cmat_ref[...])
    before = sums[:, :tm] + carry_ref[...]
    for k in range(TOP_K):
        idx_ref[k:k + 1, :] = sels[k]
        gate_ref[k:k + 1, :] = exps[k] / denom
        rank_ref[k:k + 1, :] = jnp.sum(jnp.where(hots[k], before, 0.0), axis=0, keepdims=True).astype(jnp.int32)
    carry_ref[...] = carry_ref[...] + sums[:, tm:]
    cnt_ref[...] = carry_ref[...]


def _router(hf, wrt, br, tm):
    n, d = hf.shape
    r = np.arange(tm)
    cmat = jnp.asarray(np.concatenate([(r[:, None] < r[None, :]).astype(np.float32),
                                       np.ones((tm, tm), np.float32)], axis=1), BF16)
    full = lambda a: pl.BlockSpec(a.shape, lambda i: (0, 0))
    tok = pl.BlockSpec((TOP_K, tm), lambda i: (0, i))
    return pl.pallas_call(
        functools.partial(_router_kernel, tm=tm),
        grid=(n // tm,),
        in_specs=[pl.BlockSpec((tm, d), lambda i: (i, 0)), full(wrt), full(br), full(cmat)],
        out_specs=[tok, tok, tok, pl.BlockSpec((N_EXPERTS, tm), lambda i: (0, 0))],
        out_shape=[jax.ShapeDtypeStruct((TOP_K, n), jnp.int32), jax.ShapeDtypeStruct((TOP_K, n), F32),
                   jax.ShapeDtypeStruct((TOP_K, n), jnp.int32), jax.ShapeDtypeStruct((N_EXPERTS, tm), F32)],
        scratch_shapes=[pltpu.VMEM((N_EXPERTS, tm), F32)],
        compiler_params=_params(("arbitrary",)),
        name="router",
    )(hf, wrt, br, cmat)


SC_CHUNK = 32


def _sc_workers():
    info = plsc.get_sparse_core_info()
    mesh = plsc.VectorSubcoreMesh(core_axis_name="c", subcore_axis_name="s")
    return info.num_cores, info.num_cores * info.num_subcores, mesh


def _scatter_rows(src, slot, n_out):
    n_cores, n_workers, mesh = _sc_workers()
    n, d = src.shape
    per_worker = n // n_workers
    n_chunks = per_worker // SC_CHUNK
    assert n == n_workers * n_chunks * SC_CHUNK and n_chunks % 2 == 0

    def body(src_hbm, idx_hbm, out_hbm, idx_v, rows_v, lsem, ssem):
        wid = lax.axis_index("s") * n_cores + lax.axis_index("c")
        base = wid * per_worker
        pltpu.sync_copy(idx_hbm.at[wid], idx_v)

        def load(j, b):
            return pltpu.make_async_copy(src_hbm.at[pl.ds(base + j * SC_CHUNK, SC_CHUNK)], rows_v.at[b], lsem.at[b])

        def scatters(j, b):
            return [pltpu.make_async_copy(rows_v.at[b], out_hbm.at[idx_v.at[k, j]], ssem.at[b]) for k in range(TOP_K)]

        load(0, 0).start()

        @pl.loop(0, n_chunks, step=2)
        def _(j0):
            for b in range(2):
                j = j0 + b

                @pl.when(j >= 1)
                def _():
                    for c in scatters(j - 1, 1 - b):
                        c.wait()

                @pl.when(j + 1 < n_chunks)
                def _():
                    load(j + 1, 1 - b).start()

                load(j, b).wait()
                for c in scatters(j, b):
                    c.start()

        for c in scatters(n_chunks - 1, 1):
            c.wait()

    call = pl.kernel(
        body,
        out_type=jax.ShapeDtypeStruct((n_out, d), src.dtype),
        mesh=mesh,
        scratch_types=[pltpu.VMEM((TOP_K, n_chunks, SC_CHUNK), jnp.int32), pltpu.VMEM((2, SC_CHUNK, d), src.dtype),
                       pltpu.SemaphoreType.DMA((2,)), pltpu.SemaphoreType.DMA((2,))],
    )
    idx = slot.reshape(TOP_K, n_workers, n_chunks, SC_CHUNK).transpose(1, 0, 2, 3)
    return call(src, idx)


def _fill_kernel(fill_start_ref, fill_n_ref, nvalid_ref, xs_in, xs_ref, zero_ref, sem, *, blk):
    del xs_in
    n_blocks = xs_ref.shape[0] // blk

    def block_copy(b):
        return pltpu.make_async_copy(zero_ref, xs_ref.at[pl.ds(pl.multiple_of(b * blk, blk), blk), :], sem)

    def pieces(e):
        start, n = fill_start_ref[e], fill_n_ref[e]
        head = (-start) & (TILE_ROWS - 1)
        for r in range(TILE_ROWS - 1):
            copy = pltpu.make_async_copy(zero_ref.at[pl.ds(0, 1), :], xs_ref.at[pl.ds(start + r, 1), :], sem)
            yield r < jnp.minimum(head, n), copy
        aligned, rest = start + head, n - head
        size = blk // 2
        while size >= TILE_ROWS:
            at = pl.multiple_of(aligned + (rest & ~(2 * size - 1)), TILE_ROWS)
            copy = pltpu.make_async_copy(zero_ref.at[pl.ds(0, size), :], xs_ref.at[pl.ds(at, size), :], sem)
            yield (rest & size) != 0, copy
            size //= 2

    zero_ref[...] = jnp.zeros_like(zero_ref)
    for e in range(N_EXPERTS):
        for wanted, copy in pieces(e):
            pl.when(wanted)(copy.start)
    for e in range(N_EXPERTS):
        for wanted, copy in pieces(e):
            pl.when(wanted)(copy.wait)
    lax.fori_loop(nvalid_ref[0], n_blocks, lambda b, _: (block_copy(b).start(), 0)[1], 0)
    lax.fori_loop(nvalid_ref[0], n_blocks, lambda b, _: (block_copy(b).wait(), 0)[1], 0)


def _fill(fill_start, fill_n, n_valid, xs, blk):
    m_pad, d = xs.shape
    hbm = pl.BlockSpec(memory_space=pl.ANY)
    return pl.pallas_call(
        functools.partial(_fill_kernel, blk=blk),
        grid_spec=pltpu.PrefetchScalarGridSpec(
            num_scalar_prefetch=3,
            grid=(1,),
            in_specs=[hbm],
            out_specs=hbm,
            scratch_shapes=[pltpu.VMEM((blk, d), F32), pltpu.SemaphoreType.DMA],
        ),
        out_shape=jax.ShapeDtypeStruct((m_pad, d), F32),
        input_output_aliases={3: 0},
        compiler_params=_params(("arbitrary",)),
        name="fill",
    )(fill_start, fill_n, n_valid, xs)


def _experts_kernel(exp_ref, first_ref, slot_ref, next_ref, nvalid_ref, xs_ref, wgu_hbm, bgu_ref, wd_hbm, bd_ref,
                    ys_ref, wgu_f32, wd_f32, wgu_bf, wd_bf, wsem):
    i = pl.program_id(0)
    valid = i < nvalid_ref[0]
    slot = slot_ref[i]

    def weight_copies(e, s):
        return (pltpu.make_async_copy(wgu_hbm.at[e], wgu_f32.at[s], wsem.at[s]),
                pltpu.make_async_copy(wd_hbm.at[e], wd_f32.at[s], wsem.at[s]))

    @pl.when(jnp.logical_and(valid, first_ref[i] == 1))
    def _():
        @pl.when(i == 0)
        def _():
            for c in weight_copies(exp_ref[0], 0):
                c.start()

        for c in weight_copies(exp_ref[i], slot):
            c.wait()
        wgu_bf[...] = wgu_f32[slot].astype(BF16)
        wd_bf[...] = wd_f32[slot].astype(BF16)

        @pl.when(next_ref[i] >= 0)
        def _():
            for c in weight_copies(next_ref[i], 1 - slot):
                c.start()

    @pl.when(valid)
    def _():
        f = wd_bf.shape[0]
        gu = _dot(xs_ref[...].astype(BF16), wgu_bf[...]) + bgu_ref[...]
        gate = jnp.minimum(gu[:, :f], SWIGLU_LIMIT)
        lin = jnp.clip(gu[:, f:], -SWIGLU_LIMIT, SWIGLU_LIMIT)
        act = (lin + 1.0) * (gate * jax.nn.sigmoid(SWIGLU_ALPHA * gate))
        ys_ref[...] = _dot(act.astype(BF16), wd_bf[...]) + bd_ref[...]

    @pl.when(jnp.logical_not(valid))
    def _():
        ys_ref[...] = jnp.zeros_like(ys_ref)


def _experts(blk_exp, run_first, run_slot, run_next, n_valid, xs, w_gu, b_gu, w_down, b_down, blk):
    m_pad, d = xs.shape
    e, _, f2 = w_gu.shape
    f = f2 // 2
    n_steps = m_pad // blk
    rows = pl.BlockSpec((blk, d), lambda i, *_: (i, 0))
    return pl.pallas_call(
        _experts_kernel,
        grid_spec=pltpu.PrefetchScalarGridSpec(
            num_scalar_prefetch=5,
            grid=(n_steps,),
            in_specs=[rows,
                      pl.BlockSpec(memory_space=pl.ANY),
                      pl.BlockSpec((None, 1, f2), lambda i, x, *_: (x[i], 0, 0)),
                      pl.BlockSpec(memory_space=pl.ANY),
                      pl.BlockSpec((None, 1, d), lambda i, x, *_: (x[i], 0, 0))],
            out_specs=rows,
            scratch_shapes=[pltpu.VMEM((2, d, f2), F32), pltpu.VMEM((2, f, d), F32),
                            pltpu.VMEM((d, f2), BF16), pltpu.VMEM((f, d), BF16),
                            pltpu.SemaphoreType.DMA((2,))],
        ),
        out_shape=jax.ShapeDtypeStruct((m_pad, d), F32),
        compiler_params=_params(("arbitrary",)),
        name="experts",
    )(blk_exp, run_first, run_slot, run_next, n_valid, xs, w_gu, b_gu.reshape(e, 1, f2), w_down,
      b_down.reshape(e, 1, d))


def _gather_rows(table, idx):
    n_cores, n_workers, mesh = _sc_workers()
    n_rows, d = idx.shape[0], table.shape[1]
    per_worker = n_rows // n_workers
    n_chunks = per_worker // SC_CHUNK
    assert n_rows == n_workers * n_chunks * SC_CHUNK and n_chunks % 2 == 0

    def body(table_hbm, idx_hbm, out_hbm, idx_v, rows_v, gsem, wsem):
        wid = lax.axis_index("s") * n_cores + lax.axis_index("c")
        base = wid * per_worker
        pltpu.sync_copy(idx_hbm.at[wid], idx_v)

        def gather(j, b):
            return pltpu.make_async_copy(table_hbm.at[idx_v.at[j]], rows_v.at[b], gsem.at[b])

        def put(j, b):
            return pltpu.make_async_copy(rows_v.at[b], out_hbm.at[pl.ds(base + j * SC_CHUNK, SC_CHUNK)], wsem.at[b])

        gather(0, 0).start()

        @pl.loop(0, n_chunks, step=2)
        def _(j0):
            for b in range(2):
                j = j0 + b

                @pl.when(j >= 1)
                def _():
                    put(j - 1, 1 - b).wait()

                @pl.when(j + 1 < n_chunks)
                def _():
                    gather(j + 1, 1 - b).start()

                gather(j, b).wait()
                put(j, b).start()

        put(n_chunks - 1, 1).wait()

    call = pl.kernel(
        body,
        out_type=jax.ShapeDtypeStruct((n_rows, d), table.dtype),
        mesh=mesh,
        scratch_types=[pltpu.VMEM((n_chunks, SC_CHUNK), jnp.int32), pltpu.VMEM((2, SC_CHUNK, d), table.dtype),
                       pltpu.SemaphoreType.DMA((2,)), pltpu.SemaphoreType.DMA((2,))],
    )
    return call(table, idx.reshape(n_workers, n_chunks, SC_CHUNK))


def _combine_kernel(o0_ref, o1_ref, o2_ref, o3_ref, x2_ref, gate_ref, gfin_ref, y_ref):
    acc = x2_ref[...]
    for k, o_ref in enumerate((o0_ref, o1_ref, o2_ref, o3_ref)):
        acc = acc + gate_ref[:, k:k + 1] * o_ref[...]
    y_ref[...] = _rmsnorm(acc, gfin_ref[...])


def _combine(rows, x2, gates, gfin, tm):
    n, d = x2.shape
    per_k = n // tm
    planes = [pl.BlockSpec((tm, d), functools.partial(lambda i, k: (k * per_k + i, 0), k=k)) for k in range(TOP_K)]
    return pl.pallas_call(
        _combine_kernel,
        grid=(per_k,),
        in_specs=planes + [pl.BlockSpec((tm, d), lambda i: (i, 0)),
                           pl.BlockSpec((tm, TOP_K), lambda i: (i, 0)),
                           pl.BlockSpec(gfin.shape, lambda i: (0, 0))],
        out_specs=pl.BlockSpec((tm, d), lambda i: (i, 0)),
        out_shape=jax.ShapeDtypeStruct((n, d), F32),
        compiler_params=_params(("parallel",)),
        name="combine",
    )(rows, rows, rows, rows, x2, gates, gfin)


def _pick(n, pref):
    t = min(pref, n)
    while n % t:
        t //= 2
    return t


def _layer(x2d, mem, batch, seq, g_mix, w_in, w_gla_gate, b_gla_gate, g_gla_head, w_out, g_mem_q, g_mem_kv,
           w_mq, w_mk, w_mv, w_mo, g_ffn, w_router, b_router, w_gu, b_gu, w_down, b_down, g_out):
    n, d = x2d.shape
    row = lambda v: v.reshape(1, -1).astype(F32)

    o_qg, o_kg, o_vg, o_lr, o_rg, o_qs, o_ks, o_vs = np.cumsum((0, Q_G, Q_G, V_G, GLA_GATE_RANK, V_G, SB_W, SB_W))
    lr_pad = jnp.zeros((d, LANES - GLA_GATE_RANK), w_in.dtype)
    wg = jnp.concatenate([w_in[:, o_qg:o_vg + V_G], w_in[:, o_rg:o_rg + V_G],
                          w_in[:, o_lr:o_lr + GLA_GATE_RANK], lr_pad], axis=1).astype(BF16)
    wq, wk, wv = (w_in[:, o:o + SB_W].astype(BF16) for o in (o_qs, o_ks, o_vs))
    wgate = jnp.concatenate([w_gla_gate, jnp.zeros((LANES - GLA_GATE_RANK, Q_G), F32)], axis=0)

    slab, qs, ks, vs = _in_proj(x2d, row(g_mix), wg, wq, wk, wv, _pick(n, 512))
    o_gla = _gla(slab, wgate, row(b_gla_gate), row(g_gla_head), batch, seq, _pick(seq, 512))
    o_sb = _sb(qs, ks, vs, batch, seq, 2 * LANES)
    km, vm = _mem_kv(mem, row(g_mem_kv), w_mk.astype(BF16), w_mv.astype(BF16))
    w_out_bf = w_out.astype(BF16)
    x2, hf = _mid(x2d, o_gla, o_sb, w_out_bf[:V_G], w_out_bf[V_G:], row(g_mem_q), w_mq.astype(BF16), km, vm,
                  w_mo.astype(BF16), row(g_ffn), seq, _pick(seq, 512))

    tm_r = _pick(n, 256)
    idx, gates, rank, cnt = _router(hf, w_router.T, b_router.reshape(-1, 1), tm_r)

    blk = 512
    counts = cnt[:, 0].astype(jnp.int32)
    padded = ((counts + blk - 1) // blk) * blk
    pad_end = jnp.cumsum(padded)
    pad_start = pad_end - padded
    first = jnp.sum(jnp.where(idx[..., None] == jnp.arange(N_EXPERTS), pad_start, 0), axis=-1)
    slot = first + rank
    n_steps = -(-(n * TOP_K + N_EXPERTS * (blk - 1)) // blk)
    m_pad = n_steps * blk
    n_valid = pad_end[-1] // blk
    blk_idx = jnp.arange(n_steps, dtype=jnp.int32)
    used = jnp.minimum(blk_idx, n_valid - 1)
    blk_exp = jnp.minimum(jnp.sum(used[:, None] * blk >= pad_end[None, :], axis=1), N_EXPERTS - 1).astype(jnp.int32)
    run_first = jnp.logical_and(blk_idx < n_valid, blk_exp != jnp.concatenate([blk_exp[:1] - 1, blk_exp[:-1]]))
    run_first = run_first.astype(jnp.int32)
    run_slot = (jnp.cumsum(run_first) - 1) % 2
    e_ids = jnp.arange(N_EXPERTS, dtype=jnp.int32)
    later = jnp.where(jnp.logical_and(padded[None, :] > 0, e_ids[None, :] > e_ids[:, None]), e_ids[None, :], N_EXPERTS)
    following = jnp.min(later, axis=1)
    run_next = jnp.where(following == N_EXPERTS, -1, following)[blk_exp]
    n_valid = n_valid.reshape(1)

    xs = _fill(pad_start + counts, padded - counts, n_valid, _scatter_rows(hf, slot, m_pad), blk)
    ys = _experts(blk_exp, run_first, run_slot.astype(jnp.int32), run_next.astype(jnp.int32), n_valid, xs,
                  w_gu, b_gu, w_down, b_down, blk)
    rows = _gather_rows(ys, slot.reshape(-1))
    return _combine(rows, x2, gates.T, row(g_out), _pick(n, 256))


def kernel(x, mem, g_mix, w_in, w_gla_gate, b_gla_gate, g_gla_head, w_out, g_mem_q, g_mem_kv, w_mq, w_mk, w_mv,
           w_mo, g_ffn, w_router, b_router, w_gu, b_gu, w_down, b_down, g_final):
    batch, seq, d = x.shape
    depth = g_mix.shape[0]
    assert depth == 1, "the final rmsnorm is fused into the single layer's combine step"
    y = _layer(x.reshape(batch * seq, d), mem, batch, seq, g_mix[0], w_in[0], w_gla_gate[0], b_gla_gate[0],
               g_gla_head[0], w_out[0], g_mem_q[0], g_mem_kv[0], w_mq[0], w_mk[0], w_mv[0], w_mo[0], g_ffn[0],
               w_router[0], b_router[0], w_gu[0], b_gu[0], w_down[0], b_down[0], g_final)
    return y.reshape(batch, seq, d)
```

```python
import functools

import jax
import jax.numpy as jnp
import numpy as np
from jax import lax
from jax.experimental import pallas as pl
from jax.experimental.pallas import tpu as pltpu
from jax.experimental.pallas import tpu_sc as plsc

F32 = jnp.float32
BF16 = jnp.bfloat16

EPS = 1e-5
CHUNK = 64
GLA_HEADS = 4
GLA_DK = 64
GLA_DV = 128
GLA_GATE_RANK = 16
GLA_TAU = 16.0
SB_HEADS = 8
SB_HD = 64
MEM_HEADS = 4
N_EXPERTS = 32
TOP_K = 4
SWIGLU_LIMIT = 7.0
SWIGLU_ALPHA = 1.702
SB_LOG_UNDERFLOW = -104.0
SB_HEADS_PER_STEP = 4

LANES = 128
Q_G = GLA_HEADS * GLA_DK
V_G = GLA_HEADS * GLA_DV
SB_W = SB_HEADS * SB_HD
GLA_SLAB = Q_G + Q_G + V_G + V_G + LANES

VMEM_LIMIT = 56 * 1024 * 1024
TILE_ROWS = 8

NT_DIMS = (((1,), (1,)), ((), ()))
TN_DIMS = (((0,), (0,)), ((), ()))


def _dot(a, b):
    return jnp.dot(a, b, preferred_element_type=F32)


def _split(x):
    hi = x.astype(BF16)
    lo = (x - hi.astype(F32)).astype(BF16)
    return hi, lo


def _dot_exact_lhs(a_bf16, b_f32):
    hi, lo = _split(b_f32)
    return _dot(a_bf16, hi) + _dot(a_bf16, lo)


def _dot3(a, b):
    ah, al = _split(a)
    bh, bl = _split(b)
    return _dot(ah, bh) + (_dot(ah, bl) + _dot(al, bh))


def _rmsnorm(x, g):
    return x * lax.rsqrt(jnp.mean(x * x, axis=-1, keepdims=True) + EPS) * g


def _softplus(z):
    return jnp.maximum(z, 0.0) + jnp.log(1.0 + jnp.exp(-jnp.abs(z)))


def _params(sem, vmem=VMEM_LIMIT):
    return pltpu.CompilerParams(dimension_semantics=sem, vmem_limit_bytes=vmem)


def _inproj_kernel(x_ref, g_ref, wg_ref, wq_ref, wk_ref, wv_ref, slab_ref, q_ref, k_ref, v_ref):
    h = _rmsnorm(x_ref[...], g_ref[...]).astype(BF16)
    slab_ref[...] = _dot(h, wg_ref[...])
    q_ref[...] = (_dot(h, wq_ref[...]) * (SB_HD ** -0.5)).astype(BF16)
    k_ref[...] = _dot(h, wk_ref[...]).astype(BF16)
    v_ref[...] = _dot(h, wv_ref[...]).astype(BF16)


def _in_proj(x2d, g, wg, wq, wk, wv, tm):
    n, d = x2d.shape
    full = lambda a: pl.BlockSpec(a.shape, lambda i: (0, 0))
    return pl.pallas_call(
        _inproj_kernel,
        grid=(n // tm,),
        in_specs=[pl.BlockSpec((tm, d), lambda i: (i, 0)), full(g), full(wg), full(wq), full(wk), full(wv)],
        out_specs=[pl.BlockSpec((tm, GLA_SLAB), lambda i: (i, 0))] + [pl.BlockSpec((tm, SB_W), lambda i: (i, 0))] * 3,
        out_shape=[jax.ShapeDtypeStruct((n, GLA_SLAB), F32)] + [jax.ShapeDtypeStruct((n, SB_W), BF16)] * 3,
        compiler_params=_params(("parallel",)),
        name="in_proj",
    )(x2d, g, wg, wq, wk, wv)


def _gla_kernel(slab_ref, wgate_ref, bgate_ref, ghead_ref, umat_ref, o_ref, state_ref, *, n_chunks):
    @pl.when(pl.program_id(1) == 0)
    def _():
        state_ref[...] = jnp.zeros_like(state_ref)

    qg = slab_ref[:, 0:Q_G] * (GLA_DK ** -0.5)
    kg = slab_ref[:, Q_G:2 * Q_G]
    glr = slab_ref[:, 2 * Q_G + 2 * V_G:GLA_SLAB]
    log_a = -_softplus(-(_dot3(glr, wgate_ref[...]) + bgate_ref[...])) * (1.0 / GLA_TAU)
    to_end = _dot_exact_lhs(umat_ref[...], log_a)
    kdec = kg * jnp.exp(to_end)
    g_chunk = to_end + log_a

    lane = lax.broadcasted_iota(jnp.int32, (CHUNK, LANES), 1)
    ghead = ghead_ref[...]
    for c in range(n_chunks):
        rows = slice(c * CHUNK, (c + 1) * CHUNK)
        for h in range(GLA_HEADS):
            pair = slice((h // 2) * LANES, (h // 2 + 1) * LANES)
            mine = (lane >= GLA_DK) if h % 2 else (lane < GLA_DK)
            kd = jnp.where(mine, kdec[rows, pair], 0.0).astype(BF16)
            qm = jnp.where(mine, qg[rows, pair], 0.0).astype(BF16)
            vh = slab_ref[rows, 2 * Q_G + h * GLA_DV:2 * Q_G + (h + 1) * GLA_DV].astype(BF16)
            decay = jnp.exp(g_chunk[c * CHUNK:c * CHUNK + 1, pair])
            st = decay * state_ref[h] + lax.dot_general(vh, kd, TN_DIMS, preferred_element_type=F32)
            state_ref[h] = st
            o = lax.dot_general(qm, st.astype(BF16), NT_DIMS, preferred_element_type=F32)
            rg = slab_ref[rows, 2 * Q_G + V_G + h * GLA_DV:2 * Q_G + V_G + (h + 1) * GLA_DV]
            o = _rmsnorm(o, ghead) * (rg * jax.nn.sigmoid(rg))
            o_ref[rows, h * GLA_DV:(h + 1) * GLA_DV] = o.astype(BF16)


def _gla(slab, wgate, bgate, ghead, batch, seq, ts):
    n = slab.shape[0]
    n_chunks = ts // CHUNK
    r = np.arange(ts)
    umat = jnp.asarray((r[None, :] > r[:, None]) & (r[None, :] // CHUNK == r[:, None] // CHUNK), BF16)
    full = lambda a: pl.BlockSpec(a.shape, lambda b, i: (0,) * a.ndim)
    steps = seq // ts
    return pl.pallas_call(
        functools.partial(_gla_kernel, n_chunks=n_chunks),
        grid=(batch, steps),
        in_specs=[pl.BlockSpec((ts, GLA_SLAB), lambda b, i: (b * steps + i, 0)),
                  full(wgate), full(bgate), full(ghead), full(umat)],
        out_specs=pl.BlockSpec((ts, V_G), lambda b, i: (b * steps + i, 0)),
        out_shape=jax.ShapeDtypeStruct((n, V_G), BF16),
        scratch_shapes=[pltpu.VMEM((GLA_HEADS, GLA_DV, LANES), F32)],
        compiler_params=_params(("parallel", "arbitrary")),
        name="gla",
    )(slab, wgate, bgate, ghead, umat)


def _sb_kernel(q_ref, k_ref, v_ref, tmat_ref, o_ref, acc_ref, carry_ref, *, blk):
    i = pl.program_id(2)
    sub = blk // 2
    lane = lax.broadcasted_iota(jnp.int32, (blk, LANES), 1)
    causal = lax.broadcasted_iota(jnp.int32, (blk, blk), 1) < lax.broadcasted_iota(jnp.int32, (blk, blk), 0)
    lo_head = lane < SB_HD
    n_heads = 2 * q_ref.shape[-1] // LANES
    q_heads = []
    for p in range(n_heads // 2):
        q = q_ref[:, p * LANES:(p + 1) * LANES]
        q_heads += [jnp.where(lo_head, q, jnp.zeros_like(q)), jnp.where(lo_head, jnp.zeros_like(q), q)]
    tmat = tmat_ref[...]

    def chunk(j, diag):
        start = pl.multiple_of(j * blk, blk)
        worst = None
        for h in range(n_heads):
            slab = slice((h // 2) * LANES, (h // 2 + 1) * LANES)
            kj = k_ref[pl.ds(start, blk), slab]
            vj = v_ref[pl.ds(start, blk), slab]
            z = lax.dot_general(q_heads[h], kj, NT_DIMS, preferred_element_type=F32)
            sp = _softplus(z)
            log1mb = -sp
            if diag:
                log1mb = jnp.where(causal, log1mb, 0.0)
            hi, lo = _split(log1mb)
            right = _dot(jnp.concatenate([hi[:, sub:], lo[:, sub:]], axis=1), tmat)
            left = _dot(jnp.concatenate([hi[:, :sub], lo[:, :sub]], axis=1), tmat)
            log_sig = z - sp
            if diag:
                after_right = right[:, sub:]
                log_a_right = log_sig[:, sub:] + right[:, :sub]
            else:
                carry = carry_ref[h]
                after_right = carry + right[:, sub:]
                log_a_right = log_sig[:, sub:] + right[:, :sub] + carry
            log_a_left = log_sig[:, :sub] + left[:, :sub] + after_right
            a = jnp.exp(jnp.concatenate([log_a_left, log_a_right], axis=1))
            if diag:
                a = jnp.where(causal, a, 0.0)
            pv = _dot(a.astype(BF16), vj)
            acc_ref[h] = pv if diag else acc_ref[h] + pv
            remaining = after_right + left[:, sub:]
            carry_ref[h] = remaining
            top = jnp.max(remaining)
            worst = top if worst is None else jnp.maximum(worst, top)
        return worst

    def more(state):
        return jnp.logical_and(state[0] >= 0, state[1] > SB_LOG_UNDERFLOW)

    def first_two():
        chunk(i, True)
        return chunk(i - 1, False)

    start = lax.cond(i >= 1, first_two, lambda: chunk(i, True))
    lax.while_loop(more, lambda state: (state[0] - 1, chunk(state[0], False)), (i - 2, start))
    for p in range(n_heads // 2):
        o_ref[:, p * LANES:(p + 1) * LANES] = jnp.where(lo_head, acc_ref[2 * p], acc_ref[2 * p + 1]).astype(BF16)


def _sb(q, k, v, batch, seq, blk):
    sub = blk // 2
    assert sub == LANES
    r = np.arange(sub)
    tri = (r[:, None] > r[None, :]).astype(np.float32)
    half = np.concatenate([tri, np.ones((sub, sub), np.float32)], axis=1)
    tmat = jnp.asarray(np.concatenate([half, half], axis=0), BF16)
    nq = seq // blk
    width = SB_HEADS_PER_STEP // 2 * LANES
    groups = SB_W // width
    q3, k3, v3 = (a.reshape(batch, seq, SB_W) for a in (q, k, v))
    out = pl.pallas_call(
        functools.partial(_sb_kernel, blk=blk),
        grid=(batch, groups, nq),
        in_specs=[pl.BlockSpec((None, blk, width), lambda b, p, i: (b, i, p)),
                  pl.BlockSpec((None, seq, width), lambda b, p, i: (b, 0, p)),
                  pl.BlockSpec((None, seq, width), lambda b, p, i: (b, 0, p)),
                  pl.BlockSpec(tmat.shape, lambda b, p, i: (0, 0))],
        out_specs=pl.BlockSpec((None, blk, width), lambda b, p, i: (b, i, p)),
        out_shape=jax.ShapeDtypeStruct((batch, seq, SB_W), BF16),
        scratch_shapes=[pltpu.VMEM((SB_HEADS_PER_STEP, blk, LANES), F32)] * 2,
        compiler_params=_params(("parallel", "parallel", "arbitrary")),
        name="sb",
    )(q3, k3, v3, tmat)
    return out.reshape(batch * seq, SB_W)


def _memkv_kernel(m_ref, g_ref, wk_ref, wv_ref, k_ref, v_ref):
    hm = _rmsnorm(m_ref[...], g_ref[...]).astype(BF16)
    k_ref[...] = _dot(hm, wk_ref[...]).astype(BF16)
    v_ref[...] = _dot(hm, wv_ref[...]).astype(BF16)


def _mem_kv(mem, g, wk, wv):
    b, m, d = mem.shape
    full = lambda a: pl.BlockSpec(a.shape, lambda i: (0, 0))
    blk = pl.BlockSpec((None, m, d), lambda i: (i, 0, 0))
    return pl.pallas_call(
        _memkv_kernel,
        grid=(b,),
        in_specs=[blk, full(g), full(wk), full(wv)],
        out_specs=[blk, blk],
        out_shape=[jax.ShapeDtypeStruct((b, m, d), BF16)] * 2,
        compiler_params=_params(("parallel",)),
        name="mem_kv",
    )(mem, g, wk, wv)


def _mid_kernel(x_ref, og_ref, os_ref, wog_ref, wos_ref, gq_ref, wmq_ref, km_ref, vm_ref, wmo_ref, gf_ref,
                x2_ref, hf_ref):
    x1 = x_ref[...] + _dot(og_ref[...], wog_ref[...]) + _dot(os_ref[...], wos_ref[...])
    hq = _rmsnorm(x1, gq_ref[...]).astype(BF16)
    d = x1.shape[-1]
    hd = d // MEM_HEADS
    q = (_dot(hq, wmq_ref[...]) * (hd ** -0.5)).astype(BF16)
    outs = []
    for h in range(MEM_HEADS):
        cols = slice(h * hd, (h + 1) * hd)
        s = lax.dot_general(q[:, cols], km_ref[:, cols], NT_DIMS, preferred_element_type=F32)
        e = jnp.exp(s - jnp.max(s, axis=-1, keepdims=True))
        p = (e / jnp.sum(e, axis=-1, keepdims=True)).astype(BF16)
        outs.append(_dot(p, vm_ref[:, cols]).astype(BF16))
    x2 = x1 + _dot(jnp.concatenate(outs, axis=1), wmo_ref[...])
    x2_ref[...] = x2
    hf_ref[...] = _rmsnorm(x2, gf_ref[...])


def _mid(x2d, og, osb, wog, wos, gq, wmq, km, vm, wmo, gf, seq, tm):
    n, d = x2d.shape
    m = km.shape[1]
    per_batch = seq // tm
    full = lambda a: pl.BlockSpec(a.shape, lambda i: (0, 0))
    rows = lambda w: pl.BlockSpec((tm, w), lambda i: (i, 0))
    mem = pl.BlockSpec((None, m, d), lambda i: (i // per_batch, 0, 0))
    return pl.pallas_call(
        _mid_kernel,
        grid=(n // tm,),
        in_specs=[rows(d), rows(V_G), rows(SB_W), full(wog), full(wos), full(gq), full(wmq), mem, mem,
                  full(wmo), full(gf)],
        out_specs=[rows(d), rows(d)],
        out_shape=[jax.ShapeDtypeStruct((n, d), F32)] * 2,
        compiler_params=_params(("parallel",)),
        name="mid",
    )(x2d, og, osb, wog, wos, gq, wmq, km, vm, wmo, gf)


def _router_kernel(hf_ref, wrt_ref, br_ref, cmat_ref, idx_ref, gate_ref, rank_ref, cnt_ref, carry_ref, *, tm):
    @pl.when(pl.program_id(0) == 0)
    def _():
        carry_ref[...] = jnp.zeros_like(carry_ref)

    hh, hl = _split(hf_ref[...])
    wh, wl = _split(wrt_ref[...])
    nt = lambda a, b: lax.dot_general(a, b, NT_DIMS, preferred_element_type=F32)
    vals = nt(wh, hh) + (nt(wh, hl) + nt(wl, hh)) + br_ref[...]
    eidx = lax.broadcasted_iota(jnp.int32, (N_EXPERTS, tm), 0)
    tops, sels, hots = [], [], []
    for _ in range(TOP_K):
        m = jnp.max(vals, axis=0, keepdims=True)
        sel = jnp.min(jnp.where(vals == m, eidx, N_EXPERTS), axis=0, keepdims=True)
        hot = eidx == sel
        vals = jnp.where(hot, -jnp.inf, vals)
        tops.append(m)
        sels.append(sel)
        hots.append(hot)
    exps = [jnp.exp(t - tops[0]) for t in tops]
    denom = exps[0] + exps[1] + exps[2] + exps[3]
    chosen = jnp.zeros((N_EXPERTS, tm), F32)
    for hot in hots:
        chosen = chosen + hot.astype(F32)
    sums = _dot(chosen.astype(BF16), cmat_ref[...])
    before = sums[:, :tm] + carry_ref[...]
    for k in range(TOP_K):
        idx_ref[k:k + 1, :] = sels[k]
        gate_ref[k:k + 1, :] = exps[k] / denom
        rank_ref[k:k + 1, :] = jnp.sum(jnp.where(hots[k], before, 0.0), axis=0, keepdims=True).astype(jnp.int32)
    carry_ref[...] = carry_ref[...] + sums[:, tm:]
    cnt_ref[...] = carry_ref[...]


def _router(hf, wrt, br, tm):
    n, d = hf.shape
    r = np.arange(tm)
    cmat = jnp.asarray(np.concatenate([(r[:, None] < r[None, :]).astype(np.float32),
                                       np.ones((tm, tm), np.float32)], axis=1), BF16)
    full = lambda a: pl.BlockSpec(a.shape, lambda i: (0, 0))
    tok = pl.BlockSpec((TOP_K, tm), lambda i: (0, i))
    return pl.pallas_call(
        functools.partial(_router_kernel, tm=tm),
        grid=(n // tm,),
        in_specs=[pl.BlockSpec((tm, d), lambda i: (i, 0)), full(wrt), full(br), full(cmat)],
        out_specs=[tok, tok, tok, pl.BlockSpec((N_EXPERTS, tm), lambda i: (0, 0))],
        out_shape=[jax.ShapeDtypeStruct((TOP_K, n), jnp.int32), jax.ShapeDtypeStruct((TOP_K, n), F32),
                   jax.ShapeDtypeStruct((TOP_K, n), jnp.int32), jax.ShapeDtypeStruct((N_EXPERTS, tm), F32)],
        scratch_shapes=[pltpu.VMEM((N_EXPERTS, tm), F32)],
        compiler_params=_params(("arbitrary",)),
        name="router",
    )(hf, wrt, br, cmat)


SC_CHUNK = 32


def _sc_workers():
    info = plsc.get_sparse_core_info()
    mesh = plsc.VectorSubcoreMesh(core_axis_name="c", subcore_axis_name="s")
    return info.num_cores, info.num_cores * info.num_subcores, mesh


def _scatter_rows(src, slot, n_out):
    n_cores, n_workers, mesh = _sc_workers()
    n, d = src.shape
    per_worker = n // n_workers
    n_chunks = per_worker // SC_CHUNK
    assert n == n_workers * n_chunks * SC_CHUNK and n_chunks % 2 == 0

    def body(src_hbm, idx_hbm, out_hbm, idx_v, rows_v, lsem, ssem):
        wid = lax.axis_index("s") * n_cores + lax.axis_index("c")
        base = wid * per_worker
        pltpu.sync_copy(idx_hbm.at[wid], idx_v)

        def load(j, b):
            return pltpu.make_async_copy(src_hbm.at[pl.ds(base + j * SC_CHUNK, SC_CHUNK)], rows_v.at[b], lsem.at[b])

        def scatters(j, b):
            return [pltpu.make_async_copy(rows_v.at[b], out_hbm.at[idx_v.at[k, j]], ssem.at[b]) for k in range(TOP_K)]

        load(0, 0).start()

        @pl.loop(0, n_chunks, step=2)
        def _(j0):
            for b in range(2):
                j = j0 + b

                @pl.when(j >= 1)
                def _():
                    for c in scatters(j - 1, 1 - b):
                        c.wait()

                @pl.when(j + 1 < n_chunks)
                def _():
                    load(j + 1, 1 - b).start()

                load(j, b).wait()
                for c in scatters(j, b):
                    c.start()

        for c in scatters(n_chunks - 1, 1):
            c.wait()

    call = pl.kernel(
        body,
        out_type=jax.ShapeDtypeStruct((n_out, d), src.dtype),
        mesh=mesh,
        scratch_types=[pltpu.VMEM((TOP_K, n_chunks, SC_CHUNK), jnp.int32), pltpu.VMEM((2, SC_CHUNK, d), src.dtype),
                       pltpu.SemaphoreType.DMA((2,)), pltpu.SemaphoreType.DMA((2,))],
    )
    idx = slot.reshape(TOP_K, n_workers, n_chunks, SC_CHUNK).transpose(1, 0, 2, 3)
    return call(src, idx)


def _fill_kernel(fill_start_ref, fill_n_ref, nvalid_ref, xs_in, xs_ref, zero_ref, sem, *, blk):
    del xs_in
    n_blocks = xs_ref.shape[0] // blk

    def block_copy(b):
        return pltpu.make_async_copy(zero_ref, xs_ref.at[pl.ds(pl.multiple_of(b * blk, blk), blk), :], sem)

    def pieces(e):
        start, n = fill_start_ref[e], fill_n_ref[e]
        head = (-start) & (TILE_ROWS - 1)
        for r in range(TILE_ROWS - 1):
            copy = pltpu.make_async_copy(zero_ref.at[pl.ds(0, 1), :], xs_ref.at[pl.ds(start + r, 1), :], sem)
            yield r < jnp.minimum(head, n), copy
        aligned, rest = start + head, n - head
        size = blk // 2
        while size >= TILE_ROWS:
            at = pl.multiple_of(aligned + (rest & ~(2 * size - 1)), TILE_ROWS)
            copy = pltpu.make_async_copy(zero_ref.at[pl.ds(0, size), :], xs_ref.at[pl.ds(at, size), :], sem)
            yield (rest & size) != 0, copy
            size //= 2

    zero_ref[...] = jnp.zeros_like(zero_ref)
    for e in range(N_EXPERTS):
        for wanted, copy in pieces(e):
            pl.when(wanted)(copy.start)
    for e in range(N_EXPERTS):
        for wanted, copy in pieces(e):
            pl.when(wanted)(copy.wait)
    lax.fori_loop(nvalid_ref[0], n_blocks, lambda b, _: (block_copy(b).start(), 0)[1], 0)
    lax.fori_loop(nvalid_ref[0], n_blocks, lambda b, _: (block_copy(b).wait(), 0)[1], 0)


def _fill(fill_start, fill_n, n_valid, xs, blk):
    m_pad, d = xs.shape
    hbm = pl.BlockSpec(memory_space=pl.ANY)
    return pl.pallas_call(
        functools.partial(_fill_kernel, blk=blk),
        grid_spec=pltpu.PrefetchScalarGridSpec(
            num_scalar_prefetch=3,
            grid=(1,),
            in_specs=[hbm],
            out_specs=hbm,
            scratch_shapes=[pltpu.VMEM((blk, d), F32), pltpu.SemaphoreType.DMA],
        ),
        out_shape=jax.ShapeDtypeStruct((m_pad, d), F32),
        input_output_aliases={3: 0},
        compiler_params=_params(("arbitrary",)),
        name="fill",
    )(fill_start, fill_n, n_valid, xs)


def _experts_kernel(exp_ref, first_ref, slot_ref, next_ref, nvalid_ref, xs_ref, wgu_hbm, bgu_ref, wd_hbm, bd_ref,
                    ys_ref, wgu_f32, wd_f32, wgu_bf, wd_bf, wsem):
    i = pl.program_id(0)
    valid = i < nvalid_ref[0]
    slot = slot_ref[i]

    def weight_copies(e, s):
        return (pltpu.make_async_copy(wgu_hbm.at[e], wgu_f32.at[s], wsem.at[s]),
                pltpu.make_async_copy(wd_hbm.at[e], wd_f32.at[s], wsem.at[s]))

    @pl.when(jnp.logical_and(valid, first_ref[i] == 1))
    def _():
        @pl.when(i == 0)
        def _():
            for c in weight_copies(exp_ref[0], 0):
                c.start()

        for c in weight_copies(exp_ref[i], slot):
            c.wait()
        wgu_bf[...] = wgu_f32[slot].astype(BF16)
        wd_bf[...] = wd_f32[slot].astype(BF16)

        @pl.when(next_ref[i] >= 0)
        def _():
            for c in weight_copies(next_ref[i], 1 - slot):
                c.start()

    @pl.when(valid)
    def _():
        f = wd_bf.shape[0]
        gu = _dot(xs_ref[...].astype(BF16), wgu_bf[...]) + bgu_ref[...]
        gate = jnp.minimum(gu[:, :f], SWIGLU_LIMIT)
        lin = jnp.clip(gu[:, f:], -SWIGLU_LIMIT, SWIGLU_LIMIT)
        act = (lin + 1.0) * (gate * jax.nn.sigmoid(SWIGLU_ALPHA * gate))
        ys_ref[...] = _dot(act.astype(BF16), wd_bf[...]) + bd_ref[...]

    @pl.when(jnp.logical_not(valid))
    def _():
        ys_ref[...] = jnp.zeros_like(ys_ref)


def _experts(blk_exp, run_first, run_slot, run_next, n_valid, xs, w_gu, b_gu, w_down, b_down, blk):
    m_pad, d = xs.shape
    e, _, f2 = w_gu.shape
    f = f2 // 2
    n_steps = m_pad // blk
    rows = pl.BlockSpec((blk, d), lambda i, *_: (i, 0))
    return pl.pallas_call(
        _experts_kernel,
        grid_spec=pltpu.PrefetchScalarGridSpec(
            num_scalar_prefetch=5,
            grid=(n_steps,),
            in_specs=[rows,
                      pl.BlockSpec(memory_space=pl.ANY),
                      pl.BlockSpec((None, 1, f2), lambda i, x, *_: (x[i], 0, 0)),
                      pl.BlockSpec(memory_space=pl.ANY),
                      pl.BlockSpec((None, 1, d), lambda i, x, *_: (x[i], 0, 0))],
            out_specs=rows,
            scratch_shapes=[pltpu.VMEM((2, d, f2), F32), pltpu.VMEM((2, f, d), F32),
                            pltpu.VMEM((d, f2), BF16), pltpu.VMEM((f, d), BF16),
                            pltpu.SemaphoreType.DMA((2,))],
        ),
        out_shape=jax.ShapeDtypeStruct((m_pad, d), F32),
        compiler_params=_params(("arbitrary",)),
        name="experts",
    )(blk_exp, run_first, run_slot, run_next, n_valid, xs, w_gu, b_gu.reshape(e, 1, f2), w_down,
      b_down.reshape(e, 1, d))


def _gather_rows(table, idx):
    n_cores, n_workers, mesh = _sc_workers()
    n_rows, d = idx.shape[0], table.shape[1]
    per_worker = n_rows // n_workers
    n_chunks = per_worker // SC_CHUNK
    assert n_rows == n_workers * n_chunks * SC_CHUNK and n_chunks % 2 == 0

    def body(table_hbm, idx_hbm, out_hbm, idx_v, rows_v, gsem, wsem):
        wid = lax.axis_index("s") * n_cores + lax.axis_index("c")
        base = wid * per_worker
        pltpu.sync_copy(idx_hbm.at[wid], idx_v)

        def gather(j, b):
            return pltpu.make_async_copy(table_hbm.at[idx_v.at[j]], rows_v.at[b], gsem.at[b])

        def put(j, b):
            return pltpu.make_async_copy(rows_v.at[b], out_hbm.at[pl.ds(base + j * SC_CHUNK, SC_CHUNK)], wsem.at[b])

        gather(0, 0).start()

        @pl.loop(0, n_chunks, step=2)
        def _(j0):
            for b in range(2):
                j = j0 + b

                @pl.when(j >= 1)
                def _():
                    put(j - 1, 1 - b).wait()

                @pl.when(j + 1 < n_chunks)
                def _():
                    gather(j + 1, 1 - b).start()

                gather(j, b).wait()
                put(j, b).start()

        put(n_chunks - 1, 1).wait()

    call = pl.kernel(
        body,
        out_type=jax.ShapeDtypeStruct((n_rows, d), table.dtype),
        mesh=mesh,
        scratch_types=[pltpu.VMEM((n_chunks, SC_CHUNK), jnp.int32), pltpu.VMEM((2, SC_CHUNK, d), table.dtype),
                       pltpu.SemaphoreType.DMA((2,)), pltpu.SemaphoreType.DMA((2,))],
    )
    return call(table, idx.reshape(n_workers, n_chunks, SC_CHUNK))


def _combine_kernel(o0_ref, o1_ref, o2_ref, o3_ref, x2_ref, gate_ref, gfin_ref, y_ref):
    acc = x2_ref[...]
    for k, o_ref in enumerate((o0_ref, o1_ref, o2_ref, o3_ref)):
        acc = acc + gate_ref[:, k:k + 1] * o_ref[...]
    y_ref[...] = _rmsnorm(acc, gfin_ref[...])


def _combine(rows, x2, gates, gfin, tm):
    n, d = x2.shape
    per_k = n // tm
    planes = [pl.BlockSpec((tm, d), functools.partial(lambda i, k: (k * per_k + i, 0), k=k)) for k in range(TOP_K)]
    return pl.pallas_call(
        _combine_kernel,
        grid=(per_k,),
        in_specs=planes + [pl.BlockSpec((tm, d), lambda i: (i, 0)),
                           pl.BlockSpec((tm, TOP_K), lambda i: (i, 0)),
                           pl.BlockSpec(gfin.shape, lambda i: (0, 0))],
        out_specs=pl.BlockSpec((tm, d), lambda i: (i, 0)),
        out_shape=jax.ShapeDtypeStruct((n, d), F32),
        compiler_params=_params(("parallel",)),
        name="combine",
    )(rows, rows, rows, rows, x2, gates, gfin)


def _pick(n, pref):
    t = min(pref, n)
    while n % t:
        t //= 2
    return t


def _layer(x2d, mem, batch, seq, g_mix, w_in, w_gla_gate, b_gla_gate, g_gla_head, w_out, g_mem_q, g_mem_kv,
           w_mq, w_mk, w_mv, w_mo, g_ffn, w_router, b_router, w_gu, b_gu, w_down, b_down, g_out):
    n, d = x2d.shape
    row = lambda v: v.reshape(1, -1).astype(F32)

    o_qg, o_kg, o_vg, o_lr, o_rg, o_qs, o_ks, o_vs = np.cumsum((0, Q_G, Q_G, V_G, GLA_GATE_RANK, V_G, SB_W, SB_W))
    lr_pad = jnp.zeros((d, LANES - GLA_GATE_RANK), w_in.dtype)
    wg = jnp.concatenate([w_in[:, o_qg:o_vg + V_G], w_in[:, o_rg:o_rg + V_G],
                          w_in[:, o_lr:o_lr + GLA_GATE_RANK], lr_pad], axis=1).astype(BF16)
    wq, wk, wv = (w_in[:, o:o + SB_W].astype(BF16) for o in (o_qs, o_ks, o_vs))
    wgate = jnp.concatenate([w_gla_gate, jnp.zeros((LANES - GLA_GATE_RANK, Q_G), F32)], axis=0)

    slab, qs, ks, vs = _in_proj(x2d, row(g_mix), wg, wq, wk, wv, _pick(n, 512))
    o_gla = _gla(slab, wgate, row(b_gla_gate), row(g_gla_head), batch, seq, _pick(seq, 512))
    o_sb = _sb(qs, ks, vs, batch, seq, 2 * LANES)
    km, vm = _mem_kv(mem, row(g_mem_kv), w_mk.astype(BF16), w_mv.astype(BF16))
    w_out_bf = w_out.astype(BF16)
    x2, hf = _mid(x2d, o_gla, o_sb, w_out_bf[:V_G], w_out_bf[V_G:], row(g_mem_q), w_mq.astype(BF16), km, vm,
                  w_mo.astype(BF16), row(g_ffn), seq, _pick(seq, 512))

    tm_r = _pick(n, 256)
    idx, gates, rank, cnt = _router(hf, w_router.T, b_router.reshape(-1, 1), tm_r)

    blk = 512
    counts = cnt[:, 0].astype(jnp.int32)
    padded = ((counts + blk - 1) // blk) * blk
    pad_end = jnp.cumsum(padded)
    pad_start = pad_end - padded
    first = jnp.sum(jnp.where(idx[..., None] == jnp.arange(N_EXPERTS), pad_start, 0), axis=-1)
    slot = first + rank
    n_steps = -(-(n * TOP_K + N_EXPERTS * (blk - 1)) // blk)
    m_pad = n_steps * blk
    n_valid = pad_end[-1] // blk
    blk_idx = jnp.arange(n_steps, dtype=jnp.int32)
    used = jnp.minimum(blk_idx, n_valid - 1)
    blk_exp = jnp.minimum(jnp.sum(used[:, None] * blk >= pad_end[None, :], axis=1), N_EXPERTS - 1).astype(jnp.int32)
    run_first = jnp.logical_and(blk_idx < n_valid, blk_exp != jnp.concatenate([blk_exp[:1] - 1, blk_exp[:-1]]))
    run_first = run_first.astype(jnp.int32)
    run_slot = (jnp.cumsum(run_first) - 1) % 2
    e_ids = jnp.arange(N_EXPERTS, dtype=jnp.int32)
    later = jnp.where(jnp.logical_and(padded[None, :] > 0, e_ids[None, :] > e_ids[:, None]), e_ids[None, :], N_EXPERTS)
    following = jnp.min(later, axis=1)
    run_next = jnp.where(following == N_EXPERTS, -1, following)[blk_exp]
    n_valid = n_valid.reshape(1)

    xs = _fill(pad_start + counts, padded - counts, n_valid, _scatter_rows(hf, slot, m_pad), blk)
    ys = _experts(blk_exp, run_first, run_slot.astype(jnp.int32), run_next.astype(jnp.int32), n_valid, xs,
                  w_gu, b_gu, w_down, b_down, blk)
    rows = _gather_rows(ys, slot.reshape(-1))
    return _combine(rows, x2, gates.T, row(g_out), _pick(n, 256))


def kernel(x, mem, g_mix, w_in, w_gla_gate, b_gla_gate, g_gla_head, w_out, g_mem_q, g_mem_kv, w_mq, w_mk, w_mv,
           w_mo, g_ffn, w_router, b_router, w_gu, b_gu, w_down, b_down, g_final):
    batch, seq, d = x.shape
    depth = g_mix.shape[0]
    assert depth == 1, "the final rmsnorm is fused into the single layer's combine step"
    y = _layer(x.reshape(batch * seq, d), mem, batch, seq, g_mix[0], w_in[0], w_gla_gate[0], b_gla_gate[0],
               g_gla_head[0], w_out[0], g_mem_q[0], g_mem_kv[0], w_mq[0], w_mk[0], w_mv[0], w_mo[0], g_ffn[0],
               w_router[0], b_router[0], w_gu[0], b_gu[0], w_down[0], b_down[0], g_final)
    return y.reshape(batch, seq, d)
```

```python
import functools

import jax
import jax.numpy as jnp
import numpy as np
from jax import lax
from jax.experimental import pallas as pl
from jax.experimental.pallas import tpu as pltpu
from jax.experimental.pallas import tpu_sc as plsc

F32 = jnp.float32
BF16 = jnp.bfloat16

EPS = 1e-5
CHUNK = 64
GLA_HEADS = 4
GLA_DK = 64
GLA_DV = 128
GLA_GATE_RANK = 16
GLA_TAU = 16.0
SB_HEADS = 8
SB_HD = 64
MEM_HEADS = 4
N_EXPERTS = 32
TOP_K = 4
SWIGLU_LIMIT = 7.0
SWIGLU_ALPHA = 1.702
SB_LOG_UNDERFLOW = -104.0
SB_HEADS_PER_STEP = 8

LANES = 128
Q_G = GLA_HEADS * GLA_DK
V_G = GLA_HEADS * GLA_DV
SB_W = SB_HEADS * SB_HD
GLA_SLAB = Q_G + Q_G + V_G + V_G + LANES

VMEM_LIMIT = 56 * 1024 * 1024
TILE_ROWS = 8
COMBINE_PARTS = 2

NT_DIMS = (((1,), (1,)), ((), ()))
TN_DIMS = (((0,), (0,)), ((), ()))


def _dot(a, b):
    return jnp.dot(a, b, preferred_element_type=F32)


def _split(x):
    hi = x.astype(BF16)
    lo = (x - hi.astype(F32)).astype(BF16)
    return hi, lo


def _dot_exact_lhs(a_bf16, b_f32):
    hi, lo = _split(b_f32)
    return _dot(a_bf16, hi) + _dot(a_bf16, lo)


def _dot3(a, b):
    ah, al = _split(a)
    bh, bl = _split(b)
    return _dot(ah, bh) + (_dot(ah, bl) + _dot(al, bh))


def _rmsnorm(x, g):
    return x * lax.rsqrt(jnp.mean(x * x, axis=-1, keepdims=True) + EPS) * g


def _softplus(z):
    return jnp.maximum(z, 0.0) + jnp.log(1.0 + jnp.exp(-jnp.abs(z)))


def _params(sem, vmem=VMEM_LIMIT):
    return pltpu.CompilerParams(dimension_semantics=sem, vmem_limit_bytes=vmem)


def _inproj_kernel(x_ref, g_ref, wg_ref, wq_ref, wk_ref, wv_ref, slab_ref, q_ref, k_ref, v_ref):
    h = _rmsnorm(x_ref[...], g_ref[...]).astype(BF16)
    slab_ref[...] = _dot(h, wg_ref[...])
    q_ref[...] = (_dot(h, wq_ref[...]) * (SB_HD ** -0.5)).astype(BF16)
    k_ref[...] = _dot(h, wk_ref[...]).astype(BF16)
    v_ref[...] = _dot(h, wv_ref[...]).astype(BF16)


def _in_proj(x2d, g, wg, wq, wk, wv, tm):
    n, d = x2d.shape
    full = lambda a: pl.BlockSpec(a.shape, lambda i: (0, 0))
    return pl.pallas_call(
        _inproj_kernel,
        grid=(n // tm,),
        in_specs=[pl.BlockSpec((tm, d), lambda i: (i, 0)), full(g), full(wg), full(wq), full(wk), full(wv)],
        out_specs=[pl.BlockSpec((tm, GLA_SLAB), lambda i: (i, 0))] + [pl.BlockSpec((tm, SB_W), lambda i: (i, 0))] * 3,
        out_shape=[jax.ShapeDtypeStruct((n, GLA_SLAB), F32)] + [jax.ShapeDtypeStruct((n, SB_W), BF16)] * 3,
        compiler_params=_params(("parallel",)),
        name="in_proj",
    )(x2d, g, wg, wq, wk, wv)


def _gla_kernel(slab_ref, wgate_ref, bgate_ref, ghead_ref, umat_ref, o_ref, state_ref, *, n_chunks):
    @pl.when(pl.program_id(1) == 0)
    def _():
        state_ref[...] = jnp.zeros_like(state_ref)

    qg = slab_ref[:, 0:Q_G] * (GLA_DK ** -0.5)
    kg = slab_ref[:, Q_G:2 * Q_G]
    glr = slab_ref[:, 2 * Q_G + 2 * V_G:GLA_SLAB]
    log_a = -_softplus(-(_dot3(glr, wgate_ref[...]) + bgate_ref[...])) * (1.0 / GLA_TAU)
    to_end = _dot_exact_lhs(umat_ref[...], log_a)
    kdec = kg * jnp.exp(to_end)
    g_chunk = to_end + log_a

    lane = lax.broadcasted_iota(jnp.int32, (CHUNK, LANES), 1)
    ghead = ghead_ref[...]
    for c in range(n_chunks):
        rows = slice(c * CHUNK, (c + 1) * CHUNK)
        for h in range(GLA_HEADS):
            pair = slice((h // 2) * LANES, (h // 2 + 1) * LANES)
            mine = (lane >= GLA_DK) if h % 2 else (lane < GLA_DK)
            kd = jnp.where(mine, kdec[rows, pair], 0.0).astype(BF16)
            qm = jnp.where(mine, qg[rows, pair], 0.0).astype(BF16)
            vh = slab_ref[rows, 2 * Q_G + h * GLA_DV:2 * Q_G + (h + 1) * GLA_DV].astype(BF16)
            decay = jnp.exp(g_chunk[c * CHUNK:c * CHUNK + 1, pair])
            st = decay * state_ref[h] + lax.dot_general(vh, kd, TN_DIMS, preferred_element_type=F32)
            state_ref[h] = st
            o = lax.dot_general(qm, st.astype(BF16), NT_DIMS, preferred_element_type=F32)
            rg = slab_ref[rows, 2 * Q_G + V_G + h * GLA_DV:2 * Q_G + V_G + (h + 1) * GLA_DV]
            o = _rmsnorm(o, ghead) * (rg * jax.nn.sigmoid(rg))
            o_ref[rows, h * GLA_DV:(h + 1) * GLA_DV] = o.astype(BF16)


def _gla(slab, wgate, bgate, ghead, batch, seq, ts):
    n = slab.shape[0]
    n_chunks = ts // CHUNK
    r = np.arange(ts)
    umat = jnp.asarray((r[None, :] > r[:, None]) & (r[None, :] // CHUNK == r[:, None] // CHUNK), BF16)
    full = lambda a: pl.BlockSpec(a.shape, lambda b, i: (0,) * a.ndim)
    steps = seq // ts
    return pl.pallas_call(
        functools.partial(_gla_kernel, n_chunks=n_chunks),
        grid=(batch, steps),
        in_specs=[pl.BlockSpec((ts, GLA_SLAB), lambda b, i: (b * steps + i, 0)),
                  full(wgate), full(bgate), full(ghead), full(umat)],
        out_specs=pl.BlockSpec((ts, V_G), lambda b, i: (b * steps + i, 0)),
        out_shape=jax.ShapeDtypeStruct((n, V_G), BF16),
        scratch_shapes=[pltpu.VMEM((GLA_HEADS, GLA_DV, LANES), F32)],
        compiler_params=_params(("parallel", "arbitrary")),
        name="gla",
    )(slab, wgate, bgate, ghead, umat)


def _sb_kernel(q_ref, k_ref, v_ref, tmat_ref, o_ref, acc_ref, carry_ref, *, blk):
    i = pl.program_id(2)
    sub = blk // 2
    lane = lax.broadcasted_iota(jnp.int32, (blk, LANES), 1)
    causal = lax.broadcasted_iota(jnp.int32, (blk, blk), 1) < lax.broadcasted_iota(jnp.int32, (blk, blk), 0)
    lo_head = lane < SB_HD
    n_heads = 2 * q_ref.shape[-1] // LANES
    q_heads = []
    for p in range(n_heads // 2):
        q = q_ref[:, p * LANES:(p + 1) * LANES]
        q_heads += [jnp.where(lo_head, q, jnp.zeros_like(q)), jnp.where(lo_head, jnp.zeros_like(q), q)]
    tmat = tmat_ref[...]

    def chunk(j, diag):
        start = pl.multiple_of(j * blk, blk)
        worst = None
        for h in range(n_heads):
            slab = slice((h // 2) * LANES, (h // 2 + 1) * LANES)
            kj = k_ref[pl.ds(start, blk), slab]
            vj = v_ref[pl.ds(start, blk), slab]
            z = lax.dot_general(q_heads[h], kj, NT_DIMS, preferred_element_type=F32)
            sp = _softplus(z)
            log1mb = -sp
            if diag:
                log1mb = jnp.where(causal, log1mb, 0.0)
            hi, lo = _split(log1mb)
            right = _dot(jnp.concatenate([hi[:, sub:], lo[:, sub:]], axis=1), tmat)
            left = _dot(jnp.concatenate([hi[:, :sub], lo[:, :sub]], axis=1), tmat)
            log_sig = z - sp
            if diag:
                after_right = right[:, sub:]
                log_a_right = log_sig[:, sub:] + right[:, :sub]
            else:
                carry = carry_ref[h]
                after_right = carry + right[:, sub:]
                log_a_right = log_sig[:, sub:] + right[:, :sub] + carry
            log_a_left = log_sig[:, :sub] + left[:, :sub] + after_right
            a = jnp.exp(jnp.concatenate([log_a_left, log_a_right], axis=1))
            if diag:
                a = jnp.where(causal, a, 0.0)
            pv = _dot(a.astype(BF16), vj)
            acc_ref[h] = pv if diag else acc_ref[h] + pv
            remaining = after_right + left[:, sub:]
            carry_ref[h] = remaining
            top = jnp.max(remaining)
            worst = top if worst is None else jnp.maximum(worst, top)
        return worst

    def more(state):
        return jnp.logical_and(state[0] >= 0, state[1] > SB_LOG_UNDERFLOW)

    def first_two():
        chunk(i, True)
        return chunk(i - 1, False)

    start = lax.cond(i >= 1, first_two, lambda: chunk(i, True))
    lax.while_loop(more, lambda state: (state[0] - 1, chunk(state[0], False)), (i - 2, start))
    for p in range(n_heads // 2):
        o_ref[:, p * LANES:(p + 1) * LANES] = jnp.where(lo_head, acc_ref[2 * p], acc_ref[2 * p + 1]).astype(BF16)


def _sb(q, k, v, batch, seq, blk):
    sub = blk // 2
    assert sub == LANES
    r = np.arange(sub)
    tri = (r[:, None] > r[None, :]).astype(np.float32)
    half = np.concatenate([tri, np.ones((sub, sub), np.float32)], axis=1)
    tmat = jnp.asarray(np.concatenate([half, half], axis=0), BF16)
    nq = seq // blk
    width = SB_HEADS_PER_STEP // 2 * LANES
    groups = SB_W // width
    q3, k3, v3 = (a.reshape(batch, seq, SB_W) for a in (q, k, v))
    out = pl.pallas_call(
        functools.partial(_sb_kernel, blk=blk),
        grid=(batch, groups, nq),
        in_specs=[pl.BlockSpec((None, blk, width), lambda b, p, i: (b, i, p)),
                  pl.BlockSpec((None, seq, width), lambda b, p, i: (b, 0, p)),
                  pl.BlockSpec((None, seq, width), lambda b, p, i: (b, 0, p)),
                  pl.BlockSpec(tmat.shape, lambda b, p, i: (0, 0))],
        out_specs=pl.BlockSpec((None, blk, width), lambda b, p, i: (b, i, p)),
        out_shape=jax.ShapeDtypeStruct((batch, seq, SB_W), BF16),
        scratch_shapes=[pltpu.VMEM((SB_HEADS_PER_STEP, blk, LANES), F32)] * 2,
        compiler_params=_params(("parallel", "parallel", "arbitrary")),
        name="sb",
    )(q3, k3, v3, tmat)
    return out.reshape(batch * seq, SB_W)


def _memkv_kernel(m_ref, g_ref, wk_ref, wv_ref, k_ref, v_ref):
    hm = _rmsnorm(m_ref[...], g_ref[...]).astype(BF16)
    k_ref[...] = _dot(hm, wk_ref[...]).astype(BF16)
    v_ref[...] = _dot(hm, wv_ref[...]).astype(BF16)


def _mem_kv(mem, g, wk, wv):
    b, m, d = mem.shape
    full = lambda a: pl.BlockSpec(a.shape, lambda i: (0, 0))
    blk = pl.BlockSpec((None, m, d), lambda i: (i, 0, 0))
    return pl.pallas_call(
        _memkv_kernel,
        grid=(b,),
        in_specs=[blk, full(g), full(wk), full(wv)],
        out_specs=[blk, blk],
        out_shape=[jax.ShapeDtypeStruct((b, m, d), BF16)] * 2,
        compiler_params=_params(("parallel",)),
        name="mem_kv",
    )(mem, g, wk, wv)


def _mid_kernel(x_ref, og_ref, os_ref, wog_ref, wos_ref, gq_ref, wmq_ref, km_ref, vm_ref, wmo_ref, gf_ref,
                x2_ref, hf_ref):
    x1 = x_ref[...] + _dot(og_ref[...], wog_ref[...]) + _dot(os_ref[...], wos_ref[...])
    hq = _rmsnorm(x1, gq_ref[...]).astype(BF16)
    d = x1.shape[-1]
    hd = d // MEM_HEADS
    q = (_dot(hq, wmq_ref[...]) * (hd ** -0.5)).astype(BF16)
    outs = []
    for h in range(MEM_HEADS):
        cols = slice(h * hd, (h + 1) * hd)
        s = lax.dot_general(q[:, cols], km_ref[:, cols], NT_DIMS, preferred_element_type=F32)
        e = jnp.exp(s - jnp.max(s, axis=-1, keepdims=True))
        p = (e / jnp.sum(e, axis=-1, keepdims=True)).astype(BF16)
        outs.append(_dot(p, vm_ref[:, cols]).astype(BF16))
    x2 = x1 + _dot(jnp.concatenate(outs, axis=1), wmo_ref[...])
    x2_ref[...] = x2
    hf_ref[...] = _rmsnorm(x2, gf_ref[...])


def _mid(x2d, og, osb, wog, wos, gq, wmq, km, vm, wmo, gf, seq, tm):
    n, d = x2d.shape
    m = km.shape[1]
    per_batch = seq // tm
    full = lambda a: pl.BlockSpec(a.shape, lambda i: (0, 0))
    rows = lambda w: pl.BlockSpec((tm, w), lambda i: (i, 0))
    mem = pl.BlockSpec((None, m, d), lambda i: (i // per_batch, 0, 0))
    return pl.pallas_call(
        _mid_kernel,
        grid=(n // tm,),
        in_specs=[rows(d), rows(V_G), rows(SB_W), full(wog), full(wos), full(gq), full(wmq), mem, mem,
                  full(wmo), full(gf)],
        out_specs=[rows(d), rows(d)],
        out_shape=[jax.ShapeDtypeStruct((n, d), F32)] * 2,
        compiler_params=_params(("parallel",)),
        name="mid",
    )(x2d, og, osb, wog, wos, gq, wmq, km, vm, wmo, gf)


def _router_kernel(hf_ref, wrt_ref, br_ref, cmat_ref, idx_ref, gate_ref, rank_ref, cnt_ref, carry_ref, *, tm):
    @pl.when(pl.program_id(0) == 0)
    def _():
        carry_ref[...] = jnp.zeros_like(carry_ref)

    hh, hl = _split(hf_ref[...])
    wh, wl = _split(wrt_ref[...])
    nt = lambda a, b: lax.dot_general(a, b, NT_DIMS, preferred_element_type=F32)
    vals = nt(wh, hh) + (nt(wh, hl) + nt(wl, hh)) + br_ref[...]
    eidx = lax.broadcasted_iota(jnp.int32, (N_EXPERTS, tm), 0)
    tops, sels, hots = [], [], []
    for _ in range(TOP_K):
        m = jnp.max(vals, axis=0, keepdims=True)
        sel = jnp.min(jnp.where(vals == m, eidx, N_EXPERTS), axis=0, keepdims=True)
        hot = eidx == sel
        vals = jnp.where(hot, -jnp.inf, vals)
        tops.append(m)
        sels.append(sel)
        hots.append(hot)
    exps = [jnp.exp(t - tops[0]) for t in tops]
    denom = exps[0] + exps[1] + exps[2] + exps[3]
    chosen = jnp.zeros((N_EXPERTS, tm), F32)
    for hot in hots:
        chosen = chosen + hot.astype(F32)
    sums = _dot(chosen.astype(BF16), cmat_ref[...])
    before = sums[:, :tm] + carry_ref[...]
    for k in range(TOP_K):
        idx_ref[k:k + 1, :] = sels[k]
        gate_ref[k:k + 1, :] = exps[k] / denom
        rank_ref[k:k + 1, :] = jnp.sum(jnp.where(hots[k], before, 0.0), axis=0, keepdims=True).astype(jnp.int32)
    carry_ref[...] = carry_ref[...] + sums[:, tm:]
    cnt_ref[...] = carry_ref[...]


def _router(hf, wrt, br, tm):
    n, d = hf.shape
    r = np.arange(tm)
    cmat = jnp.asarray(np.concatenate([(r[:, None] < r[None, :]).astype(np.float32),
                                       np.ones((tm, tm), np.float32)], axis=1), BF16)
    full = lambda a: pl.BlockSpec(a.shape, lambda i: (0, 0))
    tok = pl.BlockSpec((TOP_K, tm), lambda i: (0, i))
    return pl.pallas_call(
        functools.partial(_router_kernel, tm=tm),
        grid=(n // tm,),
        in_specs=[pl.BlockSpec((tm, d), lambda i: (i, 0)), full(wrt), full(br), full(cmat)],
        out_specs=[tok, tok, tok, pl.BlockSpec((N_EXPERTS, tm), lambda i: (0, 0))],
        out_shape=[jax.ShapeDtypeStruct((TOP_K, n), jnp.int32), jax.ShapeDtypeStruct((TOP_K, n), F32),
                   jax.ShapeDtypeStruct((TOP_K, n), jnp.int32), jax.ShapeDtypeStruct((N_EXPERTS, tm), F32)],
        scratch_shapes=[pltpu.VMEM((N_EXPERTS, tm), F32)],
        compiler_params=_params(("arbitrary",)),
        name="router",
    )(hf, wrt, br, cmat)


SC_CHUNK = 32


def _sc_workers():
    info = plsc.get_sparse_core_info()
    mesh = plsc.VectorSubcoreMesh(core_axis_name="c", subcore_axis_name="s")
    return info.num_cores, info.num_cores * info.num_subcores, mesh


def _scatter_rows(src, slot, n_out):
    n_cores, n_workers, mesh = _sc_workers()
    n, d = src.shape
    per_worker = n // n_workers
    n_chunks = per_worker // SC_CHUNK
    assert n == n_workers * n_chunks * SC_CHUNK and n_chunks % 2 == 0

    def body(src_hbm, idx_hbm, out_hbm, idx_v, rows_v, lsem, ssem):
        wid = lax.axis_index("s") * n_cores + lax.axis_index("c")
        base = wid * per_worker
        pltpu.sync_copy(idx_hbm.at[wid], idx_v)

        def load(j, b):
            return pltpu.make_async_copy(src_hbm.at[pl.ds(base + j * SC_CHUNK, SC_CHUNK)], rows_v.at[b], lsem.at[b])

        def scatters(j, b):
            return [pltpu.make_async_copy(rows_v.at[b], out_hbm.at[idx_v.at[k, j]], ssem.at[b]) for k in range(TOP_K)]

        load(0, 0).start()

        @pl.loop(0, n_chunks, step=2)
        def _(j0):
            for b in range(2):
                j = j0 + b

                @pl.when(j >= 1)
                def _():
                    for c in scatters(j - 1, 1 - b):
                        c.wait()

                @pl.when(j + 1 < n_chunks)
                def _():
                    load(j + 1, 1 - b).start()

                load(j, b).wait()
                for c in scatters(j, b):
                    c.start()

        for c in scatters(n_chunks - 1, 1):
            c.wait()

    call = pl.kernel(
        body,
        out_type=jax.ShapeDtypeStruct((n_out, d), src.dtype),
        mesh=mesh,
        scratch_types=[pltpu.VMEM((TOP_K, n_chunks, SC_CHUNK), jnp.int32), pltpu.VMEM((2, SC_CHUNK, d), src.dtype),
                       pltpu.SemaphoreType.DMA((2,)), pltpu.SemaphoreType.DMA((2,))],
    )
    idx = slot.reshape(TOP_K, n_workers, n_chunks, SC_CHUNK).transpose(1, 0, 2, 3)
    return call(src, idx)


def _fill_kernel(fill_start_ref, fill_n_ref, nvalid_ref, xs_in, xs_ref, zero_ref, sem, *, blk):
    del xs_in
    n_blocks = xs_ref.shape[0] // blk

    def block_copy(b):
        return pltpu.make_async_copy(zero_ref, xs_ref.at[pl.ds(pl.multiple_of(b * blk, blk), blk), :], sem)

    def pieces(e):
        start, n = fill_start_ref[e], fill_n_ref[e]
        head = (-start) & (TILE_ROWS - 1)
        for r in range(TILE_ROWS - 1):
            copy = pltpu.make_async_copy(zero_ref.at[pl.ds(0, 1), :], xs_ref.at[pl.ds(start + r, 1), :], sem)
            yield r < jnp.minimum(head, n), copy
        aligned, rest = start + head, n - head
        size = blk // 2
        while size >= TILE_ROWS:
            at = pl.multiple_of(aligned + (rest & ~(2 * size - 1)), TILE_ROWS)
            copy = pltpu.make_async_copy(zero_ref.at[pl.ds(0, size), :], xs_ref.at[pl.ds(at, size), :], sem)
            yield (rest & size) != 0, copy
            size //= 2

    zero_ref[...] = jnp.zeros_like(zero_ref)
    for e in range(N_EXPERTS):
        for wanted, copy in pieces(e):
            pl.when(wanted)(copy.start)
    for e in range(N_EXPERTS):
        for wanted, copy in pieces(e):
            pl.when(wanted)(copy.wait)
    lax.fori_loop(nvalid_ref[0], n_blocks, lambda b, _: (block_copy(b).start(), 0)[1], 0)
    lax.fori_loop(nvalid_ref[0], n_blocks, lambda b, _: (block_copy(b).wait(), 0)[1], 0)


def _fill(fill_start, fill_n, n_valid, xs, blk):
    m_pad, d = xs.shape
    hbm = pl.BlockSpec(memory_space=pl.ANY)
    return pl.pallas_call(
        functools.partial(_fill_kernel, blk=blk),
        grid_spec=pltpu.PrefetchScalarGridSpec(
            num_scalar_prefetch=3,
            grid=(1,),
            in_specs=[hbm],
            out_specs=hbm,
            scratch_shapes=[pltpu.VMEM((blk, d), F32), pltpu.SemaphoreType.DMA],
        ),
        out_shape=jax.ShapeDtypeStruct((m_pad, d), F32),
        input_output_aliases={3: 0},
        compiler_params=_params(("arbitrary",)),
        name="fill",
    )(fill_start, fill_n, n_valid, xs)


def _experts_kernel(exp_ref, first_ref, slot_ref, next_ref, nvalid_ref, xs_ref, wgu_hbm, bgu_ref, wd_hbm, bd_ref,
                    ys_ref, wgu_f32, wd_f32, wgu_bf, wd_bf, wsem):
    i = pl.program_id(0)
    valid = i < nvalid_ref[0]
    slot = slot_ref[i]

    def weight_copies(e, s):
        return (pltpu.make_async_copy(wgu_hbm.at[e], wgu_f32.at[s], wsem.at[s]),
                pltpu.make_async_copy(wd_hbm.at[e], wd_f32.at[s], wsem.at[s]))

    @pl.when(jnp.logical_and(valid, first_ref[i] == 1))
    def _():
        @pl.when(i == 0)
        def _():
            for c in weight_copies(exp_ref[0], 0):
                c.start()

        for c in weight_copies(exp_ref[i], slot):
            c.wait()
        wgu_bf[...] = wgu_f32[slot].astype(BF16)
        wd_bf[...] = wd_f32[slot].astype(BF16)

        @pl.when(next_ref[i] >= 0)
        def _():
            for c in weight_copies(next_ref[i], 1 - slot):
                c.start()

    @pl.when(valid)
    def _():
        f = wd_bf.shape[0]
        gu = _dot(xs_ref[...].astype(BF16), wgu_bf[...]) + bgu_ref[...]
        gate = jnp.minimum(gu[:, :f], SWIGLU_LIMIT)
        lin = jnp.clip(gu[:, f:], -SWIGLU_LIMIT, SWIGLU_LIMIT)
        act = (lin + 1.0) * (gate * jax.nn.sigmoid(SWIGLU_ALPHA * gate))
        ys_ref[...] = _dot(act.astype(BF16), wd_bf[...]) + bd_ref[...]

    @pl.when(jnp.logical_not(valid))
    def _():
        ys_ref[...] = jnp.zeros_like(ys_ref)


def _experts(blk_exp, run_first, run_slot, run_next, n_valid, xs, w_gu, b_gu, w_down, b_down, blk):
    m_pad, d = xs.shape
    e, _, f2 = w_gu.shape
    f = f2 // 2
    n_steps = m_pad // blk
    rows = pl.BlockSpec((blk, d), lambda i, *_: (i, 0))
    return pl.pallas_call(
        _experts_kernel,
        grid_spec=pltpu.PrefetchScalarGridSpec(
            num_scalar_prefetch=5,
            grid=(n_steps,),
            in_specs=[rows,
                      pl.BlockSpec(memory_space=pl.ANY),
                      pl.BlockSpec((None, 1, f2), lambda i, x, *_: (x[i], 0, 0)),
                      pl.BlockSpec(memory_space=pl.ANY),
                      pl.BlockSpec((None, 1, d), lambda i, x, *_: (x[i], 0, 0))],
            out_specs=rows,
            scratch_shapes=[pltpu.VMEM((2, d, f2), F32), pltpu.VMEM((2, f, d), F32),
                            pltpu.VMEM((d, f2), BF16), pltpu.VMEM((f, d), BF16),
                            pltpu.SemaphoreType.DMA((2,))],
        ),
        out_shape=jax.ShapeDtypeStruct((m_pad, d), F32),
        compiler_params=_params(("arbitrary",)),
        name="experts",
    )(blk_exp, run_first, run_slot, run_next, n_valid, xs, w_gu, b_gu.reshape(e, 1, f2), w_down,
      b_down.reshape(e, 1, d))


def _gather_rows(table, idx):
    n_cores, n_workers, mesh = _sc_workers()
    n_rows, d = idx.shape[0], table.shape[1]
    per_worker = n_rows // n_workers
    n_chunks = per_worker // SC_CHUNK
    assert n_rows == n_workers * n_chunks * SC_CHUNK and n_chunks % 2 == 0

    def body(table_hbm, idx_hbm, out_hbm, idx_v, rows_v, gsem, wsem):
        wid = lax.axis_index("s") * n_cores + lax.axis_index("c")
        base = wid * per_worker
        pltpu.sync_copy(idx_hbm.at[wid], idx_v)

        def gather(j, b):
            return pltpu.make_async_copy(table_hbm.at[idx_v.at[j]], rows_v.at[b], gsem.at[b])

        def put(j, b):
            return pltpu.make_async_copy(rows_v.at[b], out_hbm.at[pl.ds(base + j * SC_CHUNK, SC_CHUNK)], wsem.at[b])

        gather(0, 0).start()

        @pl.loop(0, n_chunks, step=2)
        def _(j0):
            for b in range(2):
                j = j0 + b

                @pl.when(j >= 1)
                def _():
                    put(j - 1, 1 - b).wait()

                @pl.when(j + 1 < n_chunks)
                def _():
                    gather(j + 1, 1 - b).start()

                gather(j, b).wait()
                put(j, b).start()

        put(n_chunks - 1, 1).wait()

    call = pl.kernel(
        body,
        out_type=jax.ShapeDtypeStruct((n_rows, d), table.dtype),
        mesh=mesh,
        scratch_types=[pltpu.VMEM((n_chunks, SC_CHUNK), jnp.int32), pltpu.VMEM((2, SC_CHUNK, d), table.dtype),
                       pltpu.SemaphoreType.DMA((2,)), pltpu.SemaphoreType.DMA((2,))],
    )
    return call(table, idx.reshape(n_workers, n_chunks, SC_CHUNK))


def _combine_kernel(o0_ref, o1_ref, o2_ref, o3_ref, x2_ref, gate_ref, gfin_ref, *rest):
    y_ref = rest[-1]
    acc = x2_ref[...]
    for k, o_ref in enumerate((o0_ref, o1_ref, o2_ref, o3_ref)):
        acc = acc + gate_ref[:, k:k + 1] * o_ref[...]
    y_ref[...] = _rmsnorm(acc, gfin_ref[...])


def _combine(rows, x2, gates, gfin, tm, part, n_parts, y_prev):
    n, d = x2.shape
    per_k = n // n_parts // tm
    first = part * per_k
    planes = [pl.BlockSpec((tm, d), functools.partial(lambda i, k: (k * per_k + i, 0), k=k)) for k in range(TOP_K)]
    tokens = lambda w: pl.BlockSpec((tm, w), lambda i: (first + i, 0))
    in_specs = planes + [tokens(d), tokens(TOP_K), pl.BlockSpec(gfin.shape, lambda i: (0, 0))]
    args = [rows, rows, rows, rows, x2, gates, gfin]
    aliases = {}
    if y_prev is not None:
        in_specs.append(pl.BlockSpec(memory_space=pl.ANY))
        args.append(y_prev)
        aliases = {len(args) - 1: 0}
    return pl.pallas_call(
        _combine_kernel,
        grid=(per_k,),
        in_specs=in_specs,
        out_specs=tokens(d),
        out_shape=jax.ShapeDtypeStruct((n, d), F32),
        input_output_aliases=aliases,
        compiler_params=_params(("parallel",)),
        name="combine",
    )(*args)


def _pick(n, pref):
    t = min(pref, n)
    while n % t:
        t //= 2
    return t


def _layer(x2d, mem, batch, seq, g_mix, w_in, w_gla_gate, b_gla_gate, g_gla_head, w_out, g_mem_q, g_mem_kv,
           w_mq, w_mk, w_mv, w_mo, g_ffn, w_router, b_router, w_gu, b_gu, w_down, b_down, g_out):
    n, d = x2d.shape
    row = lambda v: v.reshape(1, -1).astype(F32)

    o_qg, o_kg, o_vg, o_lr, o_rg, o_qs, o_ks, o_vs = np.cumsum((0, Q_G, Q_G, V_G, GLA_GATE_RANK, V_G, SB_W, SB_W))
    lr_pad = jnp.zeros((d, LANES - GLA_GATE_RANK), w_in.dtype)
    wg = jnp.concatenate([w_in[:, o_qg:o_vg + V_G], w_in[:, o_rg:o_rg + V_G],
                          w_in[:, o_lr:o_lr + GLA_GATE_RANK], lr_pad], axis=1).astype(BF16)
    wq, wk, wv = (w_in[:, o:o + SB_W].astype(BF16) for o in (o_qs, o_ks, o_vs))
    wgate = jnp.concatenate([w_gla_gate, jnp.zeros((LANES - GLA_GATE_RANK, Q_G), F32)], axis=0)

    slab, qs, ks, vs = _in_proj(x2d, row(g_mix), wg, wq, wk, wv, _pick(n, 512))
    o_gla = _gla(slab, wgate, row(b_gla_gate), row(g_gla_head), batch, seq, _pick(seq, 512))
    o_sb = _sb(qs, ks, vs, batch, seq, 2 * LANES)
    km, vm = _mem_kv(mem, row(g_mem_kv), w_mk.astype(BF16), w_mv.astype(BF16))
    w_out_bf = w_out.astype(BF16)
    x2, hf = _mid(x2d, o_gla, o_sb, w_out_bf[:V_G], w_out_bf[V_G:], row(g_mem_q), w_mq.astype(BF16), km, vm,
                  w_mo.astype(BF16), row(g_ffn), seq, _pick(seq, 512))

    tm_r = _pick(n, 256)
    idx, gates, rank, cnt = _router(hf, w_router.T, b_router.reshape(-1, 1), tm_r)

    blk = 512
    counts = cnt[:, 0].astype(jnp.int32)
    padded = ((counts + blk - 1) // blk) * blk
    pad_end = jnp.cumsum(padded)
    pad_start = pad_end - padded
    first = jnp.sum(jnp.where(idx[..., None] == jnp.arange(N_EXPERTS), pad_start, 0), axis=-1)
    slot = first + rank
    n_steps = -(-(n * TOP_K + N_EXPERTS * (blk - 1)) // blk)
    m_pad = n_steps * blk
    n_valid = pad_end[-1] // blk
    blk_idx = jnp.arange(n_steps, dtype=jnp.int32)
    used = jnp.minimum(blk_idx, n_valid - 1)
    blk_exp = jnp.minimum(jnp.sum(used[:, None] * blk >= pad_end[None, :], axis=1), N_EXPERTS - 1).astype(jnp.int32)
    run_first = jnp.logical_and(blk_idx < n_valid, blk_exp != jnp.concatenate([blk_exp[:1] - 1, blk_exp[:-1]]))
    run_first = run_first.astype(jnp.int32)
    run_slot = (jnp.cumsum(run_first) - 1) % 2
    e_ids = jnp.arange(N_EXPERTS, dtype=jnp.int32)
    later = jnp.where(jnp.logical_and(padded[None, :] > 0, e_ids[None, :] > e_ids[:, None]), e_ids[None, :], N_EXPERTS)
    following = jnp.min(later, axis=1)
    run_next = jnp.where(following == N_EXPERTS, -1, following)[blk_exp]
    n_valid = n_valid.reshape(1)

    xs = _fill(pad_start + counts, padded - counts, n_valid, _scatter_rows(hf, slot, m_pad), blk)
    ys = _experts(blk_exp, run_first, run_slot.astype(jnp.int32), run_next.astype(jnp.int32), n_valid, xs,
                  w_gu, b_gu, w_down, b_down, blk)
    y = None
    per_part = n // COMBINE_PARTS
    for part in range(COMBINE_PARTS):
        part_slot = slot[:, part * per_part:(part + 1) * per_part]
        rows = _gather_rows(ys, part_slot.reshape(-1))
        y = _combine(rows, x2, gates.T, row(g_out), _pick(per_part, 256), part, COMBINE_PARTS, y)
    return y


def kernel(x, mem, g_mix, w_in, w_gla_gate, b_gla_gate, g_gla_head, w_out, g_mem_q, g_mem_kv, w_mq, w_mk, w_mv,
           w_mo, g_ffn, w_router, b_router, w_gu, b_gu, w_down, b_down, g_final):
    batch, seq, d = x.shape
    depth = g_mix.shape[0]
    assert depth == 1, "the final rmsnorm is fused into the single layer's combine step"
    y = _layer(x.reshape(batch * seq, d), mem, batch, seq, g_mix[0], w_in[0], w_gla_gate[0], b_gla_gate[0],
               g_gla_head[0], w_out[0], g_mem_q[0], g_mem_kv[0], w_mq[0], w_mk[0], w_mv[0], w_mo[0], g_ffn[0],
               w_router[0], b_router[0], w_gu[0], b_gu[0], w_down[0], b_down[0], g_final)
    return y.reshape(batch, seq, d)
```

```python
import functools

import jax
import jax.numpy as jnp
import numpy as np
from jax import lax
from jax.experimental import pallas as pl
from jax.experimental.pallas import tpu as pltpu
from jax.experimental.pallas import tpu_sc as plsc

F32 = jnp.float32
BF16 = jnp.bfloat16

EPS = 1e-5
CHUNK = 64
GLA_HEADS = 4
GLA_DK = 64
GLA_DV = 128
GLA_GATE_RANK = 16
GLA_TAU = 16.0
SB_HEADS = 8
SB_HD = 64
MEM_HEADS = 4
N_EXPERTS = 32
TOP_K = 4
SWIGLU_LIMIT = 7.0
SWIGLU_ALPHA = 1.702
LOG2_E = 1.4426950408889634
SB_LOG2_UNDERFLOW = -150.0
SB_HEADS_PER_STEP = 8

LANES = 128
Q_G = GLA_HEADS * GLA_DK
V_G = GLA_HEADS * GLA_DV
SB_W = SB_HEADS * SB_HD
GLA_SLAB = Q_G + Q_G + V_G + V_G + LANES

VMEM_LIMIT = 56 * 1024 * 1024
TILE_ROWS = 8

NT_DIMS = (((1,), (1,)), ((), ()))
TN_DIMS = (((0,), (0,)), ((), ()))


def _dot(a, b):
    return jnp.dot(a, b, preferred_element_type=F32)


def _split(x):
    hi = x.astype(BF16)
    lo = (x - hi.astype(F32)).astype(BF16)
    return hi, lo


def _dot_exact_lhs(a_bf16, b_f32):
    hi, lo = _split(b_f32)
    return _dot(a_bf16, hi) + _dot(a_bf16, lo)


def _dot3(a, b):
    ah, al = _split(a)
    bh, bl = _split(b)
    return _dot(ah, bh) + (_dot(ah, bl) + _dot(al, bh))


def _rmsnorm(x, g):
    return x * lax.rsqrt(jnp.mean(x * x, axis=-1, keepdims=True) + EPS) * g


def _softplus(z):
    return jnp.maximum(z, 0.0) + jnp.log(1.0 + jnp.exp(-jnp.abs(z)))


def _params(sem, vmem=VMEM_LIMIT):
    return pltpu.CompilerParams(dimension_semantics=sem, vmem_limit_bytes=vmem)


def _inproj_kernel(x_ref, g_ref, wg_ref, wq_ref, wk_ref, wv_ref, slab_ref, q_ref, k_ref, v_ref):
    h = _rmsnorm(x_ref[...], g_ref[...]).astype(BF16)
    slab_ref[...] = _dot(h, wg_ref[...])
    q_ref[...] = (_dot(h, wq_ref[...]) * (SB_HD ** -0.5 * LOG2_E)).astype(BF16)
    k_ref[...] = _dot(h, wk_ref[...]).astype(BF16)
    v_ref[...] = _dot(h, wv_ref[...]).astype(BF16)


def _in_proj(x2d, g, wg, wq, wk, wv, tm):
    n, d = x2d.shape
    full = lambda a: pl.BlockSpec(a.shape, lambda i: (0, 0))
    return pl.pallas_call(
        _inproj_kernel,
        grid=(n // tm,),
        in_specs=[pl.BlockSpec((tm, d), lambda i: (i, 0)), full(g), full(wg), full(wq), full(wk), full(wv)],
        out_specs=[pl.BlockSpec((tm, GLA_SLAB), lambda i: (i, 0))] + [pl.BlockSpec((tm, SB_W), lambda i: (i, 0))] * 3,
        out_shape=[jax.ShapeDtypeStruct((n, GLA_SLAB), F32)] + [jax.ShapeDtypeStruct((n, SB_W), BF16)] * 3,
        compiler_params=_params(("parallel",)),
        name="in_proj",
    )(x2d, g, wg, wq, wk, wv)


def _gla_kernel(slab_ref, wgate_ref, bgate_ref, ghead_ref, umat_ref, o_ref, state_ref, *, n_chunks):
    @pl.when(pl.program_id(1) == 0)
    def _():
        state_ref[...] = jnp.zeros_like(state_ref)

    qg = slab_ref[:, 0:Q_G] * (GLA_DK ** -0.5)
    kg = slab_ref[:, Q_G:2 * Q_G]
    glr = slab_ref[:, 2 * Q_G + 2 * V_G:GLA_SLAB]
    log_a = -_softplus(-(_dot3(glr, wgate_ref[...]) + bgate_ref[...])) * (1.0 / GLA_TAU)
    to_end = _dot_exact_lhs(umat_ref[...], log_a)
    kdec = kg * jnp.exp(to_end)
    g_chunk = to_end + log_a

    lane = lax.broadcasted_iota(jnp.int32, (CHUNK, LANES), 1)
    ghead = ghead_ref[...]
    for c in range(n_chunks):
        rows = slice(c * CHUNK, (c + 1) * CHUNK)
        for h in range(GLA_HEADS):
            pair = slice((h // 2) * LANES, (h // 2 + 1) * LANES)
            mine = (lane >= GLA_DK) if h % 2 else (lane < GLA_DK)
            kd = jnp.where(mine, kdec[rows, pair], 0.0).astype(BF16)
            qm = jnp.where(mine, qg[rows, pair], 0.0).astype(BF16)
            vh = slab_ref[rows, 2 * Q_G + h * GLA_DV:2 * Q_G + (h + 1) * GLA_DV].astype(BF16)
            decay = jnp.exp(g_chunk[c * CHUNK:c * CHUNK + 1, pair])
            st = decay * state_ref[h] + lax.dot_general(vh, kd, TN_DIMS, preferred_element_type=F32)
            state_ref[h] = st
            o = lax.dot_general(qm, st.astype(BF16), NT_DIMS, preferred_element_type=F32)
            rg = slab_ref[rows, 2 * Q_G + V_G + h * GLA_DV:2 * Q_G + V_G + (h + 1) * GLA_DV]
            o = _rmsnorm(o, ghead) * (rg * jax.nn.sigmoid(rg))
            o_ref[rows, h * GLA_DV:(h + 1) * GLA_DV] = o.astype(BF16)


def _gla(slab, wgate, bgate, ghead, batch, seq, ts):
    n = slab.shape[0]
    n_chunks = ts // CHUNK
    r = np.arange(ts)
    umat = jnp.asarray((r[None, :] > r[:, None]) & (r[None, :] // CHUNK == r[:, None] // CHUNK), BF16)
    full = lambda a: pl.BlockSpec(a.shape, lambda b, i: (0,) * a.ndim)
    steps = seq // ts
    return pl.pallas_call(
        functools.partial(_gla_kernel, n_chunks=n_chunks),
        grid=(batch, steps),
        in_specs=[pl.BlockSpec((ts, GLA_SLAB), lambda b, i: (b * steps + i, 0)),
                  full(wgate), full(bgate), full(ghead), full(umat)],
        out_specs=pl.BlockSpec((ts, V_G), lambda b, i: (b * steps + i, 0)),
        out_shape=jax.ShapeDtypeStruct((n, V_G), BF16),
        scratch_shapes=[pltpu.VMEM((GLA_HEADS, GLA_DV, LANES), F32)],
        compiler_params=_params(("parallel", "arbitrary")),
        name="gla",
    )(slab, wgate, bgate, ghead, umat)


def _sb_kernel(q_ref, k_ref, v_ref, tmat_ref, o_ref, acc_ref, carry_ref, *, blk):
    i = pl.program_id(2)
    sub = blk // 2
    lane = lax.broadcasted_iota(jnp.int32, (blk, LANES), 1)
    causal = lax.broadcasted_iota(jnp.int32, (blk, blk), 1) < lax.broadcasted_iota(jnp.int32, (blk, blk), 0)
    lo_head = lane < SB_HD
    n_heads = 2 * q_ref.shape[-1] // LANES
    q_heads = []
    for p in range(n_heads // 2):
        q = q_ref[:, p * LANES:(p + 1) * LANES]
        q_heads += [jnp.where(lo_head, q, jnp.zeros_like(q)), jnp.where(lo_head, jnp.zeros_like(q), q)]
    tmat = tmat_ref[...]

    def chunk(j, diag):
        start = pl.multiple_of(j * blk, blk)
        worst = None
        for h in range(n_heads):
            slab = slice((h // 2) * LANES, (h // 2 + 1) * LANES)
            kj = k_ref[pl.ds(start, blk), slab]
            vj = v_ref[pl.ds(start, blk), slab]
            z = lax.dot_general(q_heads[h], kj, NT_DIMS, preferred_element_type=F32)
            sp = jnp.maximum(z, 0.0) + jnp.log2(1.0 + jnp.exp2(-jnp.abs(z)))
            log1mb = -sp
            if diag:
                log1mb = jnp.where(causal, log1mb, 0.0)
            hi, lo = _split(log1mb)
            right = _dot(jnp.concatenate([hi[:, sub:], lo[:, sub:]], axis=1), tmat)
            left = _dot(jnp.concatenate([hi[:, :sub], lo[:, :sub]], axis=1), tmat)
            log_sig = z - sp
            if diag:
                after_right = right[:, sub:]
                log_a_right = log_sig[:, sub:] + right[:, :sub]
            else:
                carry = carry_ref[h]
                after_right = carry + right[:, sub:]
                log_a_right = log_sig[:, sub:] + right[:, :sub] + carry
            log_a_left = log_sig[:, :sub] + left[:, :sub] + after_right
            a = jnp.exp2(jnp.concatenate([log_a_left, log_a_right], axis=1))
            if diag:
                a = jnp.where(causal, a, 0.0)
            pv = _dot(a.astype(BF16), vj)
            acc_ref[h] = pv if diag else acc_ref[h] + pv
            remaining = after_right + left[:, sub:]
            carry_ref[h] = remaining
            top = jnp.max(remaining)
            worst = top if worst is None else jnp.maximum(worst, top)
        return worst

    def more(state):
        return jnp.logical_and(state[0] >= 0, state[1] > SB_LOG2_UNDERFLOW)

    def first_two():
        chunk(i, True)
        return chunk(i - 1, False)

    start = lax.cond(i >= 1, first_two, lambda: chunk(i, True))
    lax.while_loop(more, lambda state: (state[0] - 1, chunk(state[0], False)), (i - 2, start))
    for p in range(n_heads // 2):
        o_ref[:, p * LANES:(p + 1) * LANES] = jnp.where(lo_head, acc_ref[2 * p], acc_ref[2 * p + 1]).astype(BF16)


def _sb(q, k, v, batch, seq, blk):
    sub = blk // 2
    assert sub == LANES
    r = np.arange(sub)
    tri = (r[:, None] > r[None, :]).astype(np.float32)
    half = np.concatenate([tri, np.ones((sub, sub), np.float32)], axis=1)
    tmat = jnp.asarray(np.concatenate([half, half], axis=0), BF16)
    nq = seq // blk
    width = SB_HEADS_PER_STEP // 2 * LANES
    groups = SB_W // width
    q3, k3, v3 = (a.reshape(batch, seq, SB_W) for a in (q, k, v))
    out = pl.pallas_call(
        functools.partial(_sb_kernel, blk=blk),
        grid=(batch, groups, nq),
        in_specs=[pl.BlockSpec((None, blk, width), lambda b, p, i: (b, i, p)),
                  pl.BlockSpec((None, seq, width), lambda b, p, i: (b, 0, p)),
                  pl.BlockSpec((None, seq, width), lambda b, p, i: (b, 0, p)),
                  pl.BlockSpec(tmat.shape, lambda b, p, i: (0, 0))],
        out_specs=pl.BlockSpec((None, blk, width), lambda b, p, i: (b, i, p)),
        out_shape=jax.ShapeDtypeStruct((batch, seq, SB_W), BF16),
        scratch_shapes=[pltpu.VMEM((SB_HEADS_PER_STEP, blk, LANES), F32)] * 2,
        compiler_params=_params(("parallel", "parallel", "arbitrary")),
        name="sb",
    )(q3, k3, v3, tmat)
    return out.reshape(batch * seq, SB_W)


def _memkv_kernel(m_ref, g_ref, wk_ref, wv_ref, k_ref, v_ref):
    hm = _rmsnorm(m_ref[...], g_ref[...]).astype(BF16)
    k_ref[...] = _dot(hm, wk_ref[...]).astype(BF16)
    v_ref[...] = _dot(hm, wv_ref[...]).astype(BF16)


def _mem_kv(mem, g, wk, wv):
    b, m, d = mem.shape
    full = lambda a: pl.BlockSpec(a.shape, lambda i: (0, 0))
    blk = pl.BlockSpec((None, m, d), lambda i: (i, 0, 0))
    return pl.pallas_call(
        _memkv_kernel,
        grid=(b,),
        in_specs=[blk, full(g), full(wk), full(wv)],
        out_specs=[blk, blk],
        out_shape=[jax.ShapeDtypeStruct((b, m, d), BF16)] * 2,
        compiler_params=_params(("parallel",)),
        name="mem_kv",
    )(mem, g, wk, wv)


def _mid_kernel(x_ref, og_ref, os_ref, wog_ref, wos_ref, gq_ref, wmq_ref, km_ref, vm_ref, wmo_ref, gf_ref,
                x2_ref, hf_ref):
    x1 = x_ref[...] + _dot(og_ref[...], wog_ref[...]) + _dot(os_ref[...], wos_ref[...])
    hq = _rmsnorm(x1, gq_ref[...]).astype(BF16)
    d = x1.shape[-1]
    hd = d // MEM_HEADS
    q = (_dot(hq, wmq_ref[...]) * (hd ** -0.5)).astype(BF16)
    outs = []
    for h in range(MEM_HEADS):
        cols = slice(h * hd, (h + 1) * hd)
        s = lax.dot_general(q[:, cols], km_ref[:, cols], NT_DIMS, preferred_element_type=F32)
        e = jnp.exp(s - jnp.max(s, axis=-1, keepdims=True))
        p = (e / jnp.sum(e, axis=-1, keepdims=True)).astype(BF16)
        outs.append(_dot(p, vm_ref[:, cols]).astype(BF16))
    x2 = x1 + _dot(jnp.concatenate(outs, axis=1), wmo_ref[...])
    x2_ref[...] = x2
    hf_ref[...] = _rmsnorm(x2, gf_ref[...])


def _mid(x2d, og, osb, wog, wos, gq, wmq, km, vm, wmo, gf, seq, tm):
    n, d = x2d.shape
    m = km.shape[1]
    per_batch = seq // tm
    full = lambda a: pl.BlockSpec(a.shape, lambda i: (0, 0))
    rows = lambda w: pl.BlockSpec((tm, w), lambda i: (i, 0))
    mem = pl.BlockSpec((None, m, d), lambda i: (i // per_batch, 0, 0))
    return pl.pallas_call(
        _mid_kernel,
        grid=(n // tm,),
        in_specs=[rows(d), rows(V_G), rows(SB_W), full(wog), full(wos), full(gq), full(wmq), mem, mem,
                  full(wmo), full(gf)],
        out_specs=[rows(d), rows(d)],
        out_shape=[jax.ShapeDtypeStruct((n, d), F32)] * 2,
        compiler_params=_params(("parallel",)),
        name="mid",
    )(x2d, og, osb, wog, wos, gq, wmq, km, vm, wmo, gf)


def _router_kernel(hf_ref, wrt_ref, br_ref, cmat_ref, idx_ref, gate_ref, rank_ref, cnt_ref, carry_ref, *, tm):
    @pl.when(pl.program_id(0) == 0)
    def _():
        carry_ref[...] = jnp.zeros_like(carry_ref)

    hh, hl = _split(hf_ref[...])
    wh, wl = _split(wrt_ref[...])
    nt = lambda a, b: lax.dot_general(a, b, NT_DIMS, preferred_element_type=F32)
    vals = nt(wh, hh) + (nt(wh, hl) + nt(wl, hh)) + br_ref[...]
    eidx = lax.broadcasted_iota(jnp.int32, (N_EXPERTS, tm), 0)
    tops, sels, hots = [], [], []
    for _ in range(TOP_K):
        m = jnp.max(vals, axis=0, keepdims=True)
        sel = jnp.min(jnp.where(vals == m, eidx, N_EXPERTS), axis=0, keepdims=True)
        hot = eidx == sel
        vals = jnp.where(hot, -jnp.inf, vals)
        tops.append(m)
        sels.append(sel)
        hots.append(hot)
    exps = [jnp.exp(t - tops[0]) for t in tops]
    denom = exps[0] + exps[1] + exps[2] + exps[3]
    chosen = jnp.zeros((N_EXPERTS, tm), F32)
    for hot in hots:
        chosen = chosen + hot.astype(F32)
    sums = _dot(chosen.astype(BF16), cmat_ref[...])
    before = sums[:, :tm] + carry_ref[...]
    for k in range(TOP_K):
        idx_ref[k:k + 1, :] = sels[k]
        gate_ref[k:k + 1, :] = exps[k] / denom
        rank_ref[k:k + 1, :] = jnp.sum(jnp.where(hots[k], before, 0.0), axis=0, keepdims=True).astype(jnp.int32)
    carry_ref[...] = carry_ref[...] + sums[:, tm:]
    cnt_ref[...] = carry_ref[...]


def _router(hf, wrt, br, tm):
    n, d = hf.shape
    r = np.arange(tm)
    cmat = jnp.asarray(np.concatenate([(r[:, None] < r[None, :]).astype(np.float32),
                                       np.ones((tm, tm), np.float32)], axis=1), BF16)
    full = lambda a: pl.BlockSpec(a.shape, lambda i: (0, 0))
    tok = pl.BlockSpec((TOP_K, tm), lambda i: (0, i))
    return pl.pallas_call(
        functools.partial(_router_kernel, tm=tm),
        grid=(n // tm,),
        in_specs=[pl.BlockSpec((tm, d), lambda i: (i, 0)), full(wrt), full(br), full(cmat)],
        out_specs=[tok, tok, tok, pl.BlockSpec((N_EXPERTS, tm), lambda i: (0, 0))],
        out_shape=[jax.ShapeDtypeStruct((TOP_K, n), jnp.int32), jax.ShapeDtypeStruct((TOP_K, n), F32),
                   jax.ShapeDtypeStruct((TOP_K, n), jnp.int32), jax.ShapeDtypeStruct((N_EXPERTS, tm), F32)],
        scratch_shapes=[pltpu.VMEM((N_EXPERTS, tm), F32)],
        compiler_params=_params(("arbitrary",)),
        name="router",
    )(hf, wrt, br, cmat)


SC_CHUNK = 32


def _sc_workers():
    info = plsc.get_sparse_core_info()
    mesh = plsc.VectorSubcoreMesh(core_axis_name="c", subcore_axis_name="s")
    return info.num_cores, info.num_cores * info.num_subcores, mesh


def _scatter_rows(src, slot, n_out):
    n_cores, n_workers, mesh = _sc_workers()
    n, d = src.shape
    per_worker = n // n_workers
    n_chunks = per_worker // SC_CHUNK
    assert n == n_workers * n_chunks * SC_CHUNK and n_chunks % 2 == 0

    def body(src_hbm, idx_hbm, out_hbm, idx_v, rows_v, lsem, ssem):
        wid = lax.axis_index("s") * n_cores + lax.axis_index("c")
        base = wid * per_worker
        pltpu.sync_copy(idx_hbm.at[wid], idx_v)

        def load(j, b):
            return pltpu.make_async_copy(src_hbm.at[pl.ds(base + j * SC_CHUNK, SC_CHUNK)], rows_v.at[b], lsem.at[b])

        def scatters(j, b):
            return [pltpu.make_async_copy(rows_v.at[b], out_hbm.at[idx_v.at[k, j]], ssem.at[b]) for k in range(TOP_K)]

        load(0, 0).start()

        @pl.loop(0, n_chunks, step=2)
        def _(j0):
            for b in range(2):
                j = j0 + b

                @pl.when(j >= 1)
                def _():
                    for c in scatters(j - 1, 1 - b):
                        c.wait()

                @pl.when(j + 1 < n_chunks)
                def _():
                    load(j + 1, 1 - b).start()

                load(j, b).wait()
                for c in scatters(j, b):
                    c.start()

        for c in scatters(n_chunks - 1, 1):
            c.wait()

    call = pl.kernel(
        body,
        out_type=jax.ShapeDtypeStruct((n_out, d), src.dtype),
        mesh=mesh,
        scratch_types=[pltpu.VMEM((TOP_K, n_chunks, SC_CHUNK), jnp.int32), pltpu.VMEM((2, SC_CHUNK, d), src.dtype),
                       pltpu.SemaphoreType.DMA((2,)), pltpu.SemaphoreType.DMA((2,))],
    )
    idx = slot.reshape(TOP_K, n_workers, n_chunks, SC_CHUNK).transpose(1, 0, 2, 3)
    return call(src, idx)


def _fill_kernel(fill_start_ref, fill_n_ref, nvalid_ref, xs_in, xs_ref, zero_ref, sem, *, blk):
    del xs_in
    n_blocks = xs_ref.shape[0] // blk

    def block_copy(b):
        return pltpu.make_async_copy(zero_ref, xs_ref.at[pl.ds(pl.multiple_of(b * blk, blk), blk), :], sem)

    def pieces(e):
        start, n = fill_start_ref[e], fill_n_ref[e]
        head = (-start) & (TILE_ROWS - 1)
        for r in range(TILE_ROWS - 1):
            copy = pltpu.make_async_copy(zero_ref.at[pl.ds(0, 1), :], xs_ref.at[pl.ds(start + r, 1), :], sem)
            yield r < jnp.minimum(head, n), copy
        aligned, rest = start + head, n - head
        size = blk // 2
        while size >= TILE_ROWS:
            at = pl.multiple_of(aligned + (rest & ~(2 * size - 1)), TILE_ROWS)
            copy = pltpu.make_async_copy(zero_ref.at[pl.ds(0, size), :], xs_ref.at[pl.ds(at, size), :], sem)
            yield (rest & size) != 0, copy
            size //= 2

    zero_ref[...] = jnp.zeros_like(zero_ref)
    for e in range(N_EXPERTS):
        for wanted, copy in pieces(e):
            pl.when(wanted)(copy.start)
    for e in range(N_EXPERTS):
        for wanted, copy in pieces(e):
            pl.when(wanted)(copy.wait)
    lax.fori_loop(nvalid_ref[0], n_blocks, lambda b, _: (block_copy(b).start(), 0)[1], 0)
    lax.fori_loop(nvalid_ref[0], n_blocks, lambda b, _: (block_copy(b).wait(), 0)[1], 0)


def _fill(fill_start, fill_n, n_valid, xs, blk):
    m_pad, d = xs.shape
    hbm = pl.BlockSpec(memory_space=pl.ANY)
    return pl.pallas_call(
        functools.partial(_fill_kernel, blk=blk),
        grid_spec=pltpu.PrefetchScalarGridSpec(
            num_scalar_prefetch=3,
            grid=(1,),
            in_specs=[hbm],
            out_specs=hbm,
            scratch_shapes=[pltpu.VMEM((blk, d), F32), pltpu.SemaphoreType.DMA],
        ),
        out_shape=jax.ShapeDtypeStruct((m_pad, d), F32),
        input_output_aliases={3: 0},
        compiler_params=_params(("arbitrary",)),
        name="fill",
    )(fill_start, fill_n, n_valid, xs)


def _experts_kernel(exp_ref, first_ref, slot_ref, next_ref, nvalid_ref, xs_ref, wgu_hbm, bgu_ref, wd_hbm, bd_ref,
                    ys_ref, wgu_f32, wd_f32, wgu_bf, wd_bf, wsem):
    i = pl.program_id(0)
    valid = i < nvalid_ref[0]
    slot = slot_ref[i]

    def weight_copies(e, s):
        return (pltpu.make_async_copy(wgu_hbm.at[e], wgu_f32.at[s], wsem.at[s]),
                pltpu.make_async_copy(wd_hbm.at[e], wd_f32.at[s], wsem.at[s]))

    @pl.when(jnp.logical_and(valid, first_ref[i] == 1))
    def _():
        @pl.when(i == 0)
        def _():
            for c in weight_copies(exp_ref[0], 0):
                c.start()

        for c in weight_copies(exp_ref[i], slot):
            c.wait()
        wgu_bf[...] = wgu_f32[slot].astype(BF16)
        wd_bf[...] = wd_f32[slot].astype(BF16)

        @pl.when(next_ref[i] >= 0)
        def _():
            for c in weight_copies(next_ref[i], 1 - slot):
                c.start()

    @pl.when(valid)
    def _():
        f = wd_bf.shape[0]
        gu = _dot(xs_ref[...].astype(BF16), wgu_bf[...]) + bgu_ref[...]
        gate = jnp.minimum(gu[:, :f], SWIGLU_LIMIT)
        lin = jnp.clip(gu[:, f:], -SWIGLU_LIMIT, SWIGLU_LIMIT)
        act = (lin + 1.0) * (gate * jax.nn.sigmoid(SWIGLU_ALPHA * gate))
        ys_ref[...] = _dot(act.astype(BF16), wd_bf[...]) + bd_ref[...]

    @pl.when(jnp.logical_not(valid))
    def _():
        ys_ref[...] = jnp.zeros_like(ys_ref)


def _experts(blk_exp, run_first, run_slot, run_next, n_valid, xs, w_gu, b_gu, w_down, b_down, blk):
    m_pad, d = xs.shape
    e, _, f2 = w_gu.shape
    f = f2 // 2
    n_steps = m_pad // blk
    rows = pl.BlockSpec((blk, d), lambda i, *_: (i, 0))
    return pl.pallas_call(
        _experts_kernel,
        grid_spec=pltpu.PrefetchScalarGridSpec(
            num_scalar_prefetch=5,
            grid=(n_steps,),
            in_specs=[rows,
                      pl.BlockSpec(memory_space=pl.ANY),
                      pl.BlockSpec((None, 1, f2), lambda i, x, *_: (x[i], 0, 0)),
                      pl.BlockSpec(memory_space=pl.ANY),
                      pl.BlockSpec((None, 1, d), lambda i, x, *_: (x[i], 0, 0))],
            out_specs=rows,
            scratch_shapes=[pltpu.VMEM((2, d, f2), F32), pltpu.VMEM((2, f, d), F32),
                            pltpu.VMEM((d, f2), BF16), pltpu.VMEM((f, d), BF16),
                            pltpu.SemaphoreType.DMA((2,))],
        ),
        out_shape=jax.ShapeDtypeStruct((m_pad, d), F32),
        compiler_params=_params(("arbitrary",)),
        name="experts",
    )(blk_exp, run_first, run_slot, run_next, n_valid, xs, w_gu, b_gu.reshape(e, 1, f2), w_down,
      b_down.reshape(e, 1, d))


def _gather_rows(table, idx):
    n_cores, n_workers, mesh = _sc_workers()
    n_rows, d = idx.shape[0], table.shape[1]
    per_worker = n_rows // n_workers
    n_chunks = per_worker // SC_CHUNK
    assert n_rows == n_workers * n_chunks * SC_CHUNK and n_chunks % 2 == 0

    def body(table_hbm, idx_hbm, out_hbm, idx_v, rows_v, gsem, wsem):
        wid = lax.axis_index("s") * n_cores + lax.axis_index("c")
        base = wid * per_worker
        pltpu.sync_copy(idx_hbm.at[wid], idx_v)

        def gather(j, b):
            return pltpu.make_async_copy(table_hbm.at[idx_v.at[j]], rows_v.at[b], gsem.at[b])

        def put(j, b):
            return pltpu.make_async_copy(rows_v.at[b], out_hbm.at[pl.ds(base + j * SC_CHUNK, SC_CHUNK)], wsem.at[b])

        gather(0, 0).start()

        @pl.loop(0, n_chunks, step=2)
        def _(j0):
            for b in range(2):
                j = j0 + b

                @pl.when(j >= 1)
                def _():
                    put(j - 1, 1 - b).wait()

                @pl.when(j + 1 < n_chunks)
                def _():
                    gather(j + 1, 1 - b).start()

                gather(j, b).wait()
                put(j, b).start()

        put(n_chunks - 1, 1).wait()

    call = pl.kernel(
        body,
        out_type=jax.ShapeDtypeStruct((n_rows, d), table.dtype),
        mesh=mesh,
        scratch_types=[pltpu.VMEM((n_chunks, SC_CHUNK), jnp.int32), pltpu.VMEM((2, SC_CHUNK, d), table.dtype),
                       pltpu.SemaphoreType.DMA((2,)), pltpu.SemaphoreType.DMA((2,))],
    )
    return call(table, idx.reshape(n_workers, n_chunks, SC_CHUNK))


def _combine_kernel(o0_ref, o1_ref, o2_ref, o3_ref, x2_ref, gate_ref, gfin_ref, y_ref):
    acc = x2_ref[...]
    for k, o_ref in enumerate((o0_ref, o1_ref, o2_ref, o3_ref)):
        acc = acc + gate_ref[:, k:k + 1] * o_ref[...]
    y_ref[...] = _rmsnorm(acc, gfin_ref[...])


def _combine(rows, x2, gates, gfin, tm):
    n, d = x2.shape
    per_k = n // tm
    planes = [pl.BlockSpec((tm, d), functools.partial(lambda i, k: (k * per_k + i, 0), k=k)) for k in range(TOP_K)]
    return pl.pallas_call(
        _combine_kernel,
        grid=(per_k,),
        in_specs=planes + [pl.BlockSpec((tm, d), lambda i: (i, 0)),
                           pl.BlockSpec((tm, TOP_K), lambda i: (i, 0)),
                           pl.BlockSpec(gfin.shape, lambda i: (0, 0))],
        out_specs=pl.BlockSpec((tm, d), lambda i: (i, 0)),
        out_shape=jax.ShapeDtypeStruct((n, d), F32),
        compiler_params=_params(("parallel",)),
        name="combine",
    )(rows, rows, rows, rows, x2, gates, gfin)


def _pick(n, pref):
    t = min(pref, n)
    while n % t:
        t //= 2
    return t


def _layer(x2d, mem, batch, seq, g_mix, w_in, w_gla_gate, b_gla_gate, g_gla_head, w_out, g_mem_q, g_mem_kv,
           w_mq, w_mk, w_mv, w_mo, g_ffn, w_router, b_router, w_gu, b_gu, w_down, b_down, g_out):
    n, d = x2d.shape
    row = lambda v: v.reshape(1, -1).astype(F32)

    o_qg, o_kg, o_vg, o_lr, o_rg, o_qs, o_ks, o_vs = np.cumsum((0, Q_G, Q_G, V_G, GLA_GATE_RANK, V_G, SB_W, SB_W))
    lr_pad = jnp.zeros((d, LANES - GLA_GATE_RANK), w_in.dtype)
    wg = jnp.concatenate([w_in[:, o_qg:o_vg + V_G], w_in[:, o_rg:o_rg + V_G],
                          w_in[:, o_lr:o_lr + GLA_GATE_RANK], lr_pad], axis=1).astype(BF16)
    wq, wk, wv = (w_in[:, o:o + SB_W].astype(BF16) for o in (o_qs, o_ks, o_vs))
    wgate = jnp.concatenate([w_gla_gate, jnp.zeros((LANES - GLA_GATE_RANK, Q_G), F32)], axis=0)

    slab, qs, ks, vs = _in_proj(x2d, row(g_mix), wg, wq, wk, wv, _pick(n, 512))
    o_gla = _gla(slab, wgate, row(b_gla_gate), row(g_gla_head), batch, seq, _pick(seq, 512))
    o_sb = _sb(qs, ks, vs, batch, seq, 2 * LANES)
    km, vm = _mem_kv(mem, row(g_mem_kv), w_mk.astype(BF16), w_mv.astype(BF16))
    w_out_bf = w_out.astype(BF16)
    x2, hf = _mid(x2d, o_gla, o_sb, w_out_bf[:V_G], w_out_bf[V_G:], row(g_mem_q), w_mq.astype(BF16), km, vm,
                  w_mo.astype(BF16), row(g_ffn), seq, _pick(seq, 512))

    tm_r = _pick(n, 512)
    idx, gates, rank, cnt = _router(hf, w_router.T, b_router.reshape(-1, 1), tm_r)

    blk = 512
    counts = cnt[:, 0].astype(jnp.int32)
    padded = ((counts + blk - 1) // blk) * blk
    pad_end = jnp.cumsum(padded)
    pad_start = pad_end - padded
    first = jnp.sum(jnp.where(idx[..., None] == jnp.arange(N_EXPERTS), pad_start, 0), axis=-1)
    slot = first + rank
    n_steps = -(-(n * TOP_K + N_EXPERTS * (blk - 1)) // blk)
    m_pad = n_steps * blk
    n_valid = pad_end[-1] // blk
    blk_idx = jnp.arange(n_steps, dtype=jnp.int32)
    used = jnp.minimum(blk_idx, n_valid - 1)
    blk_exp = jnp.minimum(jnp.sum(used[:, None] * blk >= pad_end[None, :], axis=1), N_EXPERTS - 1).astype(jnp.int32)
    run_first = jnp.logical_and(blk_idx < n_valid, blk_exp != jnp.concatenate([blk_exp[:1] - 1, blk_exp[:-1]]))
    run_first = run_first.astype(jnp.int32)
    run_slot = (jnp.cumsum(run_first) - 1) % 2
    e_ids = jnp.arange(N_EXPERTS, dtype=jnp.int32)
    later = jnp.where(jnp.logical_and(padded[None, :] > 0, e_ids[None, :] > e_ids[:, None]), e_ids[None, :], N_EXPERTS)
    following = jnp.min(later, axis=1)
    run_next = jnp.where(following == N_EXPERTS, -1, following)[blk_exp]
    n_valid = n_valid.reshape(1)

    xs = _fill(pad_start + counts, padded - counts, n_valid, _scatter_rows(hf, slot, m_pad), blk)
    ys = _experts(blk_exp, run_first, run_slot.astype(jnp.int32), run_next.astype(jnp.int32), n_valid, xs,
                  w_gu, b_gu, w_down, b_down, blk)
    rows = _gather_rows(ys, slot.reshape(-1))
    return _combine(rows, x2, gates.T, row(g_out), _pick(n, 256))


def kernel(x, mem, g_mix, w_in, w_gla_gate, b_gla_gate, g_gla_head, w_out, g_mem_q, g_mem_kv, w_mq, w_mk, w_mv,
           w_mo, g_ffn, w_router, b_router, w_gu, b_gu, w_down, b_down, g_final):
    batch, seq, d = x.shape
    depth = g_mix.shape[0]
    assert depth == 1, "the final rmsnorm is fused into the single layer's combine step"
    y = _layer(x.reshape(batch * seq, d), mem, batch, seq, g_mix[0], w_in[0], w_gla_gate[0], b_gla_gate[0],
               g_gla_head[0], w_out[0], g_mem_q[0], g_mem_kv[0], w_mq[0], w_mk[0], w_mv[0], w_mo[0], g_ffn[0],
               w_router[0], b_router[0], w_gu[0], b_gu[0], w_down[0], b_down[0], g_final)
    return y.reshape(batch, seq, d)
```

```python
import functools

import jax
import jax.numpy as jnp
import numpy as np
from jax import lax
from jax.experimental import pallas as pl
from jax.experimental.pallas import tpu as pltpu
from jax.experimental.pallas import tpu_sc as plsc

F32 = jnp.float32
BF16 = jnp.bfloat16

EPS = 1e-5
CHUNK = 64
GLA_HEADS = 4
GLA_DK = 64
GLA_DV = 128
GLA_GATE_RANK = 16
GLA_TAU = 16.0
SB_HEADS = 8
SB_HD = 64
MEM_HEADS = 4
N_EXPERTS = 32
TOP_K = 4
SWIGLU_LIMIT = 7.0
SWIGLU_ALPHA = 1.702
SB_LOG_UNDERFLOW = -104.0
SB_HEADS_PER_STEP = 8

LANES = 128
Q_G = GLA_HEADS * GLA_DK
V_G = GLA_HEADS * GLA_DV
SB_W = SB_HEADS * SB_HD
GLA_SLAB = Q_G + Q_G + V_G + V_G + LANES

VMEM_LIMIT = 56 * 1024 * 1024
TILE_ROWS = 8

NT_DIMS = (((1,), (1,)), ((), ()))
TN_DIMS = (((0,), (0,)), ((), ()))


def _dot(a, b):
    return jnp.dot(a, b, preferred_element_type=F32)


def _split(x):
    hi = x.astype(BF16)
    lo = (x - hi.astype(F32)).astype(BF16)
    return hi, lo


def _dot_exact_lhs(a_bf16, b_f32):
    hi, lo = _split(b_f32)
    return _dot(a_bf16, hi) + _dot(a_bf16, lo)


def _dot3(a, b):
    ah, al = _split(a)
    bh, bl = _split(b)
    return _dot(ah, bh) + (_dot(ah, bl) + _dot(al, bh))


def _rmsnorm(x, g):
    return x * lax.rsqrt(jnp.mean(x * x, axis=-1, keepdims=True) + EPS) * g


def _softplus(z):
    return jnp.maximum(z, 0.0) + jnp.log(1.0 + jnp.exp(-jnp.abs(z)))


def _params(sem, vmem=VMEM_LIMIT):
    return pltpu.CompilerParams(dimension_semantics=sem, vmem_limit_bytes=vmem)


def _inproj_kernel(x_ref, g_ref, wg_ref, wq_ref, wk_ref, wv_ref, slab_ref, q_ref, k_ref, v_ref):
    h = _rmsnorm(x_ref[...], g_ref[...]).astype(BF16)
    slab_ref[...] = _dot(h, wg_ref[...])
    q_ref[...] = (_dot(h, wq_ref[...]) * (SB_HD ** -0.5)).astype(BF16)
    k_ref[...] = _dot(h, wk_ref[...]).astype(BF16)
    v_ref[...] = _dot(h, wv_ref[...]).astype(BF16)


def _in_proj(x2d, g, wg, wq, wk, wv, tm):
    n, d = x2d.shape
    full = lambda a: pl.BlockSpec(a.shape, lambda i: (0, 0))
    return pl.pallas_call(
        _inproj_kernel,
        grid=(n // tm,),
        in_specs=[pl.BlockSpec((tm, d), lambda i: (i, 0)), full(g), full(wg), full(wq), full(wk), full(wv)],
        out_specs=[pl.BlockSpec((tm, GLA_SLAB), lambda i: (i, 0))] + [pl.BlockSpec((tm, SB_W), lambda i: (i, 0))] * 3,
        out_shape=[jax.ShapeDtypeStruct((n, GLA_SLAB), F32)] + [jax.ShapeDtypeStruct((n, SB_W), BF16)] * 3,
        compiler_params=_params(("parallel",)),
        name="in_proj",
    )(x2d, g, wg, wq, wk, wv)


def _gla_kernel(slab_ref, wgate_ref, bgate_ref, ghead_ref, umat_ref, o_ref, state_ref, *, n_chunks):
    @pl.when(pl.program_id(1) == 0)
    def _():
        state_ref[...] = jnp.zeros_like(state_ref)

    qg = slab_ref[:, 0:Q_G] * (GLA_DK ** -0.5)
    kg = slab_ref[:, Q_G:2 * Q_G]
    glr = slab_ref[:, 2 * Q_G + 2 * V_G:GLA_SLAB]
    log_a = -_softplus(-(_dot3(glr, wgate_ref[...]) + bgate_ref[...])) * (1.0 / GLA_TAU)
    to_end = _dot_exact_lhs(umat_ref[...], log_a)
    kdec = kg * jnp.exp(to_end)
    g_chunk = to_end + log_a

    lane = lax.broadcasted_iota(jnp.int32, (CHUNK, LANES), 1)
    ghead = ghead_ref[...]
    for c in range(n_chunks):
        rows = slice(c * CHUNK, (c + 1) * CHUNK)
        for h in range(GLA_HEADS):
            pair = slice((h // 2) * LANES, (h // 2 + 1) * LANES)
            mine = (lane >= GLA_DK) if h % 2 else (lane < GLA_DK)
            kd = jnp.where(mine, kdec[rows, pair], 0.0).astype(BF16)
            qm = jnp.where(mine, qg[rows, pair], 0.0).astype(BF16)
            vh = slab_ref[rows, 2 * Q_G + h * GLA_DV:2 * Q_G + (h + 1) * GLA_DV].astype(BF16)
            decay = jnp.exp(g_chunk[c * CHUNK:c * CHUNK + 1, pair])
            st = decay * state_ref[h] + lax.dot_general(vh, kd, TN_DIMS, preferred_element_type=F32)
            state_ref[h] = st
            o = lax.dot_general(qm, st.astype(BF16), NT_DIMS, preferred_element_type=F32)
            rg = slab_ref[rows, 2 * Q_G + V_G + h * GLA_DV:2 * Q_G + V_G + (h + 1) * GLA_DV]
            o = _rmsnorm(o, ghead) * (rg * jax.nn.sigmoid(rg))
            o_ref[rows, h * GLA_DV:(h + 1) * GLA_DV] = o.astype(BF16)


def _gla(slab, wgate, bgate, ghead, batch, seq, ts):
    n = slab.shape[0]
    n_chunks = ts // CHUNK
    r = np.arange(ts)
    umat = jnp.asarray((r[None, :] > r[:, None]) & (r[None, :] // CHUNK == r[:, None] // CHUNK), BF16)
    full = lambda a: pl.BlockSpec(a.shape, lambda b, i: (0,) * a.ndim)
    steps = seq // ts
    return pl.pallas_call(
        functools.partial(_gla_kernel, n_chunks=n_chunks),
        grid=(batch, steps),
        in_specs=[pl.BlockSpec((ts, GLA_SLAB), lambda b, i: (b * steps + i, 0)),
                  full(wgate), full(bgate), full(ghead), full(umat)],
        out_specs=pl.BlockSpec((ts, V_G), lambda b, i: (b * steps + i, 0)),
        out_shape=jax.ShapeDtypeStruct((n, V_G), BF16),
        scratch_shapes=[pltpu.VMEM((GLA_HEADS, GLA_DV, LANES), F32)],
        compiler_params=_params(("parallel", "arbitrary")),
        name="gla",
    )(slab, wgate, bgate, ghead, umat)


def _sb_kernel(q_ref, k_ref, v_ref, tmat_ref, o_ref, acc_ref, carry_ref, *, blk):
    i = pl.program_id(2)
    sub = blk // 2
    lane = lax.broadcasted_iota(jnp.int32, (blk, LANES), 1)
    causal = lax.broadcasted_iota(jnp.int32, (blk, blk), 1) < lax.broadcasted_iota(jnp.int32, (blk, blk), 0)
    lo_head = lane < SB_HD
    n_heads = 2 * q_ref.shape[-1] // LANES
    q_heads = []
    for p in range(n_heads // 2):
        q = q_ref[:, p * LANES:(p + 1) * LANES]
        q_heads += [jnp.where(lo_head, q, jnp.zeros_like(q)), jnp.where(lo_head, jnp.zeros_like(q), q)]
    tmat = tmat_ref[...]

    def chunk(j, diag):
        start = pl.multiple_of(j * blk, blk)
        worst = None
        for h in range(n_heads):
            slab = slice((h // 2) * LANES, (h // 2 + 1) * LANES)
            kj = k_ref[pl.ds(start, blk), slab]
            vj = v_ref[pl.ds(start, blk), slab]
            z = lax.dot_general(q_heads[h], kj, NT_DIMS, preferred_element_type=F32)
            sp = _softplus(z)
            log1mb = -sp
            if diag:
                log1mb = jnp.where(causal, log1mb, 0.0)
            hi, lo = _split(log1mb)
            right = _dot(jnp.concatenate([hi[:, sub:], lo[:, sub:]], axis=1), tmat)
            left = _dot(jnp.concatenate([hi[:, :sub], lo[:, :sub]], axis=1), tmat)
            log_sig = z - sp
            if diag:
                after_right = right[:, sub:]
                log_a_right = log_sig[:, sub:] + right[:, :sub]
            else:
                carry = carry_ref[h]
                after_right = carry + right[:, sub:]
                log_a_right = log_sig[:, sub:] + right[:, :sub] + carry
            log_a_left = log_sig[:, :sub] + left[:, :sub] + after_right
            a = jnp.exp(jnp.concatenate([log_a_left, log_a_right], axis=1))
            if diag:
                a = jnp.where(causal, a, 0.0)
            pv = _dot(a.astype(BF16), vj)
            acc_ref[h] = pv if diag else acc_ref[h] + pv
            remaining = after_right + left[:, sub:]
            carry_ref[h] = remaining
            top = jnp.max(remaining)
            worst = top if worst is None else jnp.maximum(worst, top)
        return worst

    def more(state):
        return jnp.logical_and(state[0] >= 0, state[1] > SB_LOG_UNDERFLOW)

    def first_two():
        chunk(i, True)
        return chunk(i - 1, False)

    start = lax.cond(i >= 1, first_two, lambda: chunk(i, True))
    lax.while_loop(more, lambda state: (state[0] - 1, chunk(state[0], False)), (i - 2, start))
    for p in range(n_heads // 2):
        o_ref[:, p * LANES:(p + 1) * LANES] = jnp.where(lo_head, acc_ref[2 * p], acc_ref[2 * p + 1]).astype(BF16)


def _sb(q, k, v, batch, seq, blk):
    sub = blk // 2
    assert sub == LANES
    r = np.arange(sub)
    tri = (r[:, None] > r[None, :]).astype(np.float32)
    half = np.concatenate([tri, np.ones((sub, sub), np.float32)], axis=1)
    tmat = jnp.asarray(np.concatenate([half, half], axis=0), BF16)
    nq = seq // blk
    width = SB_HEADS_PER_STEP // 2 * LANES
    groups = SB_W // width
    q3, k3, v3 = (a.reshape(batch, seq, SB_W) for a in (q, k, v))
    out = pl.pallas_call(
        functools.partial(_sb_kernel, blk=blk),
        grid=(batch, groups, nq),
        in_specs=[pl.BlockSpec((None, blk, width), lambda b, p, i: (b, i, p)),
                  pl.BlockSpec((None, seq, width), lambda b, p, i: (b, 0, p)),
                  pl.BlockSpec((None, seq, width), lambda b, p, i: (b, 0, p)),
                  pl.BlockSpec(tmat.shape, lambda b, p, i: (0, 0))],
        out_specs=pl.BlockSpec((None, blk, width), lambda b, p, i: (b, i, p)),
        out_shape=jax.ShapeDtypeStruct((batch, seq, SB_W), BF16),
        scratch_shapes=[pltpu.VMEM((SB_HEADS_PER_STEP, blk, LANES), F32)] * 2,
        compiler_params=_params(("parallel", "parallel", "arbitrary")),
        name="sb",
    )(q3, k3, v3, tmat)
    return out.reshape(batch * seq, SB_W)


def _memkv_kernel(m_ref, g_ref, wk_ref, wv_ref, k_ref, v_ref):
    hm = _rmsnorm(m_ref[...], g_ref[...]).astype(BF16)
    k_ref[...] = _dot(hm, wk_ref[...]).astype(BF16)
    v_ref[...] = _dot(hm, wv_ref[...]).astype(BF16)


def _mem_kv(mem, g, wk, wv):
    b, m, d = mem.shape
    full = lambda a: pl.BlockSpec(a.shape, lambda i: (0, 0))
    blk = pl.BlockSpec((None, m, d), lambda i: (i, 0, 0))
    return pl.pallas_call(
        _memkv_kernel,
        grid=(b,),
        in_specs=[blk, full(g), full(wk), full(wv)],
        out_specs=[blk, blk],
        out_shape=[jax.ShapeDtypeStruct((b, m, d), BF16)] * 2,
        compiler_params=_params(("parallel",)),
        name="mem_kv",
    )(mem, g, wk, wv)


def _mid_kernel(x_ref, og_ref, os_ref, wog_ref, wos_ref, gq_ref, wmq_ref, km_ref, vm_ref, wmo_ref, gf_ref,
                x2_ref, hf_ref):
    x1 = x_ref[...] + _dot(og_ref[...], wog_ref[...]) + _dot(os_ref[...], wos_ref[...])
    hq = _rmsnorm(x1, gq_ref[...]).astype(BF16)
    d = x1.shape[-1]
    hd = d // MEM_HEADS
    q = (_dot(hq, wmq_ref[...]) * (hd ** -0.5)).astype(BF16)
    outs = []
    for h in range(MEM_HEADS):
        cols = slice(h * hd, (h + 1) * hd)
        s = lax.dot_general(q[:, cols], km_ref[:, cols], NT_DIMS, preferred_element_type=F32)
        e = jnp.exp(s - jnp.max(s, axis=-1, keepdims=True))
        p = (e / jnp.sum(e, axis=-1, keepdims=True)).astype(BF16)
        outs.append(_dot(p, vm_ref[:, cols]).astype(BF16))
    x2 = x1 + _dot(jnp.concatenate(outs, axis=1), wmo_ref[...])
    x2_ref[...] = x2
    hf_ref[...] = _rmsnorm(x2, gf_ref[...])


def _mid(x2d, og, osb, wog, wos, gq, wmq, km, vm, wmo, gf, seq, tm):
    n, d = x2d.shape
    m = km.shape[1]
    per_batch = seq // tm
    full = lambda a: pl.BlockSpec(a.shape, lambda i: (0, 0))
    rows = lambda w: pl.BlockSpec((tm, w), lambda i: (i, 0))
    mem = pl.BlockSpec((None, m, d), lambda i: (i // per_batch, 0, 0))
    return pl.pallas_call(
        _mid_kernel,
        grid=(n // tm,),
        in_specs=[rows(d), rows(V_G), rows(SB_W), full(wog), full(wos), full(gq), full(wmq), mem, mem,
                  full(wmo), full(gf)],
        out_specs=[rows(d), rows(d)],
        out_shape=[jax.ShapeDtypeStruct((n, d), F32)] * 2,
        compiler_params=_params(("parallel",)),
        name="mid",
    )(x2d, og, osb, wog, wos, gq, wmq, km, vm, wmo, gf)


def _router_kernel(hf_ref, wrt_ref, br_ref, cmat_ref, idx_ref, gate_ref, rank_ref, cnt_ref, carry_ref, *, tm):
    @pl.when(pl.program_id(0) == 0)
    def _():
        carry_ref[...] = jnp.zeros_like(carry_ref)

    hh, hl = _split(hf_ref[...])
    wh, wl = _split(wrt_ref[...])
    nt = lambda a, b: lax.dot_general(a, b, NT_DIMS, preferred_element_type=F32)
    vals = nt(wh, hh) + (nt(wh, hl) + nt(wl, hh)) + br_ref[...]
    eidx = lax.broadcasted_iota(jnp.int32, (N_EXPERTS, tm), 0)
    tops, sels, hots = [], [], []
    for _ in range(TOP_K):
        m = jnp.max(vals, axis=0, keepdims=True)
        sel = jnp.min(jnp.where(vals == m, eidx, N_EXPERTS), axis=0, keepdims=True)
        hot = eidx == sel
        vals = jnp.where(hot, -jnp.inf, vals)
        tops.append(m)
        sels.append(sel)
        hots.append(hot)
    exps = [jnp.exp(t - tops[0]) for t in tops]
    denom = exps[0] + exps[1] + exps[2] + exps[3]
    chosen = jnp.zeros((N_EXPERTS, tm), F32)
    for hot in hots:
        chosen = chosen + hot.astype(F32)
    sums = _dot(chosen.astype(BF16), cmat_ref[...])
    before = sums[:, :tm] + carry_ref[...]
    for k in range(TOP_K):
        idx_ref[k:k + 1, :] = sels[k]
        gate_ref[k:k + 1, :] = exps[k] / denom
        rank_ref[k:k + 1, :] = jnp.sum(jnp.where(hots[k], before, 0.0), axis=0, keepdims=True).astype(jnp.int32)
    carry_ref[...] = carry_ref[...] + sums[:, tm:]
    cnt_ref[...] = carry_ref[...]


def _router(hf, wrt, br, tm):
    n, d = hf.shape
    r = np.arange(tm)
    cmat = jnp.asarray(np.concatenate([(r[:, None] < r[None, :]).astype(np.float32),
                                       np.ones((tm, tm), np.float32)], axis=1), BF16)
    full = lambda a: pl.BlockSpec(a.shape, lambda i: (0, 0))
    tok = pl.BlockSpec((TOP_K, tm), lambda i: (0, i))
    return pl.pallas_call(
        functools.partial(_router_kernel, tm=tm),
        grid=(n // tm,),
        in_specs=[pl.BlockSpec((tm, d), lambda i: (i, 0)), full(wrt), full(br), full(cmat)],
        out_specs=[tok, tok, tok, pl.BlockSpec((N_EXPERTS, tm), lambda i: (0, 0))],
        out_shape=[jax.ShapeDtypeStruct((TOP_K, n), jnp.int32), jax.ShapeDtypeStruct((TOP_K, n), F32),
                   jax.ShapeDtypeStruct((TOP_K, n), jnp.int32), jax.ShapeDtypeStruct((N_EXPERTS, tm), F32)],
        scratch_shapes=[pltpu.VMEM((N_EXPERTS, tm), F32)],
        compiler_params=_params(("arbitrary",)),
        name="router",
    )(hf, wrt, br, cmat)


SC_CHUNK = 32


def _sc_workers():
    info = plsc.get_sparse_core_info()
    mesh = plsc.VectorSubcoreMesh(core_axis_name="c", subcore_axis_name="s")
    return info.num_cores, info.num_cores * info.num_subcores, mesh


def _scatter_rows(src, slot, n_out):
    n_cores, n_workers, mesh = _sc_workers()
    n, d = src.shape
    per_worker = n // n_workers
    n_chunks = per_worker // SC_CHUNK
    assert n == n_workers * n_chunks * SC_CHUNK and n_chunks % 2 == 0

    def body(src_hbm, idx_hbm, out_hbm, idx_v, rows_v, lsem, ssem):
        wid = lax.axis_index("s") * n_cores + lax.axis_index("c")
        base = wid * per_worker
        pltpu.sync_copy(idx_hbm.at[wid], idx_v)

        def load(j, b):
            return pltpu.make_async_copy(src_hbm.at[pl.ds(base + j * SC_CHUNK, SC_CHUNK)], rows_v.at[b], lsem.at[b])

        def scatters(j, b):
            return [pltpu.make_async_copy(rows_v.at[b], out_hbm.at[idx_v.at[k, j]], ssem.at[b]) for k in range(TOP_K)]

        load(0, 0).start()

        @pl.loop(0, n_chunks, step=2)
        def _(j0):
            for b in range(2):
                j = j0 + b

                @pl.when(j >= 1)
                def _():
                    for c in scatters(j - 1, 1 - b):
                        c.wait()

                @pl.when(j + 1 < n_chunks)
                def _():
                    load(j + 1, 1 - b).start()

                load(j, b).wait()
                for c in scatters(j, b):
                    c.start()

        for c in scatters(n_chunks - 1, 1):
            c.wait()

    call = pl.kernel(
        body,
        out_type=jax.ShapeDtypeStruct((n_out, d), src.dtype),
        mesh=mesh,
        scratch_types=[pltpu.VMEM((TOP_K, n_chunks, SC_CHUNK), jnp.int32), pltpu.VMEM((2, SC_CHUNK, d), src.dtype),
                       pltpu.SemaphoreType.DMA((2,)), pltpu.SemaphoreType.DMA((2,))],
    )
    idx = slot.reshape(TOP_K, n_workers, n_chunks, SC_CHUNK).transpose(1, 0, 2, 3)
    return call(src, idx)


def _fill_kernel(fill_start_ref, fill_n_ref, nvalid_ref, xs_in, xs_ref, zero_ref, sem, *, blk):
    del xs_in
    n_blocks = xs_ref.shape[0] // blk

    def block_copy(b):
        return pltpu.make_async_copy(zero_ref, xs_ref.at[pl.ds(pl.multiple_of(b * blk, blk), blk), :], sem)

    def pieces(e):
        start, n = fill_start_ref[e], fill_n_ref[e]
        head = (-start) & (TILE_ROWS - 1)
        for r in range(TILE_ROWS - 1):
            copy = pltpu.make_async_copy(zero_ref.at[pl.ds(0, 1), :], xs_ref.at[pl.ds(start + r, 1), :], sem)
            yield r < jnp.minimum(head, n), copy
        aligned, rest = start + head, n - head
        size = blk // 2
        while size >= TILE_ROWS:
            at = pl.multiple_of(aligned + (rest & ~(2 * size - 1)), TILE_ROWS)
            copy = pltpu.make_async_copy(zero_ref.at[pl.ds(0, size), :], xs_ref.at[pl.ds(at, size), :], sem)
            yield (rest & size) != 0, copy
            size //= 2

    zero_ref[...] = jnp.zeros_like(zero_ref)
    for e in range(N_EXPERTS):
        for wanted, copy in pieces(e):
            pl.when(wanted)(copy.start)
    for e in range(N_EXPERTS):
        for wanted, copy in pieces(e):
            pl.when(wanted)(copy.wait)
    lax.fori_loop(nvalid_ref[0], n_blocks, lambda b, _: (block_copy(b).start(), 0)[1], 0)
    lax.fori_loop(nvalid_ref[0], n_blocks, lambda b, _: (block_copy(b).wait(), 0)[1], 0)


def _fill(fill_start, fill_n, n_valid, xs, blk):
    m_pad, d = xs.shape
    hbm = pl.BlockSpec(memory_space=pl.ANY)
    return pl.pallas_call(
        functools.partial(_fill_kernel, blk=blk),
        grid_spec=pltpu.PrefetchScalarGridSpec(
            num_scalar_prefetch=3,
            grid=(1,),
            in_specs=[hbm],
            out_specs=hbm,
            scratch_shapes=[pltpu.VMEM((blk, d), F32), pltpu.SemaphoreType.DMA],
        ),
        out_shape=jax.ShapeDtypeStruct((m_pad, d), F32),
        input_output_aliases={3: 0},
        compiler_params=_params(("arbitrary",)),
        name="fill",
    )(fill_start, fill_n, n_valid, xs)


def _experts_kernel(exp_ref, first_ref, slot_ref, next_ref, nvalid_ref, xs_ref, wgu_hbm, bgu_ref, wd_hbm, bd_ref,
                    ys_ref, wgu_f32, wd_f32, wgu_bf, wd_bf, wsem):
    i = pl.program_id(0)
    valid = i < nvalid_ref[0]
    slot = slot_ref[i]

    def weight_copies(e, s):
        return (pltpu.make_async_copy(wgu_hbm.at[e], wgu_f32.at[s], wsem.at[s]),
                pltpu.make_async_copy(wd_hbm.at[e], wd_f32.at[s], wsem.at[s]))

    @pl.when(jnp.logical_and(valid, first_ref[i] == 1))
    def _():
        @pl.when(i == 0)
        def _():
            for c in weight_copies(exp_ref[0], 0):
                c.start()

        for c in weight_copies(exp_ref[i], slot):
            c.wait()
        wgu_bf[...] = wgu_f32[slot].astype(BF16)
        wd_bf[...] = wd_f32[slot].astype(BF16)

        @pl.when(next_ref[i] >= 0)
        def _():
            for c in weight_copies(next_ref[i], 1 - slot):
                c.start()

    @pl.when(valid)
    def _():
        f = wd_bf.shape[0]
        gu = _dot(xs_ref[...].astype(BF16), wgu_bf[...]) + bgu_ref[...]
        gate = jnp.minimum(gu[:, :f], SWIGLU_LIMIT)
        lin = jnp.clip(gu[:, f:], -SWIGLU_LIMIT, SWIGLU_LIMIT)
        act = (lin + 1.0) * (gate * jax.nn.sigmoid(SWIGLU_ALPHA * gate))
        ys_ref[...] = _dot(act.astype(BF16), wd_bf[...]) + bd_ref[...]

    @pl.when(jnp.logical_not(valid))
    def _():
        ys_ref[...] = jnp.zeros_like(ys_ref)


def _experts(blk_exp, run_first, run_slot, run_next, n_valid, xs, w_gu, b_gu, w_down, b_down, blk):
    m_pad, d = xs.shape
    e, _, f2 = w_gu.shape
    f = f2 // 2
    n_steps = m_pad // blk
    rows = pl.BlockSpec((blk, d), lambda i, *_: (i, 0))
    return pl.pallas_call(
        _experts_kernel,
        grid_spec=pltpu.PrefetchScalarGridSpec(
            num_scalar_prefetch=5,
            grid=(n_steps,),
            in_specs=[rows,
                      pl.BlockSpec(memory_space=pl.ANY),
                      pl.BlockSpec((None, 1, f2), lambda i, x, *_: (x[i], 0, 0)),
                      pl.BlockSpec(memory_space=pl.ANY),
                      pl.BlockSpec((None, 1, d), lambda i, x, *_: (x[i], 0, 0))],
            out_specs=rows,
            scratch_shapes=[pltpu.VMEM((2, d, f2), F32), pltpu.VMEM((2, f, d), F32),
                            pltpu.VMEM((d, f2), BF16), pltpu.VMEM((f, d), BF16),
                            pltpu.SemaphoreType.DMA((2,))],
        ),
        out_shape=jax.ShapeDtypeStruct((m_pad, d), F32),
        compiler_params=_params(("arbitrary",)),
        name="experts",
    )(blk_exp, run_first, run_slot, run_next, n_valid, xs, w_gu, b_gu.reshape(e, 1, f2), w_down,
      b_down.reshape(e, 1, d))


def _gather_rows(table, idx):
    n_cores, n_workers, mesh = _sc_workers()
    n_rows, d = idx.shape[0], table.shape[1]
    per_worker = n_rows // n_workers
    n_chunks = per_worker // SC_CHUNK
    assert n_rows == n_workers * n_chunks * SC_CHUNK and n_chunks % 2 == 0

    def body(table_hbm, idx_hbm, out_hbm, idx_v, rows_v, gsem, wsem):
        wid = lax.axis_index("s") * n_cores + lax.axis_index("c")
        base = wid * per_worker
        pltpu.sync_copy(idx_hbm.at[wid], idx_v)

        def gather(j, b):
            return pltpu.make_async_copy(table_hbm.at[idx_v.at[j]], rows_v.at[b], gsem.at[b])

        def put(j, b):
            return pltpu.make_async_copy(rows_v.at[b], out_hbm.at[pl.ds(base + j * SC_CHUNK, SC_CHUNK)], wsem.at[b])

        gather(0, 0).start()

        @pl.loop(0, n_chunks, step=2)
        def _(j0):
            for b in range(2):
                j = j0 + b

                @pl.when(j >= 1)
                def _():
                    put(j - 1, 1 - b).wait()

                @pl.when(j + 1 < n_chunks)
                def _():
                    gather(j + 1, 1 - b).start()

                gather(j, b).wait()
                put(j, b).start()

        put(n_chunks - 1, 1).wait()

    call = pl.kernel(
        body,
        out_type=jax.ShapeDtypeStruct((n_rows, d), table.dtype),
        mesh=mesh,
        scratch_types=[pltpu.VMEM((n_chunks, SC_CHUNK), jnp.int32), pltpu.VMEM((2, SC_CHUNK, d), table.dtype),
                       pltpu.SemaphoreType.DMA((2,)), pltpu.SemaphoreType.DMA((2,))],
    )
    return call(table, idx.reshape(n_workers, n_chunks, SC_CHUNK))


def _combine_kernel(o0_ref, o1_ref, o2_ref, o3_ref, x2_ref, gate_ref, gfin_ref, y_ref):
    acc = x2_ref[...]
    for k, o_ref in enumerate((o0_ref, o1_ref, o2_ref, o3_ref)):
        acc = acc + gate_ref[:, k:k + 1] * o_ref[...]
    y_ref[...] = _rmsnorm(acc, gfin_ref[...])


def _combine(rows, x2, gates, gfin, tm):
    n, d = x2.shape
    per_k = n // tm
    planes = [pl.BlockSpec((tm, d), functools.partial(lambda i, k: (k * per_k + i, 0), k=k)) for k in range(TOP_K)]
    return pl.pallas_call(
        _combine_kernel,
        grid=(per_k,),
        in_specs=planes + [pl.BlockSpec((tm, d), lambda i: (i, 0)),
                           pl.BlockSpec((tm, TOP_K), lambda i: (i, 0)),
                           pl.BlockSpec(gfin.shape, lambda i: (0, 0))],
        out_specs=pl.BlockSpec((tm, d), lambda i: (i, 0)),
        out_shape=jax.ShapeDtypeStruct((n, d), F32),
        compiler_params=_params(("parallel",)),
        name="combine",
    )(rows, rows, rows, rows, x2, gates, gfin)


def _pick(n, pref):
    t = min(pref, n)
    while n % t:
        t //= 2
    return t


def _layer(x2d, mem, batch, seq, g_mix, w_in, w_gla_gate, b_gla_gate, g_gla_head, w_out, g_mem_q, g_mem_kv,
           w_mq, w_mk, w_mv, w_mo, g_ffn, w_router, b_router, w_gu, b_gu, w_down, b_down, g_out):
    n, d = x2d.shape
    row = lambda v: v.reshape(1, -1).astype(F32)

    o_qg, o_kg, o_vg, o_lr, o_rg, o_qs, o_ks, o_vs = np.cumsum((0, Q_G, Q_G, V_G, GLA_GATE_RANK, V_G, SB_W, SB_W))
    lr_pad = jnp.zeros((d, LANES - GLA_GATE_RANK), w_in.dtype)
    wg = jnp.concatenate([w_in[:, o_qg:o_vg + V_G], w_in[:, o_rg:o_rg + V_G],
                          w_in[:, o_lr:o_lr + GLA_GATE_RANK], lr_pad], axis=1).astype(BF16)
    wq, wk, wv = (w_in[:, o:o + SB_W].astype(BF16) for o in (o_qs, o_ks, o_vs))
    wgate = jnp.concatenate([w_gla_gate, jnp.zeros((LANES - GLA_GATE_RANK, Q_G), F32)], axis=0)

    slab, qs, ks, vs = _in_proj(x2d, row(g_mix), wg, wq, wk, wv, _pick(n, 512))
    o_gla = _gla(slab, wgate, row(b_gla_gate), row(g_gla_head), batch, seq, _pick(seq, 512))
    o_sb = _sb(qs, ks, vs, batch, seq, 2 * LANES)
    km, vm = _mem_kv(mem, row(g_mem_kv), w_mk.astype(BF16), w_mv.astype(BF16))
    w_out_bf = w_out.astype(BF16)
    x2, hf = _mid(x2d, o_gla, o_sb, w_out_bf[:V_G], w_out_bf[V_G:], row(g_mem_q), w_mq.astype(BF16), km, vm,
                  w_mo.astype(BF16), row(g_ffn), seq, _pick(seq, 1024))

    tm_r = _pick(n, 512)
    idx, gates, rank, cnt = _router(hf, w_router.T, b_router.reshape(-1, 1), tm_r)

    blk = 512
    counts = cnt[:, 0].astype(jnp.int32)
    padded = ((counts + blk - 1) // blk) * blk
    pad_end = jnp.cumsum(padded)
    pad_start = pad_end - padded
    first = jnp.sum(jnp.where(idx[..., None] == jnp.arange(N_EXPERTS), pad_start, 0), axis=-1)
    slot = first + rank
    n_steps = -(-(n * TOP_K + N_EXPERTS * (blk - 1)) // blk)
    m_pad = n_steps * blk
    n_valid = pad_end[-1] // blk
    blk_idx = jnp.arange(n_steps, dtype=jnp.int32)
    used = jnp.minimum(blk_idx, n_valid - 1)
    blk_exp = jnp.minimum(jnp.sum(used[:, None] * blk >= pad_end[None, :], axis=1), N_EXPERTS - 1).astype(jnp.int32)
    run_first = jnp.logical_and(blk_idx < n_valid, blk_exp != jnp.concatenate([blk_exp[:1] - 1, blk_exp[:-1]]))
    run_first = run_first.astype(jnp.int32)
    run_slot = (jnp.cumsum(run_first) - 1) % 2
    e_ids = jnp.arange(N_EXPERTS, dtype=jnp.int32)
    later = jnp.where(jnp.logical_and(padded[None, :] > 0, e_ids[None, :] > e_ids[:, None]), e_ids[None, :], N_EXPERTS)
    following = jnp.min(later, axis=1)
    run_next = jnp.where(following == N_EXPERTS, -1, following)[blk_exp]
    n_valid = n_valid.reshape(1)

    xs = _fill(pad_start + counts, padded - counts, n_valid, _scatter_rows(hf, slot, m_pad), blk)
    ys = _experts(blk_exp, run_first, run_slot.astype(jnp.int32), run_next.astype(jnp.int32), n_valid, xs,
                  w_gu, b_gu, w_down, b_down, blk)
    rows = _gather_rows(ys, slot.reshape(-1))
    return _combine(rows, x2, gates.T, row(g_out), _pick(n, 256))


def kernel(x, mem, g_mix, w_in, w_gla_gate, b_gla_gate, g_gla_head, w_out, g_mem_q, g_mem_kv, w_mq, w_mk, w_mv,
           w_mo, g_ffn, w_router, b_router, w_gu, b_gu, w_down, b_down, g_final):
    batch, seq, d = x.shape
    depth = g_mix.shape[0]
    assert depth == 1, "the final rmsnorm is fused into the single layer's combine step"
    y = _layer(x.reshape(batch * seq, d), mem, batch, seq, g_mix[0], w_in[0], w_gla_gate[0], b_gla_gate[0],
               g_gla_head[0], w_out[0], g_mem_q[0], g_mem_kv[0], w_mq[0], w_mk[0], w_mv[0], w_mo[0], g_ffn[0],
               w_router[0], b_router[0], w_gu[0], b_gu[0], w_down[0], b_down[0], g_final)
    return y.reshape(batch, seq, d)
```
